```python
import math
import jax, jax.numpy as jnp
from jax import lax
import numpy as np


D_MODEL = 2048
BATCH = 4
SEQ = 2048
DEPTH = 4

N_MIXERS = 3
NORM_EPS = 1e-6

RWKV_WIDTH = D_MODEL
RWKV_HEAD = 64
RWKV_HEADS = RWKV_WIDTH // RWKV_HEAD
RWKV_DECAY_LORA = max(32, int(round(1.8 * D_MODEL ** 0.5 / 32)) * 32)
RWKV_ICLR_LORA = max(32, int(round(1.8 * D_MODEL ** 0.5 / 32)) * 32)
RWKV_GN_EPS = 64e-5
RWKV_N_SHIFT = 6

GLA_HEADS = 4
GLA_KEY_WIDTH = D_MODEL // 2
GLA_VALUE_WIDTH = D_MODEL
GLA_HEAD_K = GLA_KEY_WIDTH // GLA_HEADS
GLA_HEAD_V = GLA_VALUE_WIDTH // GLA_HEADS
GLA_GATE_RANK = 16
GLA_GATE_TAU = 16.0
GLA_CHUNK = 64
GLA_IN_WIDTH = 2 * GLA_KEY_WIDTH + 2 * GLA_VALUE_WIDTH + GLA_GATE_RANK

SSM_WIDTH = 2 * D_MODEL
SSM_HEADDIM = 64
SSM_HEADS = SSM_WIDTH // SSM_HEADDIM
SSM_STATE = 128
SSM_GROUPS = 8
SSM_CONV = 4
SSM_CHUNK = 128
SSM_NORM_EPS = 1e-5
SSM_CONV_WIDTH = SSM_WIDTH + 2 * SSM_GROUPS * SSM_STATE
SSM_IN_WIDTH = SSM_WIDTH + SSM_CONV_WIDTH + SSM_HEADS

N_RWKV_LAYERS = (DEPTH + 2) // 3
N_GLA_LAYERS = (DEPTH + 1) // 3
N_SSD_LAYERS = DEPTH // 3

kernel_name = 'hybrid_rwkv7_gla_ssd_adaln'


def rms_normalize(x, gain, eps):
    xf = x.astype(jnp.float32)
    y = xf * lax.rsqrt(jnp.mean(xf * xf, axis=-1, keepdims=True) + eps) * gain.astype(jnp.float32)
    return y.astype(x.dtype)


def causal_depthwise_conv(x, w, b):
    k_width, ch = w.shape
    y = lax.conv_general_dilated(x, w[:, None, :], window_strides=(1,), padding=((k_width - 1, 0),),
                                 dimension_numbers=('NWC', 'WIO', 'NWC'), feature_group_count=ch)
    return y + b


def rwkv7_mixer(h, mu, w_in, dec_w1, dec_w2, dec_w0, iclr_w1, iclr_w2, iclr_w0,
                k_k, k_a, r_k, gn_w, gn_b, w_out):
    bsz, s, _ = h.shape
    H, N, W = RWKV_HEADS, RWKV_HEAD, RWKV_WIDTH
    h_prev = jnp.pad(h, ((0, 0), (1, 0), (0, 0)))[:, :-1]
    xs = h[None] + (h_prev - h)[None] * mu[:, None, None, :]
    r, k, v, g = jnp.einsum('cbsd,dcw->cbsw', xs[:4], w_in.reshape(D_MODEL, 4, W))
    w_log = -jax.nn.softplus(-(dec_w0 + jnp.tanh(xs[4] @ dec_w1) @ dec_w2)) - 0.5
    decay = jnp.exp(-jnp.exp(w_log.astype(jnp.float32)))
    a = jax.nn.sigmoid(iclr_w0 + (xs[5] @ iclr_w1) @ iclr_w2)
    heads = lambda t: t.reshape(bsz, s, H, N)
    kk = heads(k * k_k)
    kk = kk / jnp.maximum(jnp.sqrt(jnp.sum(kk * kk, axis=-1, keepdims=True)), 1e-12)
    k = k * (1 + (a - 1) * k_a)
    r, k, v, a, decay = heads(r), heads(k), heads(v), heads(a), heads(decay)

    def step(state, inp):
        r_t, k_t, v_t, w_t, kk_t, a_t = inp
        removal = jnp.einsum('bhvk,bhk->bhv', state, kk_t)
        state = (state * w_t[:, :, None, :]
                 - jnp.einsum('bhv,bhk->bhvk', removal, kk_t * a_t)
                 + jnp.einsum('bhv,bhk->bhvk', v_t, k_t))
        return state, jnp.einsum('bhvk,bhk->bhv', state, r_t)

    tm = lambda t: jnp.moveaxis(t, 1, 0)
    state0 = jnp.zeros((bsz, H, N, N), jnp.float32)
    _, y = lax.scan(step, state0, (tm(r), tm(k), tm(v), tm(decay), tm(kk), tm(a)))
    y = jnp.moveaxis(y, 0, 1).astype(jnp.float32)
    mean = jnp.mean(y, axis=-1, keepdims=True)
    var = jnp.mean(jnp.square(y - mean), axis=-1, keepdims=True)
    y = ((y - mean) * lax.rsqrt(var + RWKV_GN_EPS)).reshape(bsz, s, W) * gn_w + gn_b
    bonus = jnp.sum(r * k * r_k, axis=-1, keepdims=True) * v
    y = (y + bonus.reshape(bsz, s, W)).astype(h.dtype)
    return (y * jax.nn.silu(g)) @ w_out


def gla_mixer(h, w_in, gate_w2, gate_b, head_g, w_out):
    bsz, s, _ = h.shape
    H, DK, DV, C = GLA_HEADS, GLA_HEAD_K, GLA_HEAD_V, GLA_CHUNK
    nc = s // C
    kw, vw = GLA_KEY_WIDTH, GLA_VALUE_WIDTH
    proj = h @ w_in
    q, k, v, g, low = jnp.split(proj, [kw, 2 * kw, 2 * kw + vw, 2 * kw + 2 * vw], axis=-1)
    log_alpha = jax.nn.log_sigmoid((low @ gate_w2 + gate_b).astype(jnp.float32)) / GLA_GATE_TAU

    def chunks(t, d):
        return t.reshape(bsz, nc, C, H, d).transpose(1, 0, 3, 2, 4)

    qc = chunks(q * DK ** -0.5, DK)
    kc = chunks(k, DK)
    vc = chunks(v, DV)
    bcum = jnp.cumsum(chunks(log_alpha, DK), axis=-2)
    b_ref = bcum[..., C // 2 - 1:C // 2, :]
    causal = jnp.tril(jnp.ones((C, C), dtype=bool))
    scores = jnp.einsum('nbhik,nbhjk->nbhij', qc * jnp.exp(bcum - b_ref), kc * jnp.exp(b_ref - bcum))
    o_intra = jnp.einsum('nbhij,nbhjv->nbhiv', jnp.where(causal, scores, 0.0), vc)
    b_last = bcum[..., -1:, :]
    q_from_start = qc * jnp.exp(bcum)
    k_to_end = kc * jnp.exp(b_last - bcum)

    def step(state, inp):
        q_n, k_n, v_n, dec_n = inp
        o_n = jnp.einsum('bhik,bhkv->bhiv', q_n, state)
        state = state * dec_n[:, :, 0, :, None] + jnp.einsum('bhjk,bhjv->bhkv', k_n, v_n)
        return state, o_n

    state0 = jnp.zeros((bsz, H, DK, DV), jnp.float32)
    _, o_inter = lax.scan(step, state0, (q_from_start, k_to_end, vc, jnp.exp(b_last)))
    o = (o_intra + o_inter).transpose(1, 0, 3, 2, 4).reshape(bsz, s, H, DV)
    o = rms_normalize(o.astype(h.dtype), head_g, NORM_EPS).reshape(bsz, s, vw)
    return (o * jax.nn.silu(g)) @ w_out


def ssd_mixer(h, w_in, conv_w, conv_b, dt_bias, a_log, d_skip, norm_g, w_out):
    bsz, s, _ = h.shape
    H, P, N, G, C = SSM_HEADS, SSM_HEADDIM, SSM_STATE, SSM_GROUPS, SSM_CHUNK
    R = H // G
    nc = s // C
    proj = h @ w_in
    z, xbc, dt = jnp.split(proj, [SSM_WIDTH, SSM_WIDTH + SSM_CONV_WIDTH], axis=-1)
    xbc = jax.nn.silu(causal_depthwise_conv(xbc, conv_w, conv_b))
    xs, bm, cm = jnp.split(xbc, [SSM_WIDTH, SSM_WIDTH + G * N], axis=-1)
    dt = jax.nn.softplus((dt + dt_bias).astype(jnp.float32))
    a = -jnp.exp(a_log.astype(jnp.float32))
    xh = xs.reshape(bsz, s, H, P)
    xc = (xh * dt[..., None]).reshape(bsz, nc, C, G, R, P)
    bc = bm.reshape(bsz, nc, C, G, N)
    cc = cm.reshape(bsz, nc, C, G, N)
    acum = jnp.cumsum((dt * a).reshape(bsz, nc, C, G, R), axis=2)
    causal = jnp.tril(jnp.ones((C, C), dtype=bool))[:, :, None, None]
    seg = acum[:, :, :, None] - acum[:, :, None, :]
    decay_ij = jnp.exp(jnp.where(causal, seg, -jnp.inf))
    cb = jnp.einsum('bnigs,bnjgs->bnijg', cc, bc)
    y_diag = jnp.einsum('bnijgr,bnjgrp->bnigrp', cb[..., None] * decay_ij, xc)
    decay_to_end = jnp.exp(acum[:, :, -1:] - acum)
    chunk_states = jnp.einsum('bnjgs,bnjgrp->bngrsp', bc, xc * decay_to_end[..., None])
    chunk_decay = jnp.exp(acum[:, :, -1])

    def step(state, inp):
        st_n, dec_n = inp
        return state * dec_n[..., None, None] + st_n, state

    state0 = jnp.zeros((bsz, G, R, N, P), jnp.float32)
    _, prev = lax.scan(step, state0, (jnp.moveaxis(chunk_states, 1, 0), jnp.moveaxis(chunk_decay, 1, 0)))
    prev = jnp.moveaxis(prev, 0, 1)
    y_off = jnp.einsum('bnigs,bngrsp->bnigrp', cc, prev) * jnp.exp(acum)[..., None]
    y = (y_diag + y_off).reshape(bsz, s, H, P) + d_skip[:, None] * xh
    y = (y.reshape(bsz, s, SSM_WIDTH) * jax.nn.silu(z)).astype(h.dtype)
    y = rms_normalize(y.reshape(bsz, s, G, SSM_WIDTH // G), norm_g.reshape(G, SSM_WIDTH // G), SSM_NORM_EPS)
    return y.reshape(bsz, s, SSM_WIDTH) @ w_out


def setup_inputs(seed: int = 0) -> dict:
    key = jax.random.key(seed)
    ks = iter(jax.random.split(key, 40))

    def nrm(shape, scale):
        return scale * jax.random.normal(next(ks), shape, jnp.float32)

    def unif(shape, lo, hi):
        return jax.random.uniform(next(ks), shape, jnp.float32, lo, hi)

    D = D_MODEL
    LA, LB, LC = N_RWKV_LAYERS, N_GLA_LAYERS, N_SSD_LAYERS
    dt0 = jnp.exp(unif((LC, SSM_HEADS), math.log(1e-3), math.log(1e-1)))
    return {
        'x': nrm((BATCH, SEQ, D), 1.0),
        'c': nrm((BATCH, D), 1.0),
        'ada_w': nrm((DEPTH, D, 3 * D), 0.5 * D ** -0.5),
        'ada_b': nrm((DEPTH, 3 * D), 0.02),
        'norm_g': 1.0 + nrm((DEPTH, D), 0.02),
        'final_g': 1.0 + nrm((D,), 0.02),
        'rwkv_mu': unif((LA, RWKV_N_SHIFT, D), 0.0, 1.0),
        'rwkv_w_in': nrm((LA, D, 4 * RWKV_WIDTH), D ** -0.5),
        'rwkv_dec_w1': nrm((LA, D, RWKV_DECAY_LORA), D ** -0.5),
        'rwkv_dec_w2': nrm((LA, RWKV_DECAY_LORA, RWKV_WIDTH), 0.5 * RWKV_DECAY_LORA ** -0.5),
        'rwkv_dec_w0': unif((LA, RWKV_WIDTH), -6.5, -1.5),
        'rwkv_iclr_w1': nrm((LA, D, RWKV_ICLR_LORA), D ** -0.5),
        'rwkv_iclr_w2': nrm((LA, RWKV_ICLR_LORA, RWKV_WIDTH), 0.5 * RWKV_ICLR_LORA ** -0.5),
        'rwkv_iclr_w0': nrm((LA, RWKV_WIDTH), 0.1),
        'rwkv_k_k': 0.85 + nrm((LA, RWKV_WIDTH), 0.05),
        'rwkv_k_a': 1.0 + nrm((LA, RWKV_WIDTH), 0.05),
        'rwkv_r_k': nrm((LA, RWKV_HEADS, RWKV_HEAD), 0.1),
        'rwkv_gn_w': 1.0 + nrm((LA, RWKV_WIDTH), 0.02),
        'rwkv_gn_b': nrm((LA, RWKV_WIDTH), 0.02),
        'rwkv_w_out': nrm((LA, RWKV_WIDTH, D), RWKV_WIDTH ** -0.5),
        'gla_w_in': nrm((LB, D, GLA_IN_WIDTH), D ** -0.5),
        'gla_gate_w2': nrm((LB, GLA_GATE_RANK, GLA_KEY_WIDTH), GLA_GATE_RANK ** -0.5),
        'gla_gate_b': nrm((LB, GLA_KEY_WIDTH), 0.1),
        'gla_head_g': 1.0 + nrm((LB, GLA_HEAD_V), 0.02),
        'gla_w_out': nrm((LB, GLA_VALUE_WIDTH, D), GLA_VALUE_WIDTH ** -0.5),
        'ssd_w_in': nrm((LC, D, SSM_IN_WIDTH), D ** -0.5),
        'ssd_conv_w': nrm((LC, SSM_CONV, SSM_CONV_WIDTH), SSM_CONV ** -0.5),
        'ssd_conv_b': nrm((LC, SSM_CONV_WIDTH), 0.02),
        'ssd_dt_bias': dt0 + jnp.log(-jnp.expm1(-dt0)),
        'ssd_a_log': jnp.log(unif((LC, SSM_HEADS), 1.0, 16.0)),
        'ssd_d': 1.0 + nrm((LC, SSM_HEADS), 0.1),
        'ssd_norm_g': 1.0 + nrm((LC, SSM_WIDTH), 0.02),
        'ssd_w_out': nrm((LC, SSM_WIDTH, D), SSM_WIDTH ** -0.5),
    }


def reference(x, c, ada_w, ada_b, norm_g, final_g,
              rwkv_mu, rwkv_w_in, rwkv_dec_w1, rwkv_dec_w2, rwkv_dec_w0, rwkv_iclr_w1, rwkv_iclr_w2,
              rwkv_iclr_w0, rwkv_k_k, rwkv_k_a, rwkv_r_k, rwkv_gn_w, rwkv_gn_b, rwkv_w_out,
              gla_w_in, gla_gate_w2, gla_gate_b, gla_head_g, gla_w_out,
              ssd_w_in, ssd_conv_w, ssd_conv_b, ssd_dt_bias, ssd_a_log, ssd_d, ssd_norm_g, ssd_w_out):
    c_act = jax.nn.silu(c)
    for i in range(DEPTH):
        mod = jnp.einsum('bd,de->be', c_act, ada_w[i]) + ada_b[i]
        shift, scale, gate = jnp.split(mod[:, None, :], 3, axis=-1)
        h = rms_normalize(x, norm_g[i], NORM_EPS) * (1 + scale) + shift
        kind, j = i % N_MIXERS, i // N_MIXERS
        if kind == 0:
            out = rwkv7_mixer(h, rwkv_mu[j], rwkv_w_in[j], rwkv_dec_w1[j], rwkv_dec_w2[j], rwkv_dec_w0[j],
                              rwkv_iclr_w1[j], rwkv_iclr_w2[j], rwkv_iclr_w0[j], rwkv_k_k[j], rwkv_k_a[j],
                              rwkv_r_k[j], rwkv_gn_w[j], rwkv_gn_b[j], rwkv_w_out[j])
        elif kind == 1:
            out = gla_mixer(h, gla_w_in[j], gla_gate_w2[j], gla_gate_b[j], gla_head_g[j], gla_w_out[j])
        else:
            out = ssd_mixer(h, ssd_w_in[j], ssd_conv_w[j], ssd_conv_b[j], ssd_dt_bias[j], ssd_a_log[j],
                            ssd_d[j], ssd_norm_g[j], ssd_w_out[j])
        x = x + (gate * out).astype(x.dtype)
    return rms_normalize(x, final_g, NORM_EPS)
```

```python
import functools
import math

import jax
import jax.numpy as jnp
from jax import lax
from jax.experimental import pallas as pl
from jax.experimental.pallas import tpu as pltpu

F32 = jnp.float32
BF16 = jnp.bfloat16

D_MODEL = 2048
DEPTH = 4
N_MIXERS = 3
NORM_EPS = 1e-6

RWKV_HEAD = 64
RWKV_LORA = 96
RWKV_GN_EPS = 64e-5

GLA_HEADS = 4
GLA_KEY_WIDTH = D_MODEL // 2
GLA_VALUE_WIDTH = D_MODEL
GLA_HEAD_K = GLA_KEY_WIDTH // GLA_HEADS
GLA_HEAD_V = GLA_VALUE_WIDTH // GLA_HEADS
GLA_GATE_RANK = 16
GLA_GATE_TAU = 16.0

SSM_WIDTH = 2 * D_MODEL
SSM_HEADDIM = 64
SSM_HEADS = SSM_WIDTH // SSM_HEADDIM
SSM_STATE = 128
SSM_GROUPS = 8
SSM_CONV = 4
SSM_NORM_EPS = 1e-5
SSM_BC_WIDTH = SSM_GROUPS * SSM_STATE
SSM_CONV_WIDTH = SSM_WIDTH + 2 * SSM_BC_WIDTH

LANES = 128
SUBLANES = 8
VMEM_LIMIT_BYTES = 52 * 1024 * 1024

RWKV_CHUNK = 64
RWKV_CHUNKS_PER_STEP = 4
GLA_BLOCK = 128
SSD_CHUNK = 128

RWKV_PASSES = 1
GATE_PASSES = 3

ROW_TILE = 512


def _cparams(sem):
    return pltpu.CompilerParams(dimension_semantics=sem, vmem_limit_bytes=VMEM_LIMIT_BYTES)


def _dot(a, b, *, nt=False, passes=1):
    dims = (((1,), (1,)), ((), ())) if nt else (((1,), (0,)), ((), ()))

    def d(x, y):
        return lax.dot_general(x, y, dims, preferred_element_type=F32)

    ah = a.astype(BF16)
    bh = b.astype(BF16)
    if passes == 1:
        return d(ah, bh)
    al = (a.astype(F32) - ah.astype(F32)).astype(BF16)
    bl = (b.astype(F32) - bh.astype(F32)).astype(BF16)
    return d(ah, bh) + (d(ah, bl) + d(al, bh))


def _dot_exact_lhs(m_bf16, x):
    x1 = x.astype(BF16)
    r1 = x - x1.astype(F32)
    x2 = r1.astype(BF16)
    x3 = (r1 - x2.astype(F32)).astype(BF16)

    def d(y):
        return jnp.dot(m_bf16, y, preferred_element_type=F32)

    return d(x1) + (d(x2) + d(x3))


def _silu(x):
    return x * jax.nn.sigmoid(x)


def _softplus(x):
    return jnp.maximum(x, 0.0) + jnp.log1p(jnp.exp(-jnp.abs(x)))


def _iota2(shape, dim):
    return lax.broadcasted_iota(jnp.int32, shape, dim)


def _ada_kernel(c_ref, w_ref, b_ref, o_ref):
    c = c_ref[...]
    o_ref[...] = _dot(_silu(c), w_ref[...]) + b_ref[...]


def _ada_mod(c_pad, ada_w, ada_b, tn=1024):
    depth, d, n = ada_w.shape
    rows = c_pad.shape[0]
    return pl.pallas_call(
        _ada_kernel,
        grid=(depth, n // tn),
        in_specs=[
            pl.BlockSpec((rows, d), lambda l, j: (0, 0)),
            pl.BlockSpec((None, d, tn), lambda l, j: (l, 0, j)),
            pl.BlockSpec((None, 1, tn), lambda l, j: (l, 0, j)),
        ],
        out_specs=pl.BlockSpec((None, rows, tn), lambda l, j: (l, 0, j)),
        out_shape=jax.ShapeDtypeStruct((depth, rows, n), F32),
        compiler_params=_cparams(("arbitrary", "arbitrary")),
        name="ada_mod",
    )(c_pad, ada_w, ada_b.reshape(depth, 1, n))


def _prenorm(x, g, mod):
    ms = jnp.mean(x * x, axis=-1, keepdims=True)
    return x * lax.rsqrt(ms + NORM_EPS) * g * (1.0 + mod[1:2, :]) + mod[0:1, :]


def _inproj_kernel(x_ref, g_ref, mod_ref, w_ref, ws_ref, o_ref, os_ref, h_scr):
    @pl.when(pl.program_id(1) == 0)
    def _():
        h = _prenorm(x_ref[...], g_ref[...], mod_ref[...]).astype(BF16)
        h_scr[...] = h
        os_ref[...] = jnp.dot(h, ws_ref[...], preferred_element_type=F32)

    o_ref[...] = jnp.dot(h_scr[...], w_ref[...], preferred_element_type=F32).astype(o_ref.dtype)


def _inproj(x, g, mod_l, w_main, w_side, seq, *, tm, tn, out_dtype):
    t, d = x.shape
    n = w_main.shape[1]
    ns = w_side.shape[1]
    tm = min(tm, seq)
    tiles_per_seq = seq // tm
    return pl.pallas_call(
        _inproj_kernel,
        grid=(t // tm, n // tn),
        in_specs=[
            pl.BlockSpec((tm, d), lambda i, j: (i, 0)),
            pl.BlockSpec((1, d), lambda i, j: (0, 0)),
            pl.BlockSpec((None, 3, d), lambda i, j: (i // tiles_per_seq, 0, 0)),
            pl.BlockSpec((d, tn), lambda i, j: (0, j)),
            pl.BlockSpec((d, ns), lambda i, j: (0, 0)),
        ],
        out_specs=[
            pl.BlockSpec((tm, tn), lambda i, j: (i, j)),
            pl.BlockSpec((tm, ns), lambda i, j: (i, 0)),
        ],
        out_shape=[
            jax.ShapeDtypeStruct((t, n), out_dtype),
            jax.ShapeDtypeStruct((t, ns), F32),
        ],
        scratch_shapes=[pltpu.VMEM((tm, d), BF16)],
        compiler_params=_cparams(("arbitrary", "arbitrary")),
        name="inproj",
    )(x, g, mod_l, w_main, w_side)


def _outproj_kernel(z_ref, w_ref, x_ref, mod_ref, o_ref):
    acc = jnp.dot(z_ref[...].astype(BF16), w_ref[...], preferred_element_type=F32)
    o_ref[...] = x_ref[...] + mod_ref[2:3, :] * acc


def _outproj(z, w, x, mod_l, seq, *, tm, tn):
    t, kd = z.shape
    n = w.shape[1]
    tm = min(tm, seq)
    tiles_per_seq = seq // tm
    return pl.pallas_call(
        _outproj_kernel,
        grid=(n // tn, t // tm),
        in_specs=[
            pl.BlockSpec((tm, kd), lambda j, i: (i, 0)),
            pl.BlockSpec((kd, tn), lambda j, i: (0, j)),
            pl.BlockSpec((tm, tn), lambda j, i: (i, j)),
            pl.BlockSpec((None, 3, tn), lambda j, i: (i // tiles_per_seq, 0, j)),
        ],
        out_specs=pl.BlockSpec((tm, tn), lambda j, i: (i, j)),
        out_shape=jax.ShapeDtypeStruct((t, n), F32),
        compiler_params=_cparams(("arbitrary", "arbitrary")),
        name="outproj",
    )(z, w, x, mod_l)


def _final_norm_kernel(x_ref, g_ref, o_ref):
    x = x_ref[...]
    ms = jnp.mean(x * x, axis=-1, keepdims=True)
    o_ref[...] = x * lax.rsqrt(ms + NORM_EPS) * g_ref[...]


def _final_norm(x, g, *, tm=512):
    t, d = x.shape
    tm = min(tm, t)
    return pl.pallas_call(
        _final_norm_kernel,
        grid=(t // tm,),
        in_specs=[pl.BlockSpec((tm, d), lambda i: (i, 0)), pl.BlockSpec((1, d), lambda i: (0, 0))],
        out_specs=pl.BlockSpec((tm, d), lambda i: (i, 0)),
        out_shape=jax.ShapeDtypeStruct((t, d), F32),
        compiler_params=_cparams(("arbitrary",)),
        name="final_norm",
    )(x, g)


def _rwkv_inproj_kernel(x_ref, xh_ref, g_ref, mod_ref, mu_ref, w_ref, w1_ref, o_ref, lora_ref, xs_scr,
                        *, tiles_per_seq, col_tiles):
    i = pl.program_id(0)
    j = pl.program_id(1)

    @pl.when(j == 0)
    def _():
        g = g_ref[...]
        mod = mod_ref[...]
        h = _prenorm(x_ref[...], g, mod)
        hp8 = _prenorm(xh_ref[...], g, mod)
        first = (i % tiles_per_seq) == 0
        prev_row = jnp.where(first, 0.0, hp8[SUBLANES - 1:SUBLANES, :])
        rolled = pltpu.roll(h, 1, 0)
        row0 = _iota2(h.shape, 0) == 0
        dh = jnp.where(row0, prev_row, rolled) - h
        mu = mu_ref[...]
        for c in range(4):
            xs_scr[c] = (h + dh * mu[c:c + 1, :]).astype(BF16)
        xw = (h + dh * mu[4:5, :]).astype(BF16)
        xa = (h + dh * mu[5:6, :]).astype(BF16)
        w1 = w1_ref[...]
        dec_h = jnp.tanh(jnp.dot(xw, w1[:, :LANES], preferred_element_type=F32))
        icl_h = jnp.dot(xa, w1[:, LANES:], preferred_element_type=F32)
        lora_ref[...] = jnp.concatenate([dec_h, icl_h], axis=1)

    o_ref[...] = jnp.dot(xs_scr[j // col_tiles], w_ref[...], preferred_element_type=F32).astype(o_ref.dtype)


def _rwkv_inproj(x, g, mod_l, mu, w_in, w1, seq, *, tm, tn, out_dtype):
    t, d = x.shape
    n = w_in.shape[1]
    tm = min(tm, seq)
    tiles_per_seq = seq // tm
    halo_blocks = tm // SUBLANES
    kern = functools.partial(_rwkv_inproj_kernel, tiles_per_seq=tiles_per_seq, col_tiles=d // tn)
    return pl.pallas_call(
        kern,
        grid=(t // tm, n // tn),
        in_specs=[
            pl.BlockSpec((tm, d), lambda i, j: (i, 0)),
            pl.BlockSpec((SUBLANES, d), lambda i, j: (jnp.maximum(i * halo_blocks - 1, 0), 0)),
            pl.BlockSpec((1, d), lambda i, j: (0, 0)),
            pl.BlockSpec((None, 3, d), lambda i, j: (i // tiles_per_seq, 0, 0)),
            pl.BlockSpec((6, d), lambda i, j: (0, 0)),
            pl.BlockSpec((d, tn), lambda i, j: (0, j)),
            pl.BlockSpec((d, 2 * LANES), lambda i, j: (0, 0)),
        ],
        out_specs=[
            pl.BlockSpec((tm, tn), lambda i, j: (i, j)),
            pl.BlockSpec((tm, 2 * LANES), lambda i, j: (i, 0)),
        ],
        out_shape=[
            jax.ShapeDtypeStruct((t, n), out_dtype),
            jax.ShapeDtypeStruct((t, 2 * LANES), F32),
        ],
        scratch_shapes=[pltpu.VMEM((4, tm, d), BF16)],
        compiler_params=_cparams(("arbitrary", "arbitrary")),
        name="rwkv_inproj",
    )(x, x, g, mod_l, mu, w_in, w1)


def _unit_lower_inverse(a, eye, blk8, offdiag, passes):
    a8 = jnp.where(blk8, a, 0.0)
    a2 = _dot(a8, a8, passes=passes)
    a4 = _dot(a2, a2, passes=passes)
    inv = _dot(_dot(eye + a8, eye + a2, passes=passes), eye + a4, passes=passes)
    for m in offdiag:
        lb = jnp.where(m, a, 0.0)
        inv = inv + _dot(inv, _dot(lb, inv, passes=passes), passes=passes)
    return inv


def _rwkv_scan_kernel(r_ref, k_ref, v_ref, g_ref, lora_ref, dw2_ref, iw2_ref, dw0_ref, iw0_ref,
                      kk_ref, ka_ref, rk_ref, gnw_ref, gnb_ref, z_ref, s_scr, cum_scr, y_scr,
                      *, chunks, passes):
    L = RWKV_CHUNK
    N = RWKV_HEAD
    R = L * chunks

    @pl.when(pl.program_id(2) == 0)
    def _():
        s_scr[...] = jnp.zeros_like(s_scr)

    r = r_ref[...].astype(F32)
    k = k_ref[...].astype(F32)
    v = v_ref[...].astype(F32)
    lora = lora_ref[...]

    lane = _iota2((1, LANES), 1)
    m0 = lane < N
    mf0 = m0.astype(F32)
    mf1 = 1.0 - mf0

    def headsum(x):
        s0 = jnp.sum(jnp.where(m0, x, 0.0), axis=-1, keepdims=True)
        s1 = jnp.sum(jnp.where(m0, 0.0, x), axis=-1, keepdims=True)
        return jnp.where(m0, s0, s1)

    def stack(x):
        return jnp.concatenate([x * mf0, x * mf1], axis=0)

    dec = dw0_ref[...] + _dot(lora[:, :LANES], dw2_ref[...], passes=GATE_PASSES)
    w_log = -_softplus(-dec) - 0.5
    lw = -jnp.exp(w_log)
    a = jax.nn.sigmoid(iw0_ref[...] + _dot(lora[:, LANES:], iw2_ref[...], passes=GATE_PASSES))

    kkr = k * kk_ref[...]
    kk = kkr / jnp.maximum(jnp.sqrt(headsum(kkr * kkr)), 1e-12)
    k2 = k * (1.0 + (a - 1.0) * ka_ref[...])
    av = -kk
    bv = kk * a

    ri = _iota2((R, R), 0)
    ci = _iota2((R, R), 1)
    tri = jnp.where((ri // L == ci // L) & (ci <= ri), 1.0, 0.0).astype(BF16)
    cum = _dot_exact_lhs(tri, lw)
    cum_scr[...] = cum
    cumex = cum - lw

    P = 2 * L
    rp = _iota2((P, P), 0)
    cp = _iota2((P, P), 1)
    strict = (rp % L) > (cp % L)
    incl = (rp % L) >= (cp % L)
    eye = jnp.where(rp == cp, 1.0, 0.0)
    blk8 = (rp // 8) == (cp // 8)
    offdiag = [((rp // (2 * b)) == (cp // (2 * b))) & ((rp // b) != (cp // b)) for b in (8, 16, 32)]

    for c in range(chunks):
        sl = slice(c * L, (c + 1) * L)
        cref = cum_scr[pl.ds(c * L + L // 2 - 1, 1), :]
        clast = cum_scr[pl.ds(c * L + L - 1, 1), :]
        cum_c = cum[sl]
        cumex_c = cumex[sl]
        r_c, k_c, v_c, av_c, bv_c = r[sl], k2[sl], v[sl], av[sl], bv[sl]

        e_out = jnp.exp(cref - cum_c)
        e_end = jnp.exp(clast - cum_c)
        lhs1 = jnp.concatenate([stack(av_c * jnp.exp(cumex_c - cref)), stack(r_c * jnp.exp(cum_c - cref))], axis=0)
        rhs1 = jnp.concatenate([stack(bv_c * e_out), stack(k_c * e_out)], axis=0)
        x1 = _dot(lhs1, rhs1, nt=True, passes=passes)
        a_ab = jnp.where(strict, x1[:P, :P], 0.0)
        a_ak = jnp.where(strict, x1[:P, P:], 0.0)
        a_rb = jnp.where(incl, x1[P:, :P], 0.0)
        a_rk = jnp.where(incl, x1[P:, P:], 0.0)
        tinv = _unit_lower_inverse(a_ab, eye, blk8, offdiag, passes)

        v_st = stack(v_c)
        m1 = _dot(jnp.concatenate([a_ak, a_rk], axis=0), v_st, passes=passes)
        akv = m1[:P]
        yv_st = m1[P:]
        a0_st = stack(av_c * jnp.exp(cumex_c))
        m2 = _dot(tinv, jnp.concatenate([a0_st, akv], axis=1), passes=passes)
        w_st = m2[:, :LANES]
        uv_st = m2[:, LANES:]
        r0_st = stack(r_c * jnp.exp(cum_c))
        bk = jnp.concatenate([stack(bv_c * e_end), stack(k_c * e_end)], axis=0)

        s = s_scr[...]
        m3 = _dot(jnp.concatenate([w_st, r0_st], axis=0), s, nt=True, passes=passes)
        u_st = m3[:P] + uv_st
        y_st = m3[P:] + yv_st + _dot(a_rb, u_st, passes=passes)
        ut = _dot(s, w_st, nt=True, passes=passes) + uv_st.T
        s_scr[...] = s * jnp.exp(clast) + _dot(jnp.concatenate([ut, v_st.T], axis=1), bk, passes=passes)
        y_scr[sl, :] = y_st[:L] + y_st[L:]

    y = y_scr[...]
    mean = headsum(y) * (1.0 / N)
    yc = y - mean
    var = headsum(yc * yc) * (1.0 / N)
    yn = yc * lax.rsqrt(var + RWKV_GN_EPS) * gnw_ref[...] + gnb_ref[...]
    bonus = headsum(r * k2 * rk_ref[...]) * v
    z_ref[...] = ((yn + bonus) * _silu(g_ref[...].astype(F32))).astype(z_ref.dtype)


def _rwkv_scan(rkvg, lora, p, batch, seq, *, out_dtype):
    t = rkvg.shape[0]
    w = D_MODEL
    pairs = w // LANES
    chunks = min(RWKV_CHUNKS_PER_STEP, seq // RWKV_CHUNK)
    rows = RWKV_CHUNK * chunks
    steps = seq // rows

    def act(col0):
        return pl.BlockSpec((rows, LANES), lambda b, h, n: (b * steps + n, col0 + h))

    def vec():
        return pl.BlockSpec((1, LANES), lambda b, h, n: (0, h))

    def w2():
        return pl.BlockSpec((LANES, LANES), lambda b, h, n: (0, h))

    kern = functools.partial(_rwkv_scan_kernel, chunks=chunks, passes=RWKV_PASSES)
    return pl.pallas_call(
        kern,
        grid=(batch, pairs, steps),
        in_specs=[act(0), act(pairs), act(2 * pairs), act(3 * pairs),
                  pl.BlockSpec((rows, 2 * LANES), lambda b, h, n: (b * steps + n, 0)),
                  w2(), w2(), vec(), vec(), vec(), vec(), vec(), vec(), vec()],
        out_specs=pl.BlockSpec((rows, LANES), lambda b, h, n: (b * steps + n, h)),
        out_shape=jax.ShapeDtypeStruct((t, w), out_dtype),
        scratch_shapes=[pltpu.VMEM((LANES, LANES), F32), pltpu.VMEM((rows, LANES), F32),
                        pltpu.VMEM((rows, LANES), F32)],
        compiler_params=_cparams(("arbitrary", "arbitrary", "arbitrary")),
        name="rwkv_scan",
    )(rkvg, rkvg, rkvg, rkvg, lora, p["dec_w2"], p["iclr_w2"], p["dec_w0"], p["iclr_w0"],
      p["k_k"], p["k_a"], p["r_k"], p["gn_w"], p["gn_b"])


def _gla_kernel(q_ref, k_ref, v_ref, g_ref, low_ref, w2_ref, b_ref, hg_ref, o_ref, s_scr):
    R = GLA_BLOCK
    Hf = R // 2

    @pl.when(pl.program_id(2) == 0)
    def _():
        s_scr[...] = jnp.zeros_like(s_scr)

    q = q_ref[...].astype(F32) * (GLA_HEAD_K ** -0.5)
    k = k_ref[...].astype(F32)
    v = v_ref[...].astype(F32)
    la = -_softplus(-(_dot(low_ref[...], w2_ref[...], passes=GATE_PASSES) + b_ref[...])) * (1.0 / GLA_GATE_TAU)
    ri = _iota2((R, R), 0)
    ci = _iota2((R, R), 1)
    tri = jnp.where(ci <= ri, 1.0, 0.0).astype(BF16)
    bcum = _dot_exact_lhs(tri, la)
    rh = _iota2((Hf, Hf), 0)
    ch = _iota2((Hf, Hf), 1)
    causal = ch <= rh

    bt, bb = bcum[:Hf], bcum[Hf:]
    qt, qb = q[:Hf], q[Hf:]
    kt, kb = k[:Hf], k[Hf:]
    vt, vb = v[:Hf], v[Hf:]
    ref_t = bcum[Hf // 2 - 1:Hf // 2]
    ref_m = bcum[Hf - 1:Hf]
    ref_b = bcum[Hf + Hf // 2 - 1:Hf + Hf // 2]
    last = bcum[R - 1:R]
    s00 = jnp.where(causal, _dot(qt * jnp.exp(bt - ref_t), kt * jnp.exp(ref_t - bt), nt=True), 0.0)
    s11 = jnp.where(causal, _dot(qb * jnp.exp(bb - ref_b), kb * jnp.exp(ref_b - bb), nt=True), 0.0)
    s10 = _dot(qb * jnp.exp(bb - ref_m), kt * jnp.exp(ref_m - bt), nt=True)
    o_top = _dot(s00, vt)
    o_bot = _dot(s10, vt) + _dot(s11, vb)
    st = s_scr[...]
    o = jnp.concatenate([o_top, o_bot], axis=0) + _dot(q * jnp.exp(bcum), st, nt=True)
    ke = k * jnp.exp(last - bcum)
    s_scr[...] = st * jnp.exp(last) + _dot(v.T, ke)

    ms = jnp.mean(o * o, axis=-1, keepdims=True)
    on = o * lax.rsqrt(ms + NORM_EPS) * hg_ref[...]
    o_ref[...] = (on * _silu(g_ref[...].astype(F32))).astype(o_ref.dtype)


def _gla_scan(proj, low, gate_w2, gate_b, head_g, batch, seq, *, out_dtype):
    t = proj.shape[0]
    R = GLA_BLOCK
    steps = seq // R
    kb = GLA_KEY_WIDTH // GLA_HEAD_K
    vb0 = 2 * GLA_KEY_WIDTH // GLA_HEAD_V
    nvb = GLA_VALUE_WIDTH // GLA_HEAD_V

    def row(b, h, n):
        return b * steps + n

    return pl.pallas_call(
        _gla_kernel,
        grid=(batch, GLA_HEADS, steps),
        in_specs=[
            pl.BlockSpec((R, GLA_HEAD_K), lambda b, h, n: (row(b, h, n), h)),
            pl.BlockSpec((R, GLA_HEAD_K), lambda b, h, n: (row(b, h, n), kb + h)),
            pl.BlockSpec((R, GLA_HEAD_V), lambda b, h, n: (row(b, h, n), vb0 + h)),
            pl.BlockSpec((R, GLA_HEAD_V), lambda b, h, n: (row(b, h, n), vb0 + nvb + h)),
            pl.BlockSpec((R, LANES), lambda b, h, n: (row(b, h, n), 0)),
            pl.BlockSpec((LANES, GLA_HEAD_K), lambda b, h, n: (0, h)),
            pl.BlockSpec((1, GLA_HEAD_K), lambda b, h, n: (0, h)),
            pl.BlockSpec((1, GLA_HEAD_V), lambda b, h, n: (0, 0)),
        ],
        out_specs=pl.BlockSpec((R, GLA_HEAD_V), lambda b, h, n: (row(b, h, n), h)),
        out_shape=jax.ShapeDtypeStruct((t, GLA_VALUE_WIDTH), out_dtype),
        scratch_shapes=[pltpu.VMEM((GLA_HEAD_V, GLA_HEAD_K), F32)],
        compiler_params=_cparams(("arbitrary", "arbitrary", "arbitrary")),
        name="gla_scan",
    )(proj, proj, proj, proj, low, gate_w2, gate_b, head_g)


def _conv_kernel(x_ref, xh_ref, w_ref, b_ref, o_ref, *, tiles_per_seq):
    i = pl.program_id(1)
    x = x_ref[...].astype(F32)
    halo = jnp.where((i % tiles_per_seq) == 0, 0.0, xh_ref[...].astype(F32))
    w = w_ref[...]
    rows = _iota2(x.shape, 0)
    acc = x * w[SSM_CONV - 1:SSM_CONV, :] + b_ref[...]
    for s in range(1, SSM_CONV):
        rolled = pltpu.roll(x, s, 0)
        hrow = pltpu.roll(halo, s, 0)
        head = jnp.concatenate([hrow, jnp.zeros((x.shape[0] - SUBLANES, x.shape[1]), F32)], axis=0)
        shifted = jnp.where(rows < s, head, rolled)
        acc = acc + shifted * w[SSM_CONV - 1 - s:SSM_CONV - s, :]
    o_ref[...] = _silu(acc).astype(o_ref.dtype)


def _ssd_conv(proj, conv_w, conv_b, seq, *, tm, tc, out_dtype):
    t = proj.shape[0]
    tm = min(tm, seq)
    tiles_per_seq = seq // tm
    col0 = SSM_WIDTH // tc
    halo_blocks = tm // SUBLANES
    kern = functools.partial(_conv_kernel, tiles_per_seq=tiles_per_seq)
    return pl.pallas_call(
        kern,
        grid=(SSM_CONV_WIDTH // tc, t // tm),
        in_specs=[
            pl.BlockSpec((tm, tc), lambda j, i: (i, col0 + j)),
            pl.BlockSpec((SUBLANES, tc), lambda j, i: (jnp.maximum(i * halo_blocks - 1, 0), col0 + j)),
            pl.BlockSpec((SSM_CONV, tc), lambda j, i: (0, j)),
            pl.BlockSpec((1, tc), lambda j, i: (0, j)),
        ],
        out_specs=pl.BlockSpec((tm, tc), lambda j, i: (i, j)),
        out_shape=jax.ShapeDtypeStruct((t, SSM_CONV_WIDTH), out_dtype),
        compiler_params=_cparams(("arbitrary", "arbitrary")),
        name="ssd_conv",
    )(proj, proj, conv_w, conv_b)


def _ssd_kernel(z_ref, xs_ref, bm_ref, cm_ref, dt_ref, dtb_ref, alog_ref, dsk_ref, ng_ref, o_ref, st_scr):
    C = SSD_CHUNK
    P = SSM_HEADDIM
    pairs_per_group = (SSM_HEADS // SSM_GROUPS) // 2
    gw = SSM_WIDTH // SSM_GROUPS

    @pl.when(pl.program_id(1) == 0)
    def _():
        st_scr[...] = jnp.zeros_like(st_scr)

    dt = _softplus(dt_ref[...] + dtb_ref[...])
    da = dt * (-jnp.exp(alog_ref[...]))
    ri = _iota2((C, C), 0)
    ci = _iota2((C, C), 1)
    causal = ci <= ri
    tri = jnp.where(causal, 1.0, 0.0).astype(BF16)
    acum = _dot_exact_lhs(tri, da)
    acum_t = acum.T
    alast = acum[C - 1:C, :]
    dsk = dsk_ref[...]

    lane = _iota2((1, LANES), 1)
    m0 = lane < P
    mf0 = m0.astype(F32)
    mf1 = 1.0 - mf0

    def pair_cols(x, h0):
        return jnp.where(m0, x[:, h0:h0 + 1], x[:, h0 + 1:h0 + 2])

    for g in range(SSM_GROUPS):
        bm = bm_ref[:, g * SSM_STATE:(g + 1) * SSM_STATE].astype(F32)
        cm = cm_ref[:, g * SSM_STATE:(g + 1) * SSM_STATE].astype(F32)
        cb = _dot(cm, bm, nt=True)
        bm_t = bm.T
        ys = []
        for pp in range(pairs_per_group):
            pidx = g * pairs_per_group + pp
            h0 = 2 * pidx
            cols = slice(pidx * LANES, (pidx + 1) * LANES)
            x_p = xs_ref[:, cols].astype(F32)
            dt_p = pair_cols(dt, h0)
            ac_p = pair_cols(acum, h0)
            al_p = pair_cols(alast, h0)
            xc = x_p * dt_p
            dec0 = jnp.where(causal, jnp.exp(acum[:, h0:h0 + 1] - acum_t[h0:h0 + 1, :]), 0.0)
            dec1 = jnp.where(causal, jnp.exp(acum[:, h0 + 1:h0 + 2] - acum_t[h0 + 1:h0 + 2, :]), 0.0)
            lhs = jnp.concatenate([cb * dec0, cb * dec1], axis=1)
            rhs = jnp.concatenate([xc * mf0, xc * mf1], axis=0)
            prev = st_scr[pidx]
            y = _dot(lhs, rhs) + _dot(cm, prev) * jnp.exp(ac_p) + pair_cols(dsk, h0) * x_p
            st_scr[pidx] = prev * jnp.exp(al_p) + _dot(bm_t, xc * jnp.exp(al_p - ac_p))
            ys.append(y * _silu(z_ref[:, cols].astype(F32)))
        yg = jnp.concatenate(ys, axis=1)
        ms = jnp.mean(yg * yg, axis=-1, keepdims=True)
        o_ref[:, g * gw:(g + 1) * gw] = (yg * lax.rsqrt(ms + SSM_NORM_EPS) * ng_ref[:, g * gw:(g + 1) * gw]).astype(o_ref.dtype)


def _ssd_scan(proj, xbc, dt, dt_bias, a_log, d_skip, norm_g, batch, seq, *, out_dtype):
    t = proj.shape[0]
    C = SSD_CHUNK
    steps = seq // C
    npairs = SSM_HEADS // 2

    def row(b, n):
        return b * steps + n

    bcol = SSM_WIDTH // SSM_BC_WIDTH
    return pl.pallas_call(
        _ssd_kernel,
        grid=(batch, steps),
        in_specs=[
            pl.BlockSpec((C, SSM_WIDTH), lambda b, n: (row(b, n), 0)),
            pl.BlockSpec((C, SSM_WIDTH), lambda b, n: (row(b, n), 0)),
            pl.BlockSpec((C, SSM_BC_WIDTH), lambda b, n: (row(b, n), bcol)),
            pl.BlockSpec((C, SSM_BC_WIDTH), lambda b, n: (row(b, n), bcol + 1)),
            pl.BlockSpec((C, LANES), lambda b, n: (row(b, n), 0)),
            pl.BlockSpec((1, LANES), lambda b, n: (0, 0)),
            pl.BlockSpec((1, LANES), lambda b, n: (0, 0)),
            pl.BlockSpec((1, LANES), lambda b, n: (0, 0)),
            pl.BlockSpec((1, SSM_WIDTH), lambda b, n: (0, 0)),
        ],
        out_specs=pl.BlockSpec((C, SSM_WIDTH), lambda b, n: (row(b, n), 0)),
        out_shape=jax.ShapeDtypeStruct((t, SSM_WIDTH), out_dtype),
        scratch_shapes=[pltpu.VMEM((npairs, SSM_STATE, LANES), F32)],
        compiler_params=_cparams(("arbitrary", "arbitrary")),
        name="ssd_scan",
    )(proj, xbc, xbc, xbc, dt, dt_bias, a_log, d_skip, norm_g)


def _pad_cols(w, n):
    return jnp.pad(w, ((0, 0), (0, n - w.shape[1])))


def _pad_rows(w, n):
    return jnp.pad(w, ((0, n - w.shape[0]), (0, 0)))


ACT_DTYPE = F32


def kernel(x, c, ada_w, ada_b, norm_g, final_g, rwkv_mu, rwkv_w_in, rwkv_dec_w1, rwkv_dec_w2, rwkv_dec_w0, rwkv_iclr_w1, rwkv_iclr_w2, rwkv_iclr_w0, rwkv_k_k, rwkv_k_a, rwkv_r_k, rwkv_gn_w, rwkv_gn_b, rwkv_w_out, gla_w_in, gla_gate_w2, gla_gate_b, gla_head_g, gla_w_out, ssd_w_in, ssd_conv_w, ssd_conv_b, ssd_dt_bias, ssd_a_log, ssd_d, ssd_norm_g, ssd_w_out):
    batch, seq, d = x.shape
    t = batch * seq
    xf = x.reshape(t, d)

    c_pad = jnp.pad(c, ((0, SUBLANES - batch % SUBLANES if batch % SUBLANES else 0), (0, 0)))
    mod = _ada_mod(c_pad, ada_w, ada_b)[:, :batch].reshape(DEPTH, batch, 3, d)

    for i in range(DEPTH):
        kind, j = i % N_MIXERS, i // N_MIXERS
        g = norm_g[i].reshape(1, d)
        mod_l = mod[i]
        if kind == 0:
            w1 = jnp.concatenate([_pad_cols(rwkv_dec_w1[j], LANES), _pad_cols(rwkv_iclr_w1[j], LANES)], axis=1)
            rkvg, lora = _rwkv_inproj(xf, g, mod_l, rwkv_mu[j], rwkv_w_in[j].astype(BF16), w1.astype(BF16), seq,
                                      tm=ROW_TILE, tn=1024, out_dtype=ACT_DTYPE)
            row = lambda v: v.reshape(1, -1)
            params = dict(dec_w2=_pad_rows(rwkv_dec_w2[j], LANES), iclr_w2=_pad_rows(rwkv_iclr_w2[j], LANES),
                          dec_w0=row(rwkv_dec_w0[j]), iclr_w0=row(rwkv_iclr_w0[j]), k_k=row(rwkv_k_k[j]),
                          k_a=row(rwkv_k_a[j]), r_k=row(rwkv_r_k[j]), gn_w=row(rwkv_gn_w[j]), gn_b=row(rwkv_gn_b[j]))
            z = _rwkv_scan(rkvg, lora, params, batch, seq, out_dtype=ACT_DTYPE)
            xf = _outproj(z, rwkv_w_out[j].astype(BF16), xf, mod_l, seq, tm=ROW_TILE, tn=1024)
        elif kind == 1:
            nmain = 2 * GLA_KEY_WIDTH + 2 * GLA_VALUE_WIDTH
            w = gla_w_in[j]
            proj, low = _inproj(xf, g, mod_l, w[:, :nmain].astype(BF16), _pad_cols(w[:, nmain:], LANES).astype(BF16),
                                seq, tm=ROW_TILE, tn=1024, out_dtype=ACT_DTYPE)
            z = _gla_scan(proj, low, _pad_rows(gla_gate_w2[j], LANES), gla_gate_b[j].reshape(1, -1),
                          gla_head_g[j].reshape(1, -1), batch, seq, out_dtype=ACT_DTYPE)
            xf = _outproj(z, gla_w_out[j].astype(BF16), xf, mod_l, seq, tm=ROW_TILE, tn=1024)
        else:
            nmain = SSM_WIDTH + SSM_CONV_WIDTH
            w = ssd_w_in[j]
            proj, dt = _inproj(xf, g, mod_l, w[:, :nmain].astype(BF16), _pad_cols(w[:, nmain:], LANES).astype(BF16),
                               seq, tm=ROW_TILE, tn=1024, out_dtype=ACT_DTYPE)
            xbc = _ssd_conv(proj, ssd_conv_w[j], ssd_conv_b[j].reshape(1, -1), seq, tm=ROW_TILE, tc=1024,
                            out_dtype=ACT_DTYPE)
            padl = lambda v: _pad_cols(v.reshape(1, -1), LANES)
            z = _ssd_scan(proj, xbc, dt, padl(ssd_dt_bias[j]), padl(ssd_a_log[j]), padl(ssd_d[j]),
                          ssd_norm_g[j].reshape(1, -1), batch, seq, out_dtype=ACT_DTYPE)
            xf = _outproj(z, ssd_w_out[j].astype(BF16), xf, mod_l, seq, tm=ROW_TILE, tn=1024)

    out = _final_norm(xf, final_g.reshape(1, d))
    return out.reshape(batch, seq, d)
```

```python
import functools
import math

import jax
import jax.numpy as jnp
from jax import lax
from jax.experimental import pallas as pl
from jax.experimental.pallas import tpu as pltpu

F32 = jnp.float32
BF16 = jnp.bfloat16

D_MODEL = 2048
DEPTH = 4
N_MIXERS = 3
NORM_EPS = 1e-6

RWKV_HEAD = 64
RWKV_LORA = 96
RWKV_GN_EPS = 64e-5

GLA_HEADS = 4
GLA_KEY_WIDTH = D_MODEL // 2
GLA_VALUE_WIDTH = D_MODEL
GLA_HEAD_K = GLA_KEY_WIDTH // GLA_HEADS
GLA_HEAD_V = GLA_VALUE_WIDTH // GLA_HEADS
GLA_GATE_RANK = 16
GLA_GATE_TAU = 16.0

SSM_WIDTH = 2 * D_MODEL
SSM_HEADDIM = 64
SSM_HEADS = SSM_WIDTH // SSM_HEADDIM
SSM_STATE = 128
SSM_GROUPS = 8
SSM_CONV = 4
SSM_NORM_EPS = 1e-5
SSM_BC_WIDTH = SSM_GROUPS * SSM_STATE
SSM_CONV_WIDTH = SSM_WIDTH + 2 * SSM_BC_WIDTH

LANES = 128
SUBLANES = 8
VMEM_LIMIT_BYTES = 52 * 1024 * 1024

RWKV_CHUNK = 64
RWKV_CHUNKS_PER_STEP = 4
RWKV_PAIRS_PER_STEP = 2
GLA_BLOCK = 128
SSD_CHUNK = 128

RWKV_PASSES = 1
GATE_PASSES = 3

ROW_TILE = 512


def _cparams(sem):
    return pltpu.CompilerParams(dimension_semantics=sem, vmem_limit_bytes=VMEM_LIMIT_BYTES)


def _dot(a, b, *, nt=False, passes=1):
    dims = (((1,), (1,)), ((), ())) if nt else (((1,), (0,)), ((), ()))

    def d(x, y):
        return lax.dot_general(x, y, dims, preferred_element_type=F32)

    ah = a.astype(BF16)
    bh = b.astype(BF16)
    if passes == 1:
        return d(ah, bh)
    al = (a.astype(F32) - ah.astype(F32)).astype(BF16)
    bl = (b.astype(F32) - bh.astype(F32)).astype(BF16)
    return d(ah, bh) + (d(ah, bl) + d(al, bh))


def _dot_exact_lhs(m_bf16, x):
    x1 = x.astype(BF16)
    r1 = x - x1.astype(F32)
    x2 = r1.astype(BF16)
    x3 = (r1 - x2.astype(F32)).astype(BF16)

    def d(y):
        return jnp.dot(m_bf16, y, preferred_element_type=F32)

    return d(x1) + (d(x2) + d(x3))


def _silu(x):
    return x * jax.nn.sigmoid(x)


def _softplus(x):
    return jnp.maximum(x, 0.0) + jnp.log1p(jnp.exp(-jnp.abs(x)))


def _iota2(shape, dim):
    return lax.broadcasted_iota(jnp.int32, shape, dim)


def _ada_kernel(c_ref, w_ref, b_ref, o_ref):
    c = c_ref[...]
    o_ref[...] = _dot(_silu(c), w_ref[...]) + b_ref[...]


def _ada_mod(c_pad, ada_w, ada_b, tn=1024):
    depth, d, n = ada_w.shape
    rows = c_pad.shape[0]
    return pl.pallas_call(
        _ada_kernel,
        grid=(depth, n // tn),
        in_specs=[
            pl.BlockSpec((rows, d), lambda l, j: (0, 0)),
            pl.BlockSpec((None, d, tn), lambda l, j: (l, 0, j)),
            pl.BlockSpec((None, 1, tn), lambda l, j: (l, 0, j)),
        ],
        out_specs=pl.BlockSpec((None, rows, tn), lambda l, j: (l, 0, j)),
        out_shape=jax.ShapeDtypeStruct((depth, rows, n), F32),
        compiler_params=_cparams(("arbitrary", "arbitrary")),
        name="ada_mod",
    )(c_pad, ada_w, ada_b.reshape(depth, 1, n))


def _prenorm(x, g, mod):
    ms = jnp.mean(x * x, axis=-1, keepdims=True)
    return x * lax.rsqrt(ms + NORM_EPS) * g * (1.0 + mod[1:2, :]) + mod[0:1, :]


def _inproj_kernel(x_ref, g_ref, mod_ref, w_ref, ws_ref, o_ref, os_ref, h_scr):
    @pl.when(pl.program_id(1) == 0)
    def _():
        h = _prenorm(x_ref[...], g_ref[...], mod_ref[...]).astype(BF16)
        h_scr[...] = h
        os_ref[...] = jnp.dot(h, ws_ref[...], preferred_element_type=F32)

    o_ref[...] = jnp.dot(h_scr[...], w_ref[...], preferred_element_type=F32).astype(o_ref.dtype)


def _inproj(x, g, mod_l, w_main, w_side, seq, *, tm, tn, out_dtype):
    t, d = x.shape
    n = w_main.shape[1]
    ns = w_side.shape[1]
    tm = min(tm, seq)
    tiles_per_seq = seq // tm
    return pl.pallas_call(
        _inproj_kernel,
        grid=(t // tm, n // tn),
        in_specs=[
            pl.BlockSpec((tm, d), lambda i, j: (i, 0)),
            pl.BlockSpec((1, d), lambda i, j: (0, 0)),
            pl.BlockSpec((None, 3, d), lambda i, j: (i // tiles_per_seq, 0, 0)),
            pl.BlockSpec((d, tn), lambda i, j: (0, j)),
            pl.BlockSpec((d, ns), lambda i, j: (0, 0)),
        ],
        out_specs=[
            pl.BlockSpec((tm, tn), lambda i, j: (i, j)),
            pl.BlockSpec((tm, ns), lambda i, j: (i, 0)),
        ],
        out_shape=[
            jax.ShapeDtypeStruct((t, n), out_dtype),
            jax.ShapeDtypeStruct((t, ns), F32),
        ],
        scratch_shapes=[pltpu.VMEM((tm, d), BF16)],
        compiler_params=_cparams(("arbitrary", "arbitrary")),
        name="inproj",
    )(x, g, mod_l, w_main, w_side)


def _outproj_kernel(z_ref, w_ref, x_ref, mod_ref, o_ref):
    acc = jnp.dot(z_ref[...].astype(BF16), w_ref[...], preferred_element_type=F32)
    o_ref[...] = x_ref[...] + mod_ref[2:3, :] * acc


def _outproj(z, w, x, mod_l, seq, *, tm, tn):
    t, kd = z.shape
    n = w.shape[1]
    tm = min(tm, seq)
    tiles_per_seq = seq // tm
    return pl.pallas_call(
        _outproj_kernel,
        grid=(n // tn, t // tm),
        in_specs=[
            pl.BlockSpec((tm, kd), lambda j, i: (i, 0)),
            pl.BlockSpec((kd, tn), lambda j, i: (0, j)),
            pl.BlockSpec((tm, tn), lambda j, i: (i, j)),
            pl.BlockSpec((None, 3, tn), lambda j, i: (i // tiles_per_seq, 0, j)),
        ],
        out_specs=pl.BlockSpec((tm, tn), lambda j, i: (i, j)),
        out_shape=jax.ShapeDtypeStruct((t, n), F32),
        compiler_params=_cparams(("arbitrary", "arbitrary")),
        name="outproj",
    )(z, w, x, mod_l)


def _final_norm_kernel(x_ref, g_ref, o_ref):
    x = x_ref[...]
    ms = jnp.mean(x * x, axis=-1, keepdims=True)
    o_ref[...] = x * lax.rsqrt(ms + NORM_EPS) * g_ref[...]


def _final_norm(x, g, *, tm=512):
    t, d = x.shape
    tm = min(tm, t)
    return pl.pallas_call(
        _final_norm_kernel,
        grid=(t // tm,),
        in_specs=[pl.BlockSpec((tm, d), lambda i: (i, 0)), pl.BlockSpec((1, d), lambda i: (0, 0))],
        out_specs=pl.BlockSpec((tm, d), lambda i: (i, 0)),
        out_shape=jax.ShapeDtypeStruct((t, d), F32),
        compiler_params=_cparams(("arbitrary",)),
        name="final_norm",
    )(x, g)


def _rwkv_inproj_kernel(x_ref, xh_ref, g_ref, mod_ref, mu_ref, w_ref, w1_ref, o_ref, lora_ref, xs_scr,
                        *, tiles_per_seq, col_tiles):
    i = pl.program_id(0)
    j = pl.program_id(1)

    @pl.when(j == 0)
    def _():
        g = g_ref[...]
        mod = mod_ref[...]
        h = _prenorm(x_ref[...], g, mod)
        hp8 = _prenorm(xh_ref[...], g, mod)
        first = (i % tiles_per_seq) == 0
        prev_row = jnp.where(first, 0.0, hp8[SUBLANES - 1:SUBLANES, :])
        rolled = pltpu.roll(h, 1, 0)
        row0 = _iota2(h.shape, 0) == 0
        dh = jnp.where(row0, prev_row, rolled) - h
        mu = mu_ref[...]
        for c in range(4):
            xs_scr[c] = (h + dh * mu[c:c + 1, :]).astype(BF16)
        xw = (h + dh * mu[4:5, :]).astype(BF16)
        xa = (h + dh * mu[5:6, :]).astype(BF16)
        w1 = w1_ref[...]
        dec_h = jnp.tanh(jnp.dot(xw, w1[:, :LANES], preferred_element_type=F32))
        icl_h = jnp.dot(xa, w1[:, LANES:], preferred_element_type=F32)
        lora_ref[...] = jnp.concatenate([dec_h, icl_h], axis=1)

    o_ref[...] = jnp.dot(xs_scr[j // col_tiles], w_ref[...], preferred_element_type=F32).astype(o_ref.dtype)


def _rwkv_inproj(x, g, mod_l, mu, w_in, w1, seq, *, tm, tn, out_dtype):
    t, d = x.shape
    n = w_in.shape[1]
    tm = min(tm, seq)
    tiles_per_seq = seq // tm
    halo_blocks = tm // SUBLANES
    kern = functools.partial(_rwkv_inproj_kernel, tiles_per_seq=tiles_per_seq, col_tiles=d // tn)
    return pl.pallas_call(
        kern,
        grid=(t // tm, n // tn),
        in_specs=[
            pl.BlockSpec((tm, d), lambda i, j: (i, 0)),
            pl.BlockSpec((SUBLANES, d), lambda i, j: (jnp.maximum(i * halo_blocks - 1, 0), 0)),
            pl.BlockSpec((1, d), lambda i, j: (0, 0)),
            pl.BlockSpec((None, 3, d), lambda i, j: (i // tiles_per_seq, 0, 0)),
            pl.BlockSpec((6, d), lambda i, j: (0, 0)),
            pl.BlockSpec((d, tn), lambda i, j: (0, j)),
            pl.BlockSpec((d, 2 * LANES), lambda i, j: (0, 0)),
        ],
        out_specs=[
            pl.BlockSpec((tm, tn), lambda i, j: (i, j)),
            pl.BlockSpec((tm, 2 * LANES), lambda i, j: (i, 0)),
        ],
        out_shape=[
            jax.ShapeDtypeStruct((t, n), out_dtype),
            jax.ShapeDtypeStruct((t, 2 * LANES), F32),
        ],
        scratch_shapes=[pltpu.VMEM((4, tm, d), BF16)],
        compiler_params=_cparams(("arbitrary", "arbitrary")),
        name="rwkv_inproj",
    )(x, x, g, mod_l, mu, w_in, w1)


def _unit_lower_inverses(a_list, eye, blk8, offdiag, passes):
    a8 = [jnp.where(blk8, a, 0.0) for a in a_list]
    a2 = [_dot(x, x, passes=passes) for x in a8]
    a4 = [_dot(x, x, passes=passes) for x in a2]
    t1 = [_dot(eye + x, eye + y, passes=passes) for x, y in zip(a8, a2)]
    inv = [_dot(t, eye + z, passes=passes) for t, z in zip(t1, a4)]
    for m in offdiag:
        t = [_dot(jnp.where(m, a, 0.0), i, passes=passes) for a, i in zip(a_list, inv)]
        inv = [i + _dot(i, x, passes=passes) for i, x in zip(inv, t)]
    return inv


def _rwkv_scan_kernel(r_ref, k_ref, v_ref, g_ref, lora_ref, dw2_ref, iw2_ref, dw0_ref, iw0_ref,
                      kk_ref, ka_ref, rk_ref, gnw_ref, gnb_ref, z_ref, s_scr, cum_scr, y_scr,
                      *, chunks, pairs, passes):
    L = RWKV_CHUNK
    N = RWKV_HEAD
    R = L * chunks
    P = 2 * L

    @pl.when(pl.program_id(2) == 0)
    def _():
        s_scr[...] = jnp.zeros_like(s_scr)

    lane = _iota2((1, LANES), 1)
    m0 = lane < N
    mf0 = m0.astype(F32)
    mf1 = 1.0 - mf0

    def headsum(x):
        s0 = jnp.sum(jnp.where(m0, x, 0.0), axis=-1, keepdims=True)
        s1 = jnp.sum(jnp.where(m0, 0.0, x), axis=-1, keepdims=True)
        return jnp.where(m0, s0, s1)

    def stack(x):
        return jnp.concatenate([x * mf0, x * mf1], axis=0)

    lora = lora_ref[...]
    ri = _iota2((R, R), 0)
    ci = _iota2((R, R), 1)
    tri = jnp.where((ri // L == ci // L) & (ci <= ri), 1.0, 0.0).astype(BF16)

    pair_vals = []
    for p in range(pairs):
        cols = slice(p * LANES, (p + 1) * LANES)
        r = r_ref[:, cols].astype(F32)
        k = k_ref[:, cols].astype(F32)
        v = v_ref[:, cols].astype(F32)
        dec = dw0_ref[:, cols] + _dot(lora[:, :LANES], dw2_ref[:, cols], passes=GATE_PASSES)
        lw = -jnp.exp(-_softplus(-dec) - 0.5)
        a = jax.nn.sigmoid(iw0_ref[:, cols] + _dot(lora[:, LANES:], iw2_ref[:, cols], passes=GATE_PASSES))
        kkr = k * kk_ref[:, cols]
        kk = kkr / jnp.maximum(jnp.sqrt(headsum(kkr * kkr)), 1e-12)
        k2 = k * (1.0 + (a - 1.0) * ka_ref[:, cols])
        cum = _dot_exact_lhs(tri, lw)
        cum_scr[:, cols] = cum
        pair_vals.append(dict(r=r, k2=k2, v=v, av=-kk, bv=kk * a, cum=cum, cumex=cum - lw))

    rp = _iota2((P, P), 0)
    cp = _iota2((P, P), 1)
    strict = (rp % L) > (cp % L)
    incl = (rp % L) >= (cp % L)
    eye = jnp.where(rp == cp, 1.0, 0.0)
    blk8 = (rp // 8) == (cp // 8)
    offdiag = [((rp // (2 * b)) == (cp // (2 * b))) & ((rp // b) != (cp // b)) for b in (8, 16, 32)]

    items = [(p, c) for p in range(pairs) for c in range(chunks)]
    pre = []
    for p, c in items:
        pv = pair_vals[p]
        cols = slice(p * LANES, (p + 1) * LANES)
        sl = slice(c * L, (c + 1) * L)
        cref = cum_scr[pl.ds(c * L + L // 2 - 1, 1), cols]
        clast = cum_scr[pl.ds(c * L + L - 1, 1), cols]
        cum_c, cumex_c = pv["cum"][sl], pv["cumex"][sl]
        r_c, k_c, v_c, av_c, bv_c = pv["r"][sl], pv["k2"][sl], pv["v"][sl], pv["av"][sl], pv["bv"][sl]
        e_out = jnp.exp(cref - cum_c)
        e_end = jnp.exp(clast - cum_c)
        pre.append(dict(
            lhs1=jnp.concatenate([stack(av_c * jnp.exp(cumex_c - cref)), stack(r_c * jnp.exp(cum_c - cref))], axis=0),
            rhs1=jnp.concatenate([stack(bv_c * e_out), stack(k_c * e_out)], axis=0),
            v_st=stack(v_c), a0_st=stack(av_c * jnp.exp(cumex_c)), r0_st=stack(r_c * jnp.exp(cum_c)),
            bk=jnp.concatenate([stack(bv_c * e_end), stack(k_c * e_end)], axis=0), decay=jnp.exp(clast)))

    x1 = [_dot(f["lhs1"], f["rhs1"], nt=True, passes=passes) for f in pre]
    a_ab = [jnp.where(strict, x[:P, :P], 0.0) for x in x1]
    a_kr = [jnp.concatenate([jnp.where(strict, x[:P, P:], 0.0), jnp.where(incl, x[P:, P:], 0.0)], axis=0) for x in x1]
    a_rb = [jnp.where(incl, x[P:, :P], 0.0) for x in x1]
    m1 = [_dot(a, f["v_st"], passes=passes) for a, f in zip(a_kr, pre)]
    tinv = _unit_lower_inverses(a_ab, eye, blk8, offdiag, passes)
    m2 = [_dot(t, jnp.concatenate([f["a0_st"], m[:P]], axis=1), passes=passes) for t, f, m in zip(tinv, pre, m1)]
    fac = {}
    for it, f, m, mm, arb in zip(items, pre, m1, m2, a_rb):
        w_st, uv_st = mm[:, :LANES], mm[:, LANES:]
        fac[it] = dict(wr=jnp.concatenate([w_st, f["r0_st"]], axis=0), w_st=w_st, uv_st=uv_st, uv_t=uv_st.T,
                       yv_st=m[P:], a_rb=arb, v_t=f["v_st"].T, decay=f["decay"], bk=f["bk"])

    s = [s_scr[p] for p in range(pairs)]
    for c in range(chunks):
        fs = [fac[(p, c)] for p in range(pairs)]
        ut = [_dot(s[p], fs[p]["w_st"], nt=True, passes=passes) + fs[p]["uv_t"] for p in range(pairs)]
        m3 = [_dot(fs[p]["wr"], s[p], nt=True, passes=passes) for p in range(pairs)]
        s = [s[p] * fs[p]["decay"] + _dot(jnp.concatenate([ut[p], fs[p]["v_t"]], axis=1), fs[p]["bk"], passes=passes)
             for p in range(pairs)]
        for p in range(pairs):
            u_st = m3[p][:P] + fs[p]["uv_st"]
            y_st = m3[p][P:] + fs[p]["yv_st"] + _dot(fs[p]["a_rb"], u_st, passes=passes)
            y_scr[c * L:(c + 1) * L, p * LANES:(p + 1) * LANES] = y_st[:L] + y_st[L:]
    for p in range(pairs):
        s_scr[p] = s[p]

    for p in range(pairs):
        cols = slice(p * LANES, (p + 1) * LANES)
        pv = pair_vals[p]
        y = y_scr[:, cols]
        mean = headsum(y) * (1.0 / N)
        yc = y - mean
        var = headsum(yc * yc) * (1.0 / N)
        yn = yc * lax.rsqrt(var + RWKV_GN_EPS) * gnw_ref[:, cols] + gnb_ref[:, cols]
        bonus = headsum(pv["r"] * pv["k2"] * rk_ref[:, cols]) * pv["v"]
        z_ref[:, cols] = ((yn + bonus) * _silu(g_ref[:, cols].astype(F32))).astype(z_ref.dtype)


def _rwkv_scan(rkvg, lora, p, batch, seq, *, out_dtype):
    t = rkvg.shape[0]
    w = D_MODEL
    pairs = RWKV_PAIRS_PER_STEP
    bw = pairs * LANES
    nblk = w // bw
    chunks = min(RWKV_CHUNKS_PER_STEP, seq // RWKV_CHUNK)
    rows = RWKV_CHUNK * chunks
    steps = seq // rows

    def act(col0):
        return pl.BlockSpec((rows, bw), lambda b, h, n: (b * steps + n, col0 + h))

    def vec():
        return pl.BlockSpec((1, bw), lambda b, h, n: (0, h))

    def w2():
        return pl.BlockSpec((LANES, bw), lambda b, h, n: (0, h))

    kern = functools.partial(_rwkv_scan_kernel, chunks=chunks, pairs=pairs, passes=RWKV_PASSES)
    return pl.pallas_call(
        kern,
        grid=(batch, nblk, steps),
        in_specs=[act(0), act(nblk), act(2 * nblk), act(3 * nblk),
                  pl.BlockSpec((rows, 2 * LANES), lambda b, h, n: (b * steps + n, 0)),
                  w2(), w2(), vec(), vec(), vec(), vec(), vec(), vec(), vec()],
        out_specs=pl.BlockSpec((rows, bw), lambda b, h, n: (b * steps + n, h)),
        out_shape=jax.ShapeDtypeStruct((t, w), out_dtype),
        scratch_shapes=[pltpu.VMEM((pairs, LANES, LANES), F32), pltpu.VMEM((rows, bw), F32),
                        pltpu.VMEM((rows, bw), F32)],
        compiler_params=_cparams(("arbitrary", "arbitrary", "arbitrary")),
        name="rwkv_scan",
    )(rkvg, rkvg, rkvg, rkvg, lora, p["dec_w2"], p["iclr_w2"], p["dec_w0"], p["iclr_w0"],
      p["k_k"], p["k_a"], p["r_k"], p["gn_w"], p["gn_b"])


def _gla_kernel(q_ref, k_ref, v_ref, g_ref, low_ref, w2_ref, b_ref, hg_ref, o_ref, s_scr):
    R = GLA_BLOCK
    Hf = R // 2

    @pl.when(pl.program_id(2) == 0)
    def _():
        s_scr[...] = jnp.zeros_like(s_scr)

    q = q_ref[...].astype(F32) * (GLA_HEAD_K ** -0.5)
    k = k_ref[...].astype(F32)
    v = v_ref[...].astype(F32)
    la = -_softplus(-(_dot(low_ref[...], w2_ref[...], passes=GATE_PASSES) + b_ref[...])) * (1.0 / GLA_GATE_TAU)
    ri = _iota2((R, R), 0)
    ci = _iota2((R, R), 1)
    tri = jnp.where(ci <= ri, 1.0, 0.0).astype(BF16)
    bcum = _dot_exact_lhs(tri, la)
    rh = _iota2((Hf, Hf), 0)
    ch = _iota2((Hf, Hf), 1)
    causal = ch <= rh

    bt, bb = bcum[:Hf], bcum[Hf:]
    qt, qb = q[:Hf], q[Hf:]
    kt, kb = k[:Hf], k[Hf:]
    vt, vb = v[:Hf], v[Hf:]
    ref_t = bcum[Hf // 2 - 1:Hf // 2]
    ref_m = bcum[Hf - 1:Hf]
    ref_b = bcum[Hf + Hf // 2 - 1:Hf + Hf // 2]
    last = bcum[R - 1:R]
    s00 = jnp.where(causal, _dot(qt * jnp.exp(bt - ref_t), kt * jnp.exp(ref_t - bt), nt=True), 0.0)
    s11 = jnp.where(causal, _dot(qb * jnp.exp(bb - ref_b), kb * jnp.exp(ref_b - bb), nt=True), 0.0)
    s10 = _dot(qb * jnp.exp(bb - ref_m), kt * jnp.exp(ref_m - bt), nt=True)
    o_top = _dot(s00, vt)
    o_bot = _dot(s10, vt) + _dot(s11, vb)
    st = s_scr[...]
    o = jnp.concatenate([o_top, o_bot], axis=0) + _dot(q * jnp.exp(bcum), st, nt=True)
    ke = k * jnp.exp(last - bcum)
    s_scr[...] = st * jnp.exp(last) + _dot(v.T, ke)

    ms = jnp.mean(o * o, axis=-1, keepdims=True)
    on = o * lax.rsqrt(ms + NORM_EPS) * hg_ref[...]
    o_ref[...] = (on * _silu(g_ref[...].astype(F32))).astype(o_ref.dtype)


def _gla_scan(proj, low, gate_w2, gate_b, head_g, batch, seq, *, out_dtype):
    t = proj.shape[0]
    R = GLA_BLOCK
    steps = seq // R
    kb = GLA_KEY_WIDTH // GLA_HEAD_K
    vb0 = 2 * GLA_KEY_WIDTH // GLA_HEAD_V
    nvb = GLA_VALUE_WIDTH // GLA_HEAD_V

    def row(b, h, n):
        return b * steps + n

    return pl.pallas_call(
        _gla_kernel,
        grid=(batch, GLA_HEADS, steps),
        in_specs=[
            pl.BlockSpec((R, GLA_HEAD_K), lambda b, h, n: (row(b, h, n), h)),
            pl.BlockSpec((R, GLA_HEAD_K), lambda b, h, n: (row(b, h, n), kb + h)),
            pl.BlockSpec((R, GLA_HEAD_V), lambda b, h, n: (row(b, h, n), vb0 + h)),
            pl.BlockSpec((R, GLA_HEAD_V), lambda b, h, n: (row(b, h, n), vb0 + nvb + h)),
            pl.BlockSpec((R, LANES), lambda b, h, n: (row(b, h, n), 0)),
            pl.BlockSpec((LANES, GLA_HEAD_K), lambda b, h, n: (0, h)),
            pl.BlockSpec((1, GLA_HEAD_K), lambda b, h, n: (0, h)),
            pl.BlockSpec((1, GLA_HEAD_V), lambda b, h, n: (0, 0)),
        ],
        out_specs=pl.BlockSpec((R, GLA_HEAD_V), lambda b, h, n: (row(b, h, n), h)),
        out_shape=jax.ShapeDtypeStruct((t, GLA_VALUE_WIDTH), out_dtype),
        scratch_shapes=[pltpu.VMEM((GLA_HEAD_V, GLA_HEAD_K), F32)],
        compiler_params=_cparams(("arbitrary", "arbitrary", "arbitrary")),
        name="gla_scan",
    )(proj, proj, proj, proj, low, gate_w2, gate_b, head_g)


def _conv_kernel(x_ref, xh_ref, w_ref, b_ref, o_ref, *, tiles_per_seq):
    i = pl.program_id(1)
    x = x_ref[...].astype(F32)
    halo = jnp.where((i % tiles_per_seq) == 0, 0.0, xh_ref[...].astype(F32))
    w = w_ref[...]
    rows = _iota2(x.shape, 0)
    acc = x * w[SSM_CONV - 1:SSM_CONV, :] + b_ref[...]
    for s in range(1, SSM_CONV):
        rolled = pltpu.roll(x, s, 0)
        hrow = pltpu.roll(halo, s, 0)
        head = jnp.concatenate([hrow, jnp.zeros((x.shape[0] - SUBLANES, x.shape[1]), F32)], axis=0)
        shifted = jnp.where(rows < s, head, rolled)
        acc = acc + shifted * w[SSM_CONV - 1 - s:SSM_CONV - s, :]
    o_ref[...] = _silu(acc).astype(o_ref.dtype)


def _ssd_conv(proj, conv_w, conv_b, seq, *, tm, tc, out_dtype):
    t = proj.shape[0]
    tm = min(tm, seq)
    tiles_per_seq = seq // tm
    col0 = SSM_WIDTH // tc
    halo_blocks = tm // SUBLANES
    kern = functools.partial(_conv_kernel, tiles_per_seq=tiles_per_seq)
    return pl.pallas_call(
        kern,
        grid=(SSM_CONV_WIDTH // tc, t // tm),
        in_specs=[
            pl.BlockSpec((tm, tc), lambda j, i: (i, col0 + j)),
            pl.BlockSpec((SUBLANES, tc), lambda j, i: (jnp.maximum(i * halo_blocks - 1, 0), col0 + j)),
            pl.BlockSpec((SSM_CONV, tc), lambda j, i: (0, j)),
            pl.BlockSpec((1, tc), lambda j, i: (0, j)),
        ],
        out_specs=pl.BlockSpec((tm, tc), lambda j, i: (i, j)),
        out_shape=jax.ShapeDtypeStruct((t, SSM_CONV_WIDTH), out_dtype),
        compiler_params=_cparams(("arbitrary", "arbitrary")),
        name="ssd_conv",
    )(proj, proj, conv_w, conv_b)


def _ssd_kernel(z_ref, xs_ref, bm_ref, cm_ref, dt_ref, dtb_ref, alog_ref, dsk_ref, ng_ref, o_ref, st_scr):
    C = SSD_CHUNK
    P = SSM_HEADDIM
    pairs_per_group = (SSM_HEADS // SSM_GROUPS) // 2
    gw = SSM_WIDTH // SSM_GROUPS

    @pl.when(pl.program_id(1) == 0)
    def _():
        st_scr[...] = jnp.zeros_like(st_scr)

    dt = _softplus(dt_ref[...] + dtb_ref[...])
    da = dt * (-jnp.exp(alog_ref[...]))
    ri = _iota2((C, C), 0)
    ci = _iota2((C, C), 1)
    causal = ci <= ri
    tri = jnp.where(causal, 1.0, 0.0).astype(BF16)
    acum = _dot_exact_lhs(tri, da)
    acum_t = acum.T
    alast = acum[C - 1:C, :]
    dsk = dsk_ref[...]

    lane = _iota2((1, LANES), 1)
    m0 = lane < P
    mf0 = m0.astype(F32)
    mf1 = 1.0 - mf0

    def pair_cols(x, h0):
        return jnp.where(m0, x[:, h0:h0 + 1], x[:, h0 + 1:h0 + 2])

    for g in range(SSM_GROUPS):
        bm = bm_ref[:, g * SSM_STATE:(g + 1) * SSM_STATE].astype(F32)
        cm = cm_ref[:, g * SSM_STATE:(g + 1) * SSM_STATE].astype(F32)
        cb = _dot(cm, bm, nt=True)
        bm_t = bm.T
        ys = []
        for pp in range(pairs_per_group):
            pidx = g * pairs_per_group + pp
            h0 = 2 * pidx
            cols = slice(pidx * LANES, (pidx + 1) * LANES)
            x_p = xs_ref[:, cols].astype(F32)
            dt_p = pair_cols(dt, h0)
            ac_p = pair_cols(acum, h0)
            al_p = pair_cols(alast, h0)
            xc = x_p * dt_p
            dec0 = jnp.where(causal, jnp.exp(acum[:, h0:h0 + 1] - acum_t[h0:h0 + 1, :]), 0.0)
            dec1 = jnp.where(causal, jnp.exp(acum[:, h0 + 1:h0 + 2] - acum_t[h0 + 1:h0 + 2, :]), 0.0)
            lhs = jnp.concatenate([cb * dec0, cb * dec1], axis=1)
            rhs = jnp.concatenate([xc * mf0, xc * mf1], axis=0)
            prev = st_scr[pidx]
            y = _dot(lhs, rhs) + _dot(cm, prev) * jnp.exp(ac_p) + pair_cols(dsk, h0) * x_p
            st_scr[pidx] = prev * jnp.exp(al_p) + _dot(bm_t, xc * jnp.exp(al_p - ac_p))
            ys.append(y * _silu(z_ref[:, cols].astype(F32)))
        yg = jnp.concatenate(ys, axis=1)
        ms = jnp.mean(yg * yg, axis=-1, keepdims=True)
        o_ref[:, g * gw:(g + 1) * gw] = (yg * lax.rsqrt(ms + SSM_NORM_EPS) * ng_ref[:, g * gw:(g + 1) * gw]).astype(o_ref.dtype)


def _ssd_scan(proj, xbc, dt, dt_bias, a_log, d_skip, norm_g, batch, seq, *, out_dtype):
    t = proj.shape[0]
    C = SSD_CHUNK
    steps = seq // C
    npairs = SSM_HEADS // 2

    def row(b, n):
        return b * steps + n

    bcol = SSM_WIDTH // SSM_BC_WIDTH
    return pl.pallas_call(
        _ssd_kernel,
        grid=(batch, steps),
        in_specs=[
            pl.BlockSpec((C, SSM_WIDTH), lambda b, n: (row(b, n), 0)),
            pl.BlockSpec((C, SSM_WIDTH), lambda b, n: (row(b, n), 0)),
            pl.BlockSpec((C, SSM_BC_WIDTH), lambda b, n: (row(b, n), bcol)),
            pl.BlockSpec((C, SSM_BC_WIDTH), lambda b, n: (row(b, n), bcol + 1)),
            pl.BlockSpec((C, LANES), lambda b, n: (row(b, n), 0)),
            pl.BlockSpec((1, LANES), lambda b, n: (0, 0)),
            pl.BlockSpec((1, LANES), lambda b, n: (0, 0)),
            pl.BlockSpec((1, LANES), lambda b, n: (0, 0)),
            pl.BlockSpec((1, SSM_WIDTH), lambda b, n: (0, 0)),
        ],
        out_specs=pl.BlockSpec((C, SSM_WIDTH), lambda b, n: (row(b, n), 0)),
        out_shape=jax.ShapeDtypeStruct((t, SSM_WIDTH), out_dtype),
        scratch_shapes=[pltpu.VMEM((npairs, SSM_STATE, LANES), F32)],
        compiler_params=_cparams(("arbitrary", "arbitrary")),
        name="ssd_scan",
    )(proj, xbc, xbc, xbc, dt, dt_bias, a_log, d_skip, norm_g)


def _pad_cols(w, n):
    return jnp.pad(w, ((0, 0), (0, n - w.shape[1])))


def _pad_rows(w, n):
    return jnp.pad(w, ((0, n - w.shape[0]), (0, 0)))


ACT_DTYPE = F32


def kernel(x, c, ada_w, ada_b, norm_g, final_g, rwkv_mu, rwkv_w_in, rwkv_dec_w1, rwkv_dec_w2, rwkv_dec_w0, rwkv_iclr_w1, rwkv_iclr_w2, rwkv_iclr_w0, rwkv_k_k, rwkv_k_a, rwkv_r_k, rwkv_gn_w, rwkv_gn_b, rwkv_w_out, gla_w_in, gla_gate_w2, gla_gate_b, gla_head_g, gla_w_out, ssd_w_in, ssd_conv_w, ssd_conv_b, ssd_dt_bias, ssd_a_log, ssd_d, ssd_norm_g, ssd_w_out):
    batch, seq, d = x.shape
    t = batch * seq
    xf = x.reshape(t, d)

    c_pad = jnp.pad(c, ((0, SUBLANES - batch % SUBLANES if batch % SUBLANES else 0), (0, 0)))
    mod = _ada_mod(c_pad, ada_w, ada_b)[:, :batch].reshape(DEPTH, batch, 3, d)

    for i in range(DEPTH):
        kind, j = i % N_MIXERS, i // N_MIXERS
        g = norm_g[i].reshape(1, d)
        mod_l = mod[i]
        if kind == 0:
            w1 = jnp.concatenate([_pad_cols(rwkv_dec_w1[j], LANES), _pad_cols(rwkv_iclr_w1[j], LANES)], axis=1)
            rkvg, lora = _rwkv_inproj(xf, g, mod_l, rwkv_mu[j], rwkv_w_in[j].astype(BF16), w1.astype(BF16), seq,
                                      tm=ROW_TILE, tn=1024, out_dtype=ACT_DTYPE)
            row = lambda v: v.reshape(1, -1)
            params = dict(dec_w2=_pad_rows(rwkv_dec_w2[j], LANES), iclr_w2=_pad_rows(rwkv_iclr_w2[j], LANES),
                          dec_w0=row(rwkv_dec_w0[j]), iclr_w0=row(rwkv_iclr_w0[j]), k_k=row(rwkv_k_k[j]),
                          k_a=row(rwkv_k_a[j]), r_k=row(rwkv_r_k[j]), gn_w=row(rwkv_gn_w[j]), gn_b=row(rwkv_gn_b[j]))
            z = _rwkv_scan(rkvg, lora, params, batch, seq, out_dtype=ACT_DTYPE)
            xf = _outproj(z, rwkv_w_out[j].astype(BF16), xf, mod_l, seq, tm=ROW_TILE, tn=1024)
        elif kind == 1:
            nmain = 2 * GLA_KEY_WIDTH + 2 * GLA_VALUE_WIDTH
            w = gla_w_in[j]
            proj, low = _inproj(xf, g, mod_l, w[:, :nmain].astype(BF16), _pad_cols(w[:, nmain:], LANES).astype(BF16),
                                seq, tm=ROW_TILE, tn=1024, out_dtype=ACT_DTYPE)
            z = _gla_scan(proj, low, _pad_rows(gla_gate_w2[j], LANES), gla_gate_b[j].reshape(1, -1),
                          gla_head_g[j].reshape(1, -1), batch, seq, out_dtype=ACT_DTYPE)
            xf = _outproj(z, gla_w_out[j].astype(BF16), xf, mod_l, seq, tm=ROW_TILE, tn=1024)
        else:
            nmain = SSM_WIDTH + SSM_CONV_WIDTH
            w = ssd_w_in[j]
            proj, dt = _inproj(xf, g, mod_l, w[:, :nmain].astype(BF16), _pad_cols(w[:, nmain:], LANES).astype(BF16),
                               seq, tm=ROW_TILE, tn=1024, out_dtype=ACT_DTYPE)
            xbc = _ssd_conv(proj, ssd_conv_w[j], ssd_conv_b[j].reshape(1, -1), seq, tm=ROW_TILE, tc=1024,
                            out_dtype=ACT_DTYPE)
            padl = lambda v: _pad_cols(v.reshape(1, -1), LANES)
            z = _ssd_scan(proj, xbc, dt, padl(ssd_dt_bias[j]), padl(ssd_a_log[j]), padl(ssd_d[j]),
                          ssd_norm_g[j].reshape(1, -1), batch, seq, out_dtype=ACT_DTYPE)
            xf = _outproj(z, ssd_w_out[j].astype(BF16), xf, mod_l, seq, tm=ROW_TILE, tn=1024)

    out = _final_norm(xf, final_g.reshape(1, d))
    return out.reshape(batch, seq, d)
```

```python
import functools
import math

import jax
import jax.numpy as jnp
from jax import lax
from jax.experimental import pallas as pl
from jax.experimental.pallas import tpu as pltpu

F32 = jnp.float32
BF16 = jnp.bfloat16

D_MODEL = 2048
DEPTH = 4
N_MIXERS = 3
NORM_EPS = 1e-6

RWKV_HEAD = 64
RWKV_LORA = 96
RWKV_GN_EPS = 64e-5

GLA_HEADS = 4
GLA_KEY_WIDTH = D_MODEL // 2
GLA_VALUE_WIDTH = D_MODEL
GLA_HEAD_K = GLA_KEY_WIDTH // GLA_HEADS
GLA_HEAD_V = GLA_VALUE_WIDTH // GLA_HEADS
GLA_GATE_RANK = 16
GLA_GATE_TAU = 16.0

SSM_WIDTH = 2 * D_MODEL
SSM_HEADDIM = 64
SSM_HEADS = SSM_WIDTH // SSM_HEADDIM
SSM_STATE = 128
SSM_GROUPS = 8
SSM_CONV = 4
SSM_NORM_EPS = 1e-5
SSM_BC_WIDTH = SSM_GROUPS * SSM_STATE
SSM_CONV_WIDTH = SSM_WIDTH + 2 * SSM_BC_WIDTH

LANES = 128
SUBLANES = 8
VMEM_LIMIT_BYTES = 52 * 1024 * 1024

RWKV_CHUNK = 64
RWKV_CHUNKS_PER_STEP = 4
RWKV_PAIRS_PER_STEP = 4
GLA_BLOCK = 128
SSD_CHUNK = 128

RWKV_PASSES = 1
GATE_PASSES = 3

ROW_TILE = 512


def _cparams(sem):
    return pltpu.CompilerParams(dimension_semantics=sem, vmem_limit_bytes=VMEM_LIMIT_BYTES)


def _dot(a, b, *, nt=False, passes=1):
    dims = (((1,), (1,)), ((), ())) if nt else (((1,), (0,)), ((), ()))

    def d(x, y):
        return lax.dot_general(x, y, dims, preferred_element_type=F32)

    ah = a.astype(BF16)
    bh = b.astype(BF16)
    if passes == 1:
        return d(ah, bh)
    al = (a.astype(F32) - ah.astype(F32)).astype(BF16)
    bl = (b.astype(F32) - bh.astype(F32)).astype(BF16)
    return d(ah, bh) + (d(ah, bl) + d(al, bh))


def _dot_exact_lhs(m_bf16, x):
    x1 = x.astype(BF16)
    r1 = x - x1.astype(F32)
    x2 = r1.astype(BF16)
    x3 = (r1 - x2.astype(F32)).astype(BF16)

    def d(y):
        return jnp.dot(m_bf16, y, preferred_element_type=F32)

    return d(x1) + (d(x2) + d(x3))


def _silu(x):
    return x * jax.nn.sigmoid(x)


def _softplus(x):
    return jnp.maximum(x, 0.0) + jnp.log(1.0 + jnp.exp(-jnp.abs(x)))


def _iota2(shape, dim):
    return lax.broadcasted_iota(jnp.int32, shape, dim)


def _ada_kernel(c_ref, w_ref, b_ref, o_ref):
    c = c_ref[...]
    o_ref[...] = _dot(_silu(c), w_ref[...]) + b_ref[...]


def _ada_mod(c_pad, ada_w, ada_b, tn=1024):
    depth, d, n = ada_w.shape
    rows = c_pad.shape[0]
    return pl.pallas_call(
        _ada_kernel,
        grid=(depth, n // tn),
        in_specs=[
            pl.BlockSpec((rows, d), lambda l, j: (0, 0)),
            pl.BlockSpec((None, d, tn), lambda l, j: (l, 0, j)),
            pl.BlockSpec((None, 1, tn), lambda l, j: (l, 0, j)),
        ],
        out_specs=pl.BlockSpec((None, rows, tn), lambda l, j: (l, 0, j)),
        out_shape=jax.ShapeDtypeStruct((depth, rows, n), F32),
        compiler_params=_cparams(("arbitrary", "arbitrary")),
        name="ada_mod",
    )(c_pad, ada_w, ada_b.reshape(depth, 1, n))


def _prenorm(x, g, mod):
    ms = jnp.mean(x * x, axis=-1, keepdims=True)
    return x * lax.rsqrt(ms + NORM_EPS) * g * (1.0 + mod[1:2, :]) + mod[0:1, :]


def _inproj_kernel(x_ref, g_ref, mod_ref, w_ref, ws_ref, o_ref, os_ref, h_scr):
    @pl.when(pl.program_id(1) == 0)
    def _():
        h = _prenorm(x_ref[...], g_ref[...], mod_ref[...]).astype(BF16)
        h_scr[...] = h
        os_ref[...] = jnp.dot(h, ws_ref[...], preferred_element_type=F32)

    o_ref[...] = jnp.dot(h_scr[...], w_ref[...], preferred_element_type=F32).astype(o_ref.dtype)


def _inproj(x, g, mod_l, w_main, w_side, seq, *, tm, tn, out_dtype):
    t, d = x.shape
    n = w_main.shape[1]
    ns = w_side.shape[1]
    tm = min(tm, seq)
    tiles_per_seq = seq // tm
    return pl.pallas_call(
        _inproj_kernel,
        grid=(t // tm, n // tn),
        in_specs=[
            pl.BlockSpec((tm, d), lambda i, j: (i, 0)),
            pl.BlockSpec((1, d), lambda i, j: (0, 0)),
            pl.BlockSpec((None, 3, d), lambda i, j: (i // tiles_per_seq, 0, 0)),
            pl.BlockSpec((d, tn), lambda i, j: (0, j)),
            pl.BlockSpec((d, ns), lambda i, j: (0, 0)),
        ],
        out_specs=[
            pl.BlockSpec((tm, tn), lambda i, j: (i, j)),
            pl.BlockSpec((tm, ns), lambda i, j: (i, 0)),
        ],
        out_shape=[
            jax.ShapeDtypeStruct((t, n), out_dtype),
            jax.ShapeDtypeStruct((t, ns), F32),
        ],
        scratch_shapes=[pltpu.VMEM((tm, d), BF16)],
        compiler_params=_cparams(("arbitrary", "arbitrary")),
        name="inproj",
    )(x, g, mod_l, w_main, w_side)


def _outproj_kernel(z_ref, w_ref, x_ref, mod_ref, o_ref):
    acc = jnp.dot(z_ref[...].astype(BF16), w_ref[...], preferred_element_type=F32)
    o_ref[...] = x_ref[...] + mod_ref[2:3, :] * acc


def _outproj(z, w, x, mod_l, seq, *, tm, tn):
    t, kd = z.shape
    n = w.shape[1]
    tm = min(tm, seq)
    tiles_per_seq = seq // tm
    return pl.pallas_call(
        _outproj_kernel,
        grid=(n // tn, t // tm),
        in_specs=[
            pl.BlockSpec((tm, kd), lambda j, i: (i, 0)),
            pl.BlockSpec((kd, tn), lambda j, i: (0, j)),
            pl.BlockSpec((tm, tn), lambda j, i: (i, j)),
            pl.BlockSpec((None, 3, tn), lambda j, i: (i // tiles_per_seq, 0, j)),
        ],
        out_specs=pl.BlockSpec((tm, tn), lambda j, i: (i, j)),
        out_shape=jax.ShapeDtypeStruct((t, n), F32),
        compiler_params=_cparams(("arbitrary", "arbitrary")),
        name="outproj",
    )(z, w, x, mod_l)


def _final_norm_kernel(x_ref, g_ref, o_ref):
    x = x_ref[...]
    ms = jnp.mean(x * x, axis=-1, keepdims=True)
    o_ref[...] = x * lax.rsqrt(ms + NORM_EPS) * g_ref[...]


def _final_norm(x, g, *, tm=512):
    t, d = x.shape
    tm = min(tm, t)
    return pl.pallas_call(
        _final_norm_kernel,
        grid=(t // tm,),
        in_specs=[pl.BlockSpec((tm, d), lambda i: (i, 0)), pl.BlockSpec((1, d), lambda i: (0, 0))],
        out_specs=pl.BlockSpec((tm, d), lambda i: (i, 0)),
        out_shape=jax.ShapeDtypeStruct((t, d), F32),
        compiler_params=_cparams(("arbitrary",)),
        name="final_norm",
    )(x, g)


def _rwkv_inproj_kernel(x_ref, xh_ref, g_ref, mod_ref, mu_ref, w_ref, w1_ref, o_ref, lora_ref, xs_scr,
                        *, tiles_per_seq, col_tiles):
    i = pl.program_id(0)
    j = pl.program_id(1)

    @pl.when(j == 0)
    def _():
        g = g_ref[...]
        mod = mod_ref[...]
        h = _prenorm(x_ref[...], g, mod)
        hp8 = _prenorm(xh_ref[...], g, mod)
        first = (i % tiles_per_seq) == 0
        prev_row = jnp.where(first, 0.0, hp8[SUBLANES - 1:SUBLANES, :])
        rolled = pltpu.roll(h, 1, 0)
        row0 = _iota2(h.shape, 0) == 0
        dh = jnp.where(row0, prev_row, rolled) - h
        mu = mu_ref[...]
        for c in range(4):
            xs_scr[c] = (h + dh * mu[c:c + 1, :]).astype(BF16)
        xw = (h + dh * mu[4:5, :]).astype(BF16)
        xa = (h + dh * mu[5:6, :]).astype(BF16)
        w1 = w1_ref[...]
        dec_h = jnp.tanh(jnp.dot(xw, w1[:, :LANES], preferred_element_type=F32))
        icl_h = jnp.dot(xa, w1[:, LANES:], preferred_element_type=F32)
        lora_ref[...] = jnp.concatenate([dec_h, icl_h], axis=1)

    o_ref[...] = jnp.dot(xs_scr[j // col_tiles], w_ref[...], preferred_element_type=F32).astype(o_ref.dtype)


def _rwkv_inproj(x, g, mod_l, mu, w_in, w1, seq, *, tm, tn, out_dtype):
    t, d = x.shape
    n = w_in.shape[1]
    tm = min(tm, seq)
    tiles_per_seq = seq // tm
    halo_blocks = tm // SUBLANES
    kern = functools.partial(_rwkv_inproj_kernel, tiles_per_seq=tiles_per_seq, col_tiles=d // tn)
    return pl.pallas_call(
        kern,
        grid=(t // tm, n // tn),
        in_specs=[
            pl.BlockSpec((tm, d), lambda i, j: (i, 0)),
            pl.BlockSpec((SUBLANES, d), lambda i, j: (jnp.maximum(i * halo_blocks - 1, 0), 0)),
            pl.BlockSpec((1, d), lambda i, j: (0, 0)),
            pl.BlockSpec((None, 3, d), lambda i, j: (i // tiles_per_seq, 0, 0)),
            pl.BlockSpec((6, d), lambda i, j: (0, 0)),
            pl.BlockSpec((d, tn), lambda i, j: (0, j)),
            pl.BlockSpec((d, 2 * LANES), lambda i, j: (0, 0)),
        ],
        out_specs=[
            pl.BlockSpec((tm, tn), lambda i, j: (i, j)),
            pl.BlockSpec((tm, 2 * LANES), lambda i, j: (i, 0)),
        ],
        out_shape=[
            jax.ShapeDtypeStruct((t, n), out_dtype),
            jax.ShapeDtypeStruct((t, 2 * LANES), F32),
        ],
        scratch_shapes=[pltpu.VMEM((4, tm, d), BF16)],
        compiler_params=_cparams(("arbitrary", "arbitrary")),
        name="rwkv_inproj",
    )(x, x, g, mod_l, mu, w_in, w1)


def _unit_lower_inverses(a_list, eye, blk8, offdiag, passes):
    a8 = [jnp.where(blk8, a, 0.0) for a in a_list]
    a2 = [_dot(x, x, passes=passes) for x in a8]
    a4 = [_dot(x, x, passes=passes) for x in a2]
    t1 = [_dot(eye + x, eye + y, passes=passes) for x, y in zip(a8, a2)]
    inv = [_dot(t, eye + z, passes=passes) for t, z in zip(t1, a4)]
    for m in offdiag:
        t = [_dot(jnp.where(m, a, 0.0), i, passes=passes) for a, i in zip(a_list, inv)]
        inv = [i + _dot(i, x, passes=passes) for i, x in zip(inv, t)]
    return inv


def _rwkv_scan_kernel(r_ref, k_ref, v_ref, g_ref, lora_ref, dw2_ref, iw2_ref, dw0_ref, iw0_ref,
                      kk_ref, ka_ref, rk_ref, gnw_ref, gnb_ref, z_ref, s_scr, cum_scr, y_scr,
                      *, chunks, pairs, passes):
    L = RWKV_CHUNK
    N = RWKV_HEAD
    R = L * chunks
    P = 2 * L

    @pl.when(pl.program_id(2) == 0)
    def _():
        s_scr[...] = jnp.zeros_like(s_scr)

    lane = _iota2((1, LANES), 1)
    m0 = lane < N
    mf0 = m0.astype(F32)
    mf1 = 1.0 - mf0

    def headsum(x):
        s0 = jnp.sum(jnp.where(m0, x, 0.0), axis=-1, keepdims=True)
        s1 = jnp.sum(jnp.where(m0, 0.0, x), axis=-1, keepdims=True)
        return jnp.where(m0, s0, s1)

    def stack(x):
        return jnp.concatenate([x * mf0, x * mf1], axis=0)

    lora = lora_ref[...]
    ri = _iota2((R, R), 0)
    ci = _iota2((R, R), 1)
    tri = jnp.where((ri // L == ci // L) & (ci <= ri), 1.0, 0.0).astype(BF16)

    pair_vals = []
    for p in range(pairs):
        cols = slice(p * LANES, (p + 1) * LANES)
        r = r_ref[:, cols].astype(F32)
        k = k_ref[:, cols].astype(F32)
        v = v_ref[:, cols].astype(F32)
        dec = dw0_ref[:, cols] + _dot(lora[:, :LANES], dw2_ref[:, cols], passes=GATE_PASSES)
        lw = -jnp.exp(-_softplus(-dec) - 0.5)
        a = jax.nn.sigmoid(iw0_ref[:, cols] + _dot(lora[:, LANES:], iw2_ref[:, cols], passes=GATE_PASSES))
        kkr = k * kk_ref[:, cols]
        kk = kkr / jnp.maximum(jnp.sqrt(headsum(kkr * kkr)), 1e-12)
        k2 = k * (1.0 + (a - 1.0) * ka_ref[:, cols])
        cum = _dot_exact_lhs(tri, lw)
        cum_scr[:, cols] = cum
        pair_vals.append(dict(r=r, k2=k2, v=v, av=-kk, bv=kk * a, cum=cum, cumex=cum - lw))

    rp = _iota2((P, P), 0)
    cp = _iota2((P, P), 1)
    strict = (rp % L) > (cp % L)
    incl = (rp % L) >= (cp % L)
    eye = jnp.where(rp == cp, 1.0, 0.0)
    blk8 = (rp // 8) == (cp // 8)
    offdiag = [((rp // (2 * b)) == (cp // (2 * b))) & ((rp // b) != (cp // b)) for b in (8, 16, 32)]

    items = [(p, c) for p in range(pairs) for c in range(chunks)]
    pre = []
    for p, c in items:
        pv = pair_vals[p]
        cols = slice(p * LANES, (p + 1) * LANES)
        sl = slice(c * L, (c + 1) * L)
        cref = cum_scr[pl.ds(c * L + L // 2 - 1, 1), cols]
        clast = cum_scr[pl.ds(c * L + L - 1, 1), cols]
        cum_c, cumex_c = pv["cum"][sl], pv["cumex"][sl]
        r_c, k_c, v_c, av_c, bv_c = pv["r"][sl], pv["k2"][sl], pv["v"][sl], pv["av"][sl], pv["bv"][sl]
        e_out = jnp.exp(cref - cum_c)
        e_end = jnp.exp(clast - cum_c)
        pre.append(dict(
            lhs1=jnp.concatenate([stack(av_c * jnp.exp(cumex_c - cref)), stack(r_c * jnp.exp(cum_c - cref))], axis=0),
            rhs1=jnp.concatenate([stack(bv_c * e_out), stack(k_c * e_out)], axis=0),
            v_st=stack(v_c), a0_st=stack(av_c * jnp.exp(cumex_c)), r0_st=stack(r_c * jnp.exp(cum_c)),
            bk=jnp.concatenate([stack(bv_c * e_end), stack(k_c * e_end)], axis=0), decay=jnp.exp(clast)))

    x1 = [_dot(f["lhs1"], f["rhs1"], nt=True, passes=passes) for f in pre]
    a_ab = [jnp.where(strict, x[:P, :P], 0.0) for x in x1]
    a_kr = [jnp.concatenate([jnp.where(strict, x[:P, P:], 0.0), jnp.where(incl, x[P:, P:], 0.0)], axis=0) for x in x1]
    a_rb = [jnp.where(incl, x[P:, :P], 0.0) for x in x1]
    m1 = [_dot(a, f["v_st"], passes=passes) for a, f in zip(a_kr, pre)]
    tinv = _unit_lower_inverses(a_ab, eye, blk8, offdiag, passes)
    m2 = [_dot(t, jnp.concatenate([f["a0_st"], m[:P]], axis=1), passes=passes) for t, f, m in zip(tinv, pre, m1)]
    fac = {}
    for it, f, m, mm, arb in zip(items, pre, m1, m2, a_rb):
        w_st, uv_st = mm[:, :LANES], mm[:, LANES:]
        fac[it] = dict(wr=jnp.concatenate([w_st, f["r0_st"]], axis=0), w_st=w_st, uv_st=uv_st, uv_t=uv_st.T,
                       yv_st=m[P:], a_rb=arb, v_t=f["v_st"].T, decay=f["decay"], bk=f["bk"])

    s = [s_scr[p] for p in range(pairs)]
    for c in range(chunks):
        fs = [fac[(p, c)] for p in range(pairs)]
        ut = [_dot(s[p], fs[p]["w_st"], nt=True, passes=passes) + fs[p]["uv_t"] for p in range(pairs)]
        m3 = [_dot(fs[p]["wr"], s[p], nt=True, passes=passes) for p in range(pairs)]
        s = [s[p] * fs[p]["decay"] + _dot(jnp.concatenate([ut[p], fs[p]["v_t"]], axis=1), fs[p]["bk"], passes=passes)
             for p in range(pairs)]
        for p in range(pairs):
            u_st = m3[p][:P] + fs[p]["uv_st"]
            y_st = m3[p][P:] + fs[p]["yv_st"] + _dot(fs[p]["a_rb"], u_st, passes=passes)
            y_scr[c * L:(c + 1) * L, p * LANES:(p + 1) * LANES] = y_st[:L] + y_st[L:]
    for p in range(pairs):
        s_scr[p] = s[p]

    for p in range(pairs):
        cols = slice(p * LANES, (p + 1) * LANES)
        pv = pair_vals[p]
        y = y_scr[:, cols]
        mean = headsum(y) * (1.0 / N)
        yc = y - mean
        var = headsum(yc * yc) * (1.0 / N)
        yn = yc * lax.rsqrt(var + RWKV_GN_EPS) * gnw_ref[:, cols] + gnb_ref[:, cols]
        bonus = headsum(pv["r"] * pv["k2"] * rk_ref[:, cols]) * pv["v"]
        z_ref[:, cols] = ((yn + bonus) * _silu(g_ref[:, cols].astype(F32))).astype(z_ref.dtype)


def _rwkv_scan(rkvg, lora, p, batch, seq, *, out_dtype):
    t = rkvg.shape[0]
    w = D_MODEL
    pairs = RWKV_PAIRS_PER_STEP
    bw = pairs * LANES
    nblk = w // bw
    chunks = min(RWKV_CHUNKS_PER_STEP, seq // RWKV_CHUNK)
    rows = RWKV_CHUNK * chunks
    steps = seq // rows

    def act(col0):
        return pl.BlockSpec((rows, bw), lambda b, h, n: (b * steps + n, col0 + h))

    def vec():
        return pl.BlockSpec((1, bw), lambda b, h, n: (0, h))

    def w2():
        return pl.BlockSpec((LANES, bw), lambda b, h, n: (0, h))

    kern = functools.partial(_rwkv_scan_kernel, chunks=chunks, pairs=pairs, passes=RWKV_PASSES)
    return pl.pallas_call(
        kern,
        grid=(batch, nblk, steps),
        in_specs=[act(0), act(nblk), act(2 * nblk), act(3 * nblk),
                  pl.BlockSpec((rows, 2 * LANES), lambda b, h, n: (b * steps + n, 0)),
                  w2(), w2(), vec(), vec(), vec(), vec(), vec(), vec(), vec()],
        out_specs=pl.BlockSpec((rows, bw), lambda b, h, n: (b * steps + n, h)),
        out_shape=jax.ShapeDtypeStruct((t, w), out_dtype),
        scratch_shapes=[pltpu.VMEM((pairs, LANES, LANES), F32), pltpu.VMEM((rows, bw), F32),
                        pltpu.VMEM((rows, bw), F32)],
        compiler_params=_cparams(("arbitrary", "arbitrary", "arbitrary")),
        name="rwkv_scan",
    )(rkvg, rkvg, rkvg, rkvg, lora, p["dec_w2"], p["iclr_w2"], p["dec_w0"], p["iclr_w0"],
      p["k_k"], p["k_a"], p["r_k"], p["gn_w"], p["gn_b"])


def _gla_kernel(q_ref, k_ref, v_ref, g_ref, low_ref, w2_ref, b_ref, hg_ref, o_ref, s_scr):
    R = GLA_BLOCK
    Hf = R // 2

    @pl.when(pl.program_id(1) == 0)
    def _():
        s_scr[...] = jnp.zeros_like(s_scr)

    DK, DV = GLA_HEAD_K, GLA_HEAD_V
    heads = range(GLA_HEADS)
    ri = _iota2((R, R), 0)
    ci = _iota2((R, R), 1)
    tri = jnp.where(ci <= ri, 1.0, 0.0).astype(BF16)
    rh = _iota2((Hf, Hf), 0)
    ch = _iota2((Hf, Hf), 1)
    causal = ch <= rh

    la = -_softplus(-(_dot(low_ref[...], w2_ref[...], passes=GATE_PASSES) + b_ref[...])) * (1.0 / GLA_GATE_TAU)
    bcum = _dot_exact_lhs(tri, la)
    q = [q_ref[:, h * DK:(h + 1) * DK].astype(F32) * (DK ** -0.5) for h in heads]
    k = [k_ref[:, h * DK:(h + 1) * DK].astype(F32) for h in heads]
    v = [v_ref[:, h * DV:(h + 1) * DV].astype(F32) for h in heads]
    bc = [bcum[:, h * DK:(h + 1) * DK] for h in heads]
    ref_t = [b[Hf // 2 - 1:Hf // 2] for b in bc]
    ref_m = [b[Hf - 1:Hf] for b in bc]
    ref_b = [b[Hf + Hf // 2 - 1:Hf + Hf // 2] for b in bc]
    last = [b[R - 1:R] for b in bc]
    s00 = [jnp.where(causal, _dot(q[h][:Hf] * jnp.exp(bc[h][:Hf] - ref_t[h]),
                                  k[h][:Hf] * jnp.exp(ref_t[h] - bc[h][:Hf]), nt=True), 0.0) for h in heads]
    s11 = [jnp.where(causal, _dot(q[h][Hf:] * jnp.exp(bc[h][Hf:] - ref_b[h]),
                                  k[h][Hf:] * jnp.exp(ref_b[h] - bc[h][Hf:]), nt=True), 0.0) for h in heads]
    s10 = [_dot(q[h][Hf:] * jnp.exp(bc[h][Hf:] - ref_m[h]), k[h][:Hf] * jnp.exp(ref_m[h] - bc[h][:Hf]), nt=True)
           for h in heads]
    st = [s_scr[h] for h in heads]
    o_int = [_dot(q[h] * jnp.exp(bc[h]), st[h], nt=True) for h in heads]
    o_top = [_dot(s00[h], v[h][:Hf]) for h in heads]
    o_bot = [_dot(s10[h], v[h][:Hf]) + _dot(s11[h], v[h][Hf:]) for h in heads]
    upd = [_dot(v[h].T, k[h] * jnp.exp(last[h] - bc[h])) for h in heads]
    for h in heads:
        s_scr[h] = st[h] * jnp.exp(last[h]) + upd[h]
        o = jnp.concatenate([o_top[h], o_bot[h]], axis=0) + o_int[h]
        ms = jnp.mean(o * o, axis=-1, keepdims=True)
        on = o * lax.rsqrt(ms + NORM_EPS) * hg_ref[...]
        cols = slice(h * DV, (h + 1) * DV)
        o_ref[:, cols] = (on * _silu(g_ref[:, cols].astype(F32))).astype(o_ref.dtype)


def _gla_scan(proj, low, gate_w2, gate_b, head_g, batch, seq, *, out_dtype):
    t = proj.shape[0]
    R = GLA_BLOCK
    steps = seq // R
    kw, vw = GLA_KEY_WIDTH, GLA_VALUE_WIDTH

    def row(b, n):
        return b * steps + n

    return pl.pallas_call(
        _gla_kernel,
        grid=(batch, steps),
        in_specs=[
            pl.BlockSpec((R, kw), lambda b, n: (row(b, n), 0)),
            pl.BlockSpec((R, kw), lambda b, n: (row(b, n), 1)),
            pl.BlockSpec((R, vw), lambda b, n: (row(b, n), 2 * kw // vw)),
            pl.BlockSpec((R, vw), lambda b, n: (row(b, n), 2 * kw // vw + 1)),
            pl.BlockSpec((R, LANES), lambda b, n: (row(b, n), 0)),
            pl.BlockSpec((LANES, kw), lambda b, n: (0, 0)),
            pl.BlockSpec((1, kw), lambda b, n: (0, 0)),
            pl.BlockSpec((1, GLA_HEAD_V), lambda b, n: (0, 0)),
        ],
        out_specs=pl.BlockSpec((R, vw), lambda b, n: (row(b, n), 0)),
        out_shape=jax.ShapeDtypeStruct((t, vw), out_dtype),
        scratch_shapes=[pltpu.VMEM((GLA_HEADS, GLA_HEAD_V, GLA_HEAD_K), F32)],
        compiler_params=_cparams(("arbitrary", "arbitrary")),
        name="gla_scan",
    )(proj, proj, proj, proj, low, gate_w2, gate_b, head_g)


def _ssd_kernel(z_ref, xr_ref, br_ref, cr_ref, dt_ref, cw_ref, cbias_ref, dtb_ref, alog_ref, dsk_ref, ng_ref, o_ref,
                st_scr, raw_scr, xs_ref, bm_ref, cm_ref):
    C = SSD_CHUNK
    P = SSM_HEADDIM
    pairs_per_group = (SSM_HEADS // SSM_GROUPS) // 2
    gw = SSM_WIDTH // SSM_GROUPS

    @pl.when(pl.program_id(1) == 0)
    def _():
        st_scr[...] = jnp.zeros_like(st_scr)
        raw_scr[...] = jnp.zeros_like(raw_scr)

    raw_scr[0:SUBLANES, :] = raw_scr[C:C + SUBLANES, :]
    col = 0
    for src_ref, act_ref in ((xr_ref, xs_ref), (br_ref, bm_ref), (cr_ref, cm_ref)):
        width = src_ref.shape[1]
        cols = slice(col, col + width)
        raw_scr[SUBLANES:, cols] = src_ref[...].astype(F32)
        acc = cbias_ref[:, cols]
        for s in range(SSM_CONV):
            tap = raw_scr[SUBLANES - s:SUBLANES - s + C, cols]
            acc = acc + tap * cw_ref[SSM_CONV - 1 - s:SSM_CONV - s, cols]
        act_ref[...] = _silu(acc)
        col += width

    dt = _softplus(dt_ref[...] + dtb_ref[...])
    da = dt * (-jnp.exp(alog_ref[...]))
    ri = _iota2((C, C), 0)
    ci = _iota2((C, C), 1)
    causal = ci <= ri
    tri = jnp.where(causal, 1.0, 0.0).astype(BF16)
    acum = _dot_exact_lhs(tri, da)
    acum_t = acum.T
    alast = acum[C - 1:C, :]
    dsk = dsk_ref[...]

    lane = _iota2((1, LANES), 1)
    m0 = lane < P
    mf0 = m0.astype(F32)
    mf1 = 1.0 - mf0

    def pair_cols(x, h0):
        return jnp.where(m0, x[:, h0:h0 + 1], x[:, h0 + 1:h0 + 2])

    for g in range(SSM_GROUPS):
        bm = bm_ref[:, g * SSM_STATE:(g + 1) * SSM_STATE].astype(F32)
        cm = cm_ref[:, g * SSM_STATE:(g + 1) * SSM_STATE].astype(F32)
        cb = _dot(cm, bm, nt=True)
        bm_t = bm.T
        ys = []
        for pp in range(pairs_per_group):
            pidx = g * pairs_per_group + pp
            h0 = 2 * pidx
            cols = slice(pidx * LANES, (pidx + 1) * LANES)
            x_p = xs_ref[:, cols].astype(F32)
            dt_p = pair_cols(dt, h0)
            ac_p = pair_cols(acum, h0)
            al_p = pair_cols(alast, h0)
            xc = x_p * dt_p
            dec0 = jnp.where(causal, jnp.exp(acum[:, h0:h0 + 1] - acum_t[h0:h0 + 1, :]), 0.0)
            dec1 = jnp.where(causal, jnp.exp(acum[:, h0 + 1:h0 + 2] - acum_t[h0 + 1:h0 + 2, :]), 0.0)
            lhs = jnp.concatenate([cb * dec0, cb * dec1], axis=1)
            rhs = jnp.concatenate([xc * mf0, xc * mf1], axis=0)
            prev = st_scr[pidx]
            y = _dot(lhs, rhs) + _dot(cm, prev) * jnp.exp(ac_p) + pair_cols(dsk, h0) * x_p
            st_scr[pidx] = prev * jnp.exp(al_p) + _dot(bm_t, xc * jnp.exp(al_p - ac_p))
            ys.append(y * _silu(z_ref[:, cols].astype(F32)))
        yg = jnp.concatenate(ys, axis=1)
        ms = jnp.mean(yg * yg, axis=-1, keepdims=True)
        o_ref[:, g * gw:(g + 1) * gw] = (yg * lax.rsqrt(ms + SSM_NORM_EPS) * ng_ref[:, g * gw:(g + 1) * gw]).astype(o_ref.dtype)


def _ssd_scan(proj, dt, conv_w, conv_b, dt_bias, a_log, d_skip, norm_g, batch, seq, *, out_dtype):
    t = proj.shape[0]
    C = SSD_CHUNK
    steps = seq // C
    npairs = SSM_HEADS // 2

    def row(b, n):
        return b * steps + n

    bcol = 2 * SSM_WIDTH // SSM_BC_WIDTH
    return pl.pallas_call(
        _ssd_kernel,
        grid=(batch, steps),
        in_specs=[
            pl.BlockSpec((C, SSM_WIDTH), lambda b, n: (row(b, n), 0)),
            pl.BlockSpec((C, SSM_WIDTH), lambda b, n: (row(b, n), 1)),
            pl.BlockSpec((C, SSM_BC_WIDTH), lambda b, n: (row(b, n), bcol)),
            pl.BlockSpec((C, SSM_BC_WIDTH), lambda b, n: (row(b, n), bcol + 1)),
            pl.BlockSpec((C, LANES), lambda b, n: (row(b, n), 0)),
            pl.BlockSpec((SSM_CONV, SSM_CONV_WIDTH), lambda b, n: (0, 0)),
            pl.BlockSpec((1, SSM_CONV_WIDTH), lambda b, n: (0, 0)),
            pl.BlockSpec((1, LANES), lambda b, n: (0, 0)),
            pl.BlockSpec((1, LANES), lambda b, n: (0, 0)),
            pl.BlockSpec((1, LANES), lambda b, n: (0, 0)),
            pl.BlockSpec((1, SSM_WIDTH), lambda b, n: (0, 0)),
        ],
        out_specs=pl.BlockSpec((C, SSM_WIDTH), lambda b, n: (row(b, n), 0)),
        out_shape=jax.ShapeDtypeStruct((t, SSM_WIDTH), out_dtype),
        scratch_shapes=[pltpu.VMEM((npairs, SSM_STATE, LANES), F32), pltpu.VMEM((C + SUBLANES, SSM_CONV_WIDTH), F32),
                        pltpu.VMEM((C, SSM_WIDTH), F32), pltpu.VMEM((C, SSM_BC_WIDTH), F32),
                        pltpu.VMEM((C, SSM_BC_WIDTH), F32)],
        compiler_params=_cparams(("arbitrary", "arbitrary")),
        name="ssd_scan",
    )(proj, proj, proj, proj, dt, conv_w, conv_b, dt_bias, a_log, d_skip, norm_g)


def _pad_cols(w, n):
    return jnp.pad(w, ((0, 0), (0, n - w.shape[1])))


def _pad_rows(w, n):
    return jnp.pad(w, ((0, n - w.shape[0]), (0, 0)))


ACT_DTYPE = BF16


def kernel(x, c, ada_w, ada_b, norm_g, final_g, rwkv_mu, rwkv_w_in, rwkv_dec_w1, rwkv_dec_w2, rwkv_dec_w0, rwkv_iclr_w1, rwkv_iclr_w2, rwkv_iclr_w0, rwkv_k_k, rwkv_k_a, rwkv_r_k, rwkv_gn_w, rwkv_gn_b, rwkv_w_out, gla_w_in, gla_gate_w2, gla_gate_b, gla_head_g, gla_w_out, ssd_w_in, ssd_conv_w, ssd_conv_b, ssd_dt_bias, ssd_a_log, ssd_d, ssd_norm_g, ssd_w_out):
    batch, seq, d = x.shape
    t = batch * seq
    xf = x.reshape(t, d)

    c_pad = jnp.pad(c, ((0, SUBLANES - batch % SUBLANES if batch % SUBLANES else 0), (0, 0)))
    mod = _ada_mod(c_pad, ada_w, ada_b)[:, :batch].reshape(DEPTH, batch, 3, d)

    for i in range(DEPTH):
        kind, j = i % N_MIXERS, i // N_MIXERS
        g = norm_g[i].reshape(1, d)
        mod_l = mod[i]
        if kind == 0:
            w1 = jnp.concatenate([_pad_cols(rwkv_dec_w1[j], LANES), _pad_cols(rwkv_iclr_w1[j], LANES)], axis=1)
            rkvg, lora = _rwkv_inproj(xf, g, mod_l, rwkv_mu[j], rwkv_w_in[j].astype(BF16), w1.astype(BF16), seq,
                                      tm=ROW_TILE, tn=1024, out_dtype=ACT_DTYPE)
            row = lambda v: v.reshape(1, -1)
            params = dict(dec_w2=_pad_rows(rwkv_dec_w2[j], LANES), iclr_w2=_pad_rows(rwkv_iclr_w2[j], LANES),
                          dec_w0=row(rwkv_dec_w0[j]), iclr_w0=row(rwkv_iclr_w0[j]), k_k=row(rwkv_k_k[j]),
                          k_a=row(rwkv_k_a[j]), r_k=row(rwkv_r_k[j]), gn_w=row(rwkv_gn_w[j]), gn_b=row(rwkv_gn_b[j]))
            z = _rwkv_scan(rkvg, lora, params, batch, seq, out_dtype=ACT_DTYPE)
            xf = _outproj(z, rwkv_w_out[j].astype(BF16), xf, mod_l, seq, tm=ROW_TILE, tn=1024)
        elif kind == 1:
            nmain = 2 * GLA_KEY_WIDTH + 2 * GLA_VALUE_WIDTH
            w = gla_w_in[j]
            proj, low = _inproj(xf, g, mod_l, w[:, :nmain].astype(BF16), _pad_cols(w[:, nmain:], LANES).astype(BF16),
                                seq, tm=ROW_TILE, tn=1024, out_dtype=ACT_DTYPE)
            z = _gla_scan(proj, low, _pad_rows(gla_gate_w2[j], LANES), gla_gate_b[j].reshape(1, -1),
                          gla_head_g[j].reshape(1, -1), batch, seq, out_dtype=ACT_DTYPE)
            xf = _outproj(z, gla_w_out[j].astype(BF16), xf, mod_l, seq, tm=ROW_TILE, tn=1024)
        else:
            nmain = SSM_WIDTH + SSM_CONV_WIDTH
            w = ssd_w_in[j]
            proj, dt = _inproj(xf, g, mod_l, w[:, :nmain].astype(BF16), _pad_cols(w[:, nmain:], LANES).astype(BF16),
                               seq, tm=ROW_TILE, tn=1024, out_dtype=ACT_DTYPE)
            padl = lambda v: _pad_cols(v.reshape(1, -1), LANES)
            z = _ssd_scan(proj, dt, ssd_conv_w[j], ssd_conv_b[j].reshape(1, -1),
                          padl(ssd_dt_bias[j]), padl(ssd_a_log[j]), padl(ssd_d[j]),
                          ssd_norm_g[j].reshape(1, -1), batch, seq, out_dtype=ACT_DTYPE)
            xf = _outproj(z, ssd_w_out[j].astype(BF16), xf, mod_l, seq, tm=ROW_TILE, tn=1024)

    out = _final_norm(xf, final_g.reshape(1, d))
    return out.reshape(batch, seq, d)
```

```python
import functools
import math

import jax
import jax.numpy as jnp
from jax import lax
from jax.experimental import pallas as pl
from jax.experimental.pallas import tpu as pltpu

F32 = jnp.float32
BF16 = jnp.bfloat16

D_MODEL = 2048
DEPTH = 4
N_MIXERS = 3
NORM_EPS = 1e-6

RWKV_HEAD = 64
RWKV_LORA = 96
RWKV_GN_EPS = 64e-5

GLA_HEADS = 4
GLA_KEY_WIDTH = D_MODEL // 2
GLA_VALUE_WIDTH = D_MODEL
GLA_HEAD_K = GLA_KEY_WIDTH // GLA_HEADS
GLA_HEAD_V = GLA_VALUE_WIDTH // GLA_HEADS
GLA_GATE_RANK = 16
GLA_GATE_TAU = 16.0

SSM_WIDTH = 2 * D_MODEL
SSM_HEADDIM = 64
SSM_HEADS = SSM_WIDTH // SSM_HEADDIM
SSM_STATE = 128
SSM_GROUPS = 8
SSM_CONV = 4
SSM_NORM_EPS = 1e-5
SSM_BC_WIDTH = SSM_GROUPS * SSM_STATE
SSM_CONV_WIDTH = SSM_WIDTH + 2 * SSM_BC_WIDTH

LANES = 128
SUBLANES = 8
VMEM_LIMIT_BYTES = 52 * 1024 * 1024

RWKV_CHUNK = 64
RWKV_CHUNKS_PER_STEP = 4
RWKV_PAIRS_PER_STEP = 4
GLA_BLOCK = 128
SSD_CHUNK = 128

RWKV_PASSES = 1
GATE_PASSES = 3

ROW_TILE = 512
PROJ_ROW_TILE = 1024


def _cparams(sem):
    return pltpu.CompilerParams(dimension_semantics=sem, vmem_limit_bytes=VMEM_LIMIT_BYTES)


def _dot(a, b, *, nt=False, passes=1):
    dims = (((1,), (1,)), ((), ())) if nt else (((1,), (0,)), ((), ()))

    def d(x, y):
        return lax.dot_general(x, y, dims, preferred_element_type=F32)

    ah = a.astype(BF16)
    bh = b.astype(BF16)
    if passes == 1:
        return d(ah, bh)
    al = (a.astype(F32) - ah.astype(F32)).astype(BF16)
    bl = (b.astype(F32) - bh.astype(F32)).astype(BF16)
    return d(ah, bh) + (d(ah, bl) + d(al, bh))


def _dot_exact_lhs(m_bf16, x):
    x1 = x.astype(BF16)
    r1 = x - x1.astype(F32)
    x2 = r1.astype(BF16)
    x3 = (r1 - x2.astype(F32)).astype(BF16)

    def d(y):
        return jnp.dot(m_bf16, y, preferred_element_type=F32)

    return d(x1) + (d(x2) + d(x3))


def _silu(x):
    return x * jax.nn.sigmoid(x)


def _softplus(x):
    return jnp.maximum(x, 0.0) + jnp.log(1.0 + jnp.exp(-jnp.abs(x)))


def _iota2(shape, dim):
    return lax.broadcasted_iota(jnp.int32, shape, dim)


def _ada_kernel(c_ref, w_ref, b_ref, o_ref):
    c = c_ref[...]
    o_ref[...] = _dot(_silu(c), w_ref[...]) + b_ref[...]


def _ada_mod(c_pad, ada_w, ada_b, tn=1024):
    depth, d, n = ada_w.shape
    rows = c_pad.shape[0]
    return pl.pallas_call(
        _ada_kernel,
        grid=(depth, n // tn),
        in_specs=[
            pl.BlockSpec((rows, d), lambda l, j: (0, 0)),
            pl.BlockSpec((None, d, tn), lambda l, j: (l, 0, j)),
            pl.BlockSpec((None, 1, tn), lambda l, j: (l, 0, j)),
        ],
        out_specs=pl.BlockSpec((None, rows, tn), lambda l, j: (l, 0, j)),
        out_shape=jax.ShapeDtypeStruct((depth, rows, n), F32),
        compiler_params=_cparams(("arbitrary", "arbitrary")),
        name="ada_mod",
    )(c_pad, ada_w, ada_b.reshape(depth, 1, n))


def _prenorm(x, g, mod):
    ms = jnp.mean(x * x, axis=-1, keepdims=True)
    return x * lax.rsqrt(ms + NORM_EPS) * g * (1.0 + mod[1:2, :]) + mod[0:1, :]


def _prenorm_kernel(x_ref, g_ref, mod_ref, ws_ref, h_ref, os_ref):
    h = _prenorm(x_ref[...], g_ref[...], mod_ref[...]).astype(BF16)
    h_ref[...] = h
    os_ref[...] = jnp.dot(h, ws_ref[...], preferred_element_type=F32)


def _prenorm_call(x, g, mod_l, w_side, seq, *, tm):
    t, d = x.shape
    ns = w_side.shape[1]
    tm = min(tm, seq)
    tiles_per_seq = seq // tm
    return pl.pallas_call(
        _prenorm_kernel,
        grid=(t // tm,),
        in_specs=[
            pl.BlockSpec((tm, d), lambda i: (i, 0)),
            pl.BlockSpec((1, d), lambda i: (0, 0)),
            pl.BlockSpec((None, 3, d), lambda i: (i // tiles_per_seq, 0, 0)),
            pl.BlockSpec((d, ns), lambda i: (0, 0)),
        ],
        out_specs=[pl.BlockSpec((tm, d), lambda i: (i, 0)), pl.BlockSpec((tm, ns), lambda i: (i, 0))],
        out_shape=[jax.ShapeDtypeStruct((t, d), BF16), jax.ShapeDtypeStruct((t, ns), F32)],
        compiler_params=_cparams(("arbitrary",)),
        name="prenorm",
    )(x, g, mod_l, w_side)


def _proj_kernel(a_ref, w_ref, o_ref, wb_scr):
    @pl.when(pl.program_id(1) == 0)
    def _():
        wb_scr[...] = w_ref[...].astype(BF16)

    o_ref[...] = jnp.dot(a_ref[...], wb_scr[...], preferred_element_type=F32).astype(o_ref.dtype)


def _proj(a, w, layer, n, seq, *, tm, tn, out_dtype):
    groups, t, kd = a.shape
    tm = min(tm, seq)
    tiles_per_group = (n // tn) // groups
    return pl.pallas_call(
        _proj_kernel,
        grid=(n // tn, t // tm),
        in_specs=[
            pl.BlockSpec((None, tm, kd), lambda j, i: (j // tiles_per_group, i, 0)),
            pl.BlockSpec((None, kd, tn), lambda j, i: (layer, 0, j)),
        ],
        out_specs=pl.BlockSpec((tm, tn), lambda j, i: (i, j)),
        out_shape=jax.ShapeDtypeStruct((t, n), out_dtype),
        scratch_shapes=[pltpu.VMEM((kd, tn), BF16)],
        compiler_params=_cparams(("arbitrary", "arbitrary")),
        name="proj",
    )(a, w)


def _outproj_kernel(z_ref, w_ref, x_ref, mod_ref, o_ref, wb_scr):
    @pl.when(pl.program_id(1) == 0)
    def _():
        wb_scr[...] = w_ref[...].astype(BF16)

    acc = jnp.dot(z_ref[...], wb_scr[...], preferred_element_type=F32)
    o_ref[...] = x_ref[...] + mod_ref[2:3, :] * acc


def _outproj(z, w, layer, x, mod_l, seq, *, tm, tn):
    t, kd = z.shape
    n = w.shape[2]
    tm = min(tm, seq)
    tiles_per_seq = seq // tm
    return pl.pallas_call(
        _outproj_kernel,
        grid=(n // tn, t // tm),
        in_specs=[
            pl.BlockSpec((tm, kd), lambda j, i: (i, 0)),
            pl.BlockSpec((None, kd, tn), lambda j, i: (layer, 0, j)),
            pl.BlockSpec((tm, tn), lambda j, i: (i, j)),
            pl.BlockSpec((None, 3, tn), lambda j, i: (i // tiles_per_seq, 0, j)),
        ],
        out_specs=pl.BlockSpec((tm, tn), lambda j, i: (i, j)),
        out_shape=jax.ShapeDtypeStruct((t, n), F32),
        scratch_shapes=[pltpu.VMEM((kd, tn), BF16)],
        compiler_params=_cparams(("arbitrary", "arbitrary")),
        name="outproj",
    )(z, w, x, mod_l)


def _final_norm_kernel(x_ref, g_ref, o_ref):
    x = x_ref[...]
    ms = jnp.mean(x * x, axis=-1, keepdims=True)
    o_ref[...] = x * lax.rsqrt(ms + NORM_EPS) * g_ref[...]


def _final_norm(x, g, *, tm=512):
    t, d = x.shape
    tm = min(tm, t)
    return pl.pallas_call(
        _final_norm_kernel,
        grid=(t // tm,),
        in_specs=[pl.BlockSpec((tm, d), lambda i: (i, 0)), pl.BlockSpec((1, d), lambda i: (0, 0))],
        out_specs=pl.BlockSpec((tm, d), lambda i: (i, 0)),
        out_shape=jax.ShapeDtypeStruct((t, d), F32),
        compiler_params=_cparams(("arbitrary",)),
        name="final_norm",
    )(x, g)


def _rwkv_prenorm_kernel(x_ref, xh_ref, g_ref, mod_ref, mu_ref, w1_ref, xs_ref, lora_ref, *, tiles_per_seq):
    i = pl.program_id(0)
    g = g_ref[...]
    mod = mod_ref[...]
    h = _prenorm(x_ref[...], g, mod)
    hp8 = _prenorm(xh_ref[...], g, mod)
    first = (i % tiles_per_seq) == 0
    prev_row = jnp.where(first, 0.0, hp8[SUBLANES - 1:SUBLANES, :])
    rolled = pltpu.roll(h, 1, 0)
    row0 = _iota2(h.shape, 0) == 0
    dh = jnp.where(row0, prev_row, rolled) - h
    mu = mu_ref[...]
    for c in range(4):
        xs_ref[c] = (h + dh * mu[c:c + 1, :]).astype(BF16)
    xw = (h + dh * mu[4:5, :]).astype(BF16)
    xa = (h + dh * mu[5:6, :]).astype(BF16)
    w1 = w1_ref[...]
    dec_h = jnp.tanh(jnp.dot(xw, w1[:, :LANES], preferred_element_type=F32))
    icl_h = jnp.dot(xa, w1[:, LANES:], preferred_element_type=F32)
    lora_ref[...] = jnp.concatenate([dec_h, icl_h], axis=1)


def _rwkv_prenorm(x, g, mod_l, mu, w1, seq, *, tm):
    t, d = x.shape
    tm = min(tm, seq)
    tiles_per_seq = seq // tm
    halo_blocks = tm // SUBLANES
    kern = functools.partial(_rwkv_prenorm_kernel, tiles_per_seq=tiles_per_seq)
    return pl.pallas_call(
        kern,
        grid=(t // tm,),
        in_specs=[
            pl.BlockSpec((tm, d), lambda i: (i, 0)),
            pl.BlockSpec((SUBLANES, d), lambda i: (jnp.maximum(i * halo_blocks - 1, 0), 0)),
            pl.BlockSpec((1, d), lambda i: (0, 0)),
            pl.BlockSpec((None, 3, d), lambda i: (i // tiles_per_seq, 0, 0)),
            pl.BlockSpec((6, d), lambda i: (0, 0)),
            pl.BlockSpec((d, 2 * LANES), lambda i: (0, 0)),
        ],
        out_specs=[
            pl.BlockSpec((4, tm, d), lambda i: (0, i, 0)),
            pl.BlockSpec((tm, 2 * LANES), lambda i: (i, 0)),
        ],
        out_shape=[
            jax.ShapeDtypeStruct((4, t, d), BF16),
            jax.ShapeDtypeStruct((t, 2 * LANES), F32),
        ],
        compiler_params=_cparams(("arbitrary",)),
        name="rwkv_prenorm",
    )(x, x, g, mod_l, mu, w1)


def _unit_lower_inverses(a_list, eye, blk8, offdiag, passes):
    a8 = [jnp.where(blk8, a, 0.0) for a in a_list]
    a2 = [_dot(x, x, passes=passes) for x in a8]
    a4 = [_dot(x, x, passes=passes) for x in a2]
    t1 = [_dot(eye + x, eye + y, passes=passes) for x, y in zip(a8, a2)]
    inv = [_dot(t, eye + z, passes=passes) for t, z in zip(t1, a4)]
    for m in offdiag:
        t = [_dot(jnp.where(m, a, 0.0), i, passes=passes) for a, i in zip(a_list, inv)]
        inv = [i + _dot(i, x, passes=passes) for i, x in zip(inv, t)]
    return inv


def _rwkv_scan_kernel(r_ref, k_ref, v_ref, g_ref, lora_ref, dw2_ref, iw2_ref, dw0_ref, iw0_ref,
                      kk_ref, ka_ref, rk_ref, gnw_ref, gnb_ref, z_ref, s_scr, cum_scr, y_scr,
                      *, chunks, pairs, passes):
    L = RWKV_CHUNK
    N = RWKV_HEAD
    R = L * chunks
    P = 2 * L

    @pl.when(pl.program_id(2) == 0)
    def _():
        s_scr[...] = jnp.zeros_like(s_scr)

    lane = _iota2((1, LANES), 1)
    m0 = lane < N
    mf0 = m0.astype(F32)
    mf1 = 1.0 - mf0

    def headsum(x):
        s0 = jnp.sum(jnp.where(m0, x, 0.0), axis=-1, keepdims=True)
        s1 = jnp.sum(jnp.where(m0, 0.0, x), axis=-1, keepdims=True)
        return jnp.where(m0, s0, s1)

    def stack(x):
        return jnp.concatenate([x * mf0, x * mf1], axis=0)

    lora = lora_ref[...]
    ri = _iota2((R, R), 0)
    ci = _iota2((R, R), 1)
    tri = jnp.where((ri // L == ci // L) & (ci <= ri), 1.0, 0.0).astype(BF16)

    pair_vals = []
    for p in range(pairs):
        cols = slice(p * LANES, (p + 1) * LANES)
        r = r_ref[:, cols].astype(F32)
        k = k_ref[:, cols].astype(F32)
        v = v_ref[:, cols].astype(F32)
        dec = dw0_ref[:, cols] + _dot(lora[:, :LANES], dw2_ref[:, cols], passes=passes)
        lw = -jnp.exp(-_softplus(-dec) - 0.5)
        a = jax.nn.sigmoid(iw0_ref[:, cols] + _dot(lora[:, LANES:], iw2_ref[:, cols], passes=passes))
        kkr = k * kk_ref[:, cols]
        kk = kkr / jnp.maximum(jnp.sqrt(headsum(kkr * kkr)), 1e-12)
        k2 = k * (1.0 + (a - 1.0) * ka_ref[:, cols])
        cum = _dot_exact_lhs(tri, lw)
        cum_scr[:, cols] = cum
        pair_vals.append(dict(r=r, k2=k2, v=v, av=-kk, bv=kk * a, cum=cum, cumex=cum - lw))

    rp = _iota2((P, P), 0)
    cp = _iota2((P, P), 1)
    strict = (rp % L) > (cp % L)
    incl = (rp % L) >= (cp % L)
    eye = jnp.where(rp == cp, 1.0, 0.0)
    blk8 = (rp // 8) == (cp // 8)
    offdiag = [((rp // (2 * b)) == (cp // (2 * b))) & ((rp // b) != (cp // b)) for b in (8, 16, 32)]

    items = [(p, c) for p in range(pairs) for c in range(chunks)]
    pre = []
    for p, c in items:
        pv = pair_vals[p]
        cols = slice(p * LANES, (p + 1) * LANES)
        sl = slice(c * L, (c + 1) * L)
        cref = cum_scr[pl.ds(c * L + L // 2 - 1, 1), cols]
        clast = cum_scr[pl.ds(c * L + L - 1, 1), cols]
        cum_c, cumex_c = pv["cum"][sl], pv["cumex"][sl]
        r_c, k_c, v_c, av_c, bv_c = pv["r"][sl], pv["k2"][sl], pv["v"][sl], pv["av"][sl], pv["bv"][sl]
        e_out = jnp.exp(cref - cum_c)
        e_end = jnp.exp(clast - cum_c)
        pre.append(dict(
            lhs1=jnp.concatenate([stack(av_c * jnp.exp(cumex_c - cref)), stack(r_c * jnp.exp(cum_c - cref))], axis=0),
            rhs1=jnp.concatenate([stack(bv_c * e_out), stack(k_c * e_out)], axis=0),
            v_st=stack(v_c), a0_st=stack(av_c * jnp.exp(cumex_c)), r0_st=stack(r_c * jnp.exp(cum_c)),
            bk=jnp.concatenate([stack(bv_c * e_end), stack(k_c * e_end)], axis=0), decay=jnp.exp(clast)))

    x1 = [_dot(f["lhs1"], f["rhs1"], nt=True, passes=passes) for f in pre]
    a_ab = [jnp.where(strict, x[:P, :P], 0.0) for x in x1]
    a_kr = [jnp.concatenate([jnp.where(strict, x[:P, P:], 0.0), jnp.where(incl, x[P:, P:], 0.0)], axis=0) for x in x1]
    a_rb = [jnp.where(incl, x[P:, :P], 0.0) for x in x1]
    m1 = [_dot(a, f["v_st"], passes=passes) for a, f in zip(a_kr, pre)]
    tinv = _unit_lower_inverses(a_ab, eye, blk8, offdiag, passes)
    m2 = [_dot(t, jnp.concatenate([f["a0_st"], m[:P]], axis=1), passes=passes) for t, f, m in zip(tinv, pre, m1)]
    fac = {}
    for it, f, m, mm, arb in zip(items, pre, m1, m2, a_rb):
        w_st, uv_st = mm[:, :LANES], mm[:, LANES:]
        fac[it] = dict(wr=jnp.concatenate([w_st, f["r0_st"]], axis=0), uv_st=uv_st,
                       yv_st=m[P:], a_rb=arb, v_t=f["v_st"].T, decay=f["decay"], bk=f["bk"])

    s = [s_scr[p] for p in range(pairs)]
    for c in range(chunks):
        fs = [fac[(p, c)] for p in range(pairs)]
        m3 = [_dot(fs[p]["wr"], s[p], nt=True, passes=passes) for p in range(pairs)]
        u_st = [m3[p][:P] + fs[p]["uv_st"] for p in range(pairs)]
        s = [s[p] * fs[p]["decay"]
             + _dot(jnp.concatenate([u_st[p].T, fs[p]["v_t"]], axis=1), fs[p]["bk"], passes=passes)
             for p in range(pairs)]
        for p in range(pairs):
            y_st = m3[p][P:] + fs[p]["yv_st"] + _dot(fs[p]["a_rb"], u_st[p], passes=passes)
            y_scr[c * L:(c + 1) * L, p * LANES:(p + 1) * LANES] = y_st[:L] + y_st[L:]
    for p in range(pairs):
        s_scr[p] = s[p]

    for p in range(pairs):
        cols = slice(p * LANES, (p + 1) * LANES)
        pv = pair_vals[p]
        y = y_scr[:, cols]
        mean = headsum(y) * (1.0 / N)
        yc = y - mean
        var = headsum(yc * yc) * (1.0 / N)
        yn = yc * lax.rsqrt(var + RWKV_GN_EPS) * gnw_ref[:, cols] + gnb_ref[:, cols]
        bonus = headsum(pv["r"] * pv["k2"] * rk_ref[:, cols]) * pv["v"]
        z_ref[:, cols] = ((yn + bonus) * _silu(g_ref[:, cols].astype(F32))).astype(z_ref.dtype)


def _rwkv_scan(rkvg, lora, p, batch, seq, *, out_dtype):
    t = rkvg.shape[0]
    w = D_MODEL
    pairs = RWKV_PAIRS_PER_STEP
    bw = pairs * LANES
    nblk = w // bw
    chunks = min(RWKV_CHUNKS_PER_STEP, seq // RWKV_CHUNK)
    rows = RWKV_CHUNK * chunks
    steps = seq // rows

    def act(col0):
        return pl.BlockSpec((rows, bw), lambda b, h, n: (b * steps + n, col0 + h))

    def vec():
        return pl.BlockSpec((1, bw), lambda b, h, n: (0, h))

    def w2():
        return pl.BlockSpec((LANES, bw), lambda b, h, n: (0, h))

    kern = functools.partial(_rwkv_scan_kernel, chunks=chunks, pairs=pairs, passes=RWKV_PASSES)
    return pl.pallas_call(
        kern,
        grid=(batch, nblk, steps),
        in_specs=[act(0), act(nblk), act(2 * nblk), act(3 * nblk),
                  pl.BlockSpec((rows, 2 * LANES), lambda b, h, n: (b * steps + n, 0)),
                  w2(), w2(), vec(), vec(), vec(), vec(), vec(), vec(), vec()],
        out_specs=pl.BlockSpec((rows, bw), lambda b, h, n: (b * steps + n, h)),
        out_shape=jax.ShapeDtypeStruct((t, w), out_dtype),
        scratch_shapes=[pltpu.VMEM((pairs, LANES, LANES), F32), pltpu.VMEM((rows, bw), F32),
                        pltpu.VMEM((rows, bw), F32)],
        compiler_params=_cparams(("arbitrary", "arbitrary", "arbitrary")),
        name="rwkv_scan",
    )(rkvg, rkvg, rkvg, rkvg, lora, p["dec_w2"], p["iclr_w2"], p["dec_w0"], p["iclr_w0"],
      p["k_k"], p["k_a"], p["r_k"], p["gn_w"], p["gn_b"])


def _gla_kernel(q_ref, k_ref, v_ref, g_ref, low_ref, w2_ref, b_ref, hg_ref, o_ref, s_scr):
    R = GLA_BLOCK
    Hf = R // 2

    @pl.when(pl.program_id(1) == 0)
    def _():
        s_scr[...] = jnp.zeros_like(s_scr)

    DK, DV = GLA_HEAD_K, GLA_HEAD_V
    heads = range(GLA_HEADS)
    ri = _iota2((R, R), 0)
    ci = _iota2((R, R), 1)
    tri = jnp.where(ci <= ri, 1.0, 0.0).astype(BF16)
    rh = _iota2((Hf, Hf), 0)
    ch = _iota2((Hf, Hf), 1)
    causal = ch <= rh

    la = -_softplus(-(_dot(low_ref[...], w2_ref[...], passes=GATE_PASSES) + b_ref[...])) * (1.0 / GLA_GATE_TAU)
    bcum = _dot_exact_lhs(tri, la)
    q = [q_ref[:, h * DK:(h + 1) * DK].astype(F32) * (DK ** -0.5) for h in heads]
    k = [k_ref[:, h * DK:(h + 1) * DK].astype(F32) for h in heads]
    v = [v_ref[:, h * DV:(h + 1) * DV].astype(F32) for h in heads]
    bc = [bcum[:, h * DK:(h + 1) * DK] for h in heads]
    ref_t = [b[Hf // 2 - 1:Hf // 2] for b in bc]
    ref_m = [b[Hf - 1:Hf] for b in bc]
    ref_b = [b[Hf + Hf // 2 - 1:Hf + Hf // 2] for b in bc]
    last = [b[R - 1:R] for b in bc]
    s00 = [jnp.where(causal, _dot(q[h][:Hf] * jnp.exp(bc[h][:Hf] - ref_t[h]),
                                  k[h][:Hf] * jnp.exp(ref_t[h] - bc[h][:Hf]), nt=True), 0.0) for h in heads]
    s11 = [jnp.where(causal, _dot(q[h][Hf:] * jnp.exp(bc[h][Hf:] - ref_b[h]),
                                  k[h][Hf:] * jnp.exp(ref_b[h] - bc[h][Hf:]), nt=True), 0.0) for h in heads]
    s10 = [_dot(q[h][Hf:] * jnp.exp(bc[h][Hf:] - ref_m[h]), k[h][:Hf] * jnp.exp(ref_m[h] - bc[h][:Hf]), nt=True)
           for h in heads]
    st = [s_scr[h] for h in heads]
    o_int = [_dot(q[h] * jnp.exp(bc[h]), st[h], nt=True) for h in heads]
    o_top = [_dot(s00[h], v[h][:Hf]) for h in heads]
    o_bot = [_dot(s10[h], v[h][:Hf]) + _dot(s11[h], v[h][Hf:]) for h in heads]
    upd = [_dot(v[h].T, k[h] * jnp.exp(last[h] - bc[h])) for h in heads]
    for h in heads:
        s_scr[h] = st[h] * jnp.exp(last[h]) + upd[h]
        o = jnp.concatenate([o_top[h], o_bot[h]], axis=0) + o_int[h]
        ms = jnp.mean(o * o, axis=-1, keepdims=True)
        on = o * lax.rsqrt(ms + NORM_EPS) * hg_ref[...]
        cols = slice(h * DV, (h + 1) * DV)
        o_ref[:, cols] = (on * _silu(g_ref[:, cols].astype(F32))).astype(o_ref.dtype)


def _gla_scan(proj, low, gate_w2, gate_b, head_g, batch, seq, *, out_dtype):
    t = proj.shape[0]
    R = GLA_BLOCK
    steps = seq // R
    kw, vw = GLA_KEY_WIDTH, GLA_VALUE_WIDTH

    def row(b, n):
        return b * steps + n

    return pl.pallas_call(
        _gla_kernel,
        grid=(batch, steps),
        in_specs=[
            pl.BlockSpec((R, kw), lambda b, n: (row(b, n), 0)),
            pl.BlockSpec((R, kw), lambda b, n: (row(b, n), 1)),
            pl.BlockSpec((R, vw), lambda b, n: (row(b, n), 2 * kw // vw)),
            pl.BlockSpec((R, vw), lambda b, n: (row(b, n), 2 * kw // vw + 1)),
            pl.BlockSpec((R, LANES), lambda b, n: (row(b, n), 0)),
            pl.BlockSpec((LANES, kw), lambda b, n: (0, 0)),
            pl.BlockSpec((1, kw), lambda b, n: (0, 0)),
            pl.BlockSpec((1, GLA_HEAD_V), lambda b, n: (0, 0)),
        ],
        out_specs=pl.BlockSpec((R, vw), lambda b, n: (row(b, n), 0)),
        out_shape=jax.ShapeDtypeStruct((t, vw), out_dtype),
        scratch_shapes=[pltpu.VMEM((GLA_HEADS, GLA_HEAD_V, GLA_HEAD_K), F32)],
        compiler_params=_cparams(("arbitrary", "arbitrary")),
        name="gla_scan",
    )(proj, proj, proj, proj, low, gate_w2, gate_b, head_g)


def _ssd_kernel(z_ref, xr_ref, br_ref, cr_ref, dt_ref, cw_ref, cbias_ref, dtb_ref, alog_ref, dsk_ref, ng_ref, o_ref,
                st_scr, raw_scr, xs_ref, bm_ref, cm_ref):
    C = SSD_CHUNK
    P = SSM_HEADDIM
    pairs_per_group = (SSM_HEADS // SSM_GROUPS) // 2
    gw = SSM_WIDTH // SSM_GROUPS

    @pl.when(pl.program_id(1) == 0)
    def _():
        st_scr[...] = jnp.zeros_like(st_scr)
        raw_scr[...] = jnp.zeros_like(raw_scr)

    raw_scr[0:SUBLANES, :] = raw_scr[C:C + SUBLANES, :]
    col = 0
    for src_ref, act_ref in ((xr_ref, xs_ref), (br_ref, bm_ref), (cr_ref, cm_ref)):
        width = src_ref.shape[1]
        cols = slice(col, col + width)
        raw_scr[SUBLANES:, cols] = src_ref[...].astype(F32)
        acc = cbias_ref[:, cols]
        for s in range(SSM_CONV):
            tap = raw_scr[SUBLANES - s:SUBLANES - s + C, cols]
            acc = acc + tap * cw_ref[SSM_CONV - 1 - s:SSM_CONV - s, cols]
        act_ref[...] = _silu(acc)
        col += width

    dt = _softplus(dt_ref[...] + dtb_ref[...])
    da = dt * (-jnp.exp(alog_ref[...]))
    ri = _iota2((C, C), 0)
    ci = _iota2((C, C), 1)
    causal = ci <= ri
    tri = jnp.where(causal, 1.0, 0.0).astype(BF16)
    acum = _dot_exact_lhs(tri, da)
    acum_t = acum.T
    alast = acum[C - 1:C, :]
    dsk = dsk_ref[...]

    lane = _iota2((1, LANES), 1)
    m0 = lane < P
    mf0 = m0.astype(F32)
    mf1 = 1.0 - mf0

    def pair_cols(x, h0):
        return jnp.where(m0, x[:, h0:h0 + 1], x[:, h0 + 1:h0 + 2])

    for g in range(SSM_GROUPS):
        bm = bm_ref[:, g * SSM_STATE:(g + 1) * SSM_STATE].astype(F32)
        cm = cm_ref[:, g * SSM_STATE:(g + 1) * SSM_STATE].astype(F32)
        cb = _dot(cm, bm, nt=True)
        bm_t = bm.T
        ys = []
        for pp in range(pairs_per_group):
            pidx = g * pairs_per_group + pp
            h0 = 2 * pidx
            cols = slice(pidx * LANES, (pidx + 1) * LANES)
            x_p = xs_ref[:, cols].astype(F32)
            dt_p = pair_cols(dt, h0)
            ac_p = pair_cols(acum, h0)
            al_p = pair_cols(alast, h0)
            xc = x_p * dt_p
            dec0 = jnp.where(causal, jnp.exp(acum[:, h0:h0 + 1] - acum_t[h0:h0 + 1, :]), 0.0)
            dec1 = jnp.where(causal, jnp.exp(acum[:, h0 + 1:h0 + 2] - acum_t[h0 + 1:h0 + 2, :]), 0.0)
            lhs = jnp.concatenate([cb * dec0, cb * dec1], axis=1)
            rhs = jnp.concatenate([xc * mf0, xc * mf1], axis=0)
            prev = st_scr[pidx]
            y = _dot(lhs, rhs) + _dot(cm, prev) * jnp.exp(ac_p) + pair_cols(dsk, h0) * x_p
            st_scr[pidx] = prev * jnp.exp(al_p) + _dot(bm_t, xc * jnp.exp(al_p - ac_p))
            ys.append(y * _silu(z_ref[:, cols].astype(F32)))
        yg = jnp.concatenate(ys, axis=1)
        ms = jnp.mean(yg * yg, axis=-1, keepdims=True)
        o_ref[:, g * gw:(g + 1) * gw] = (yg * lax.rsqrt(ms + SSM_NORM_EPS) * ng_ref[:, g * gw:(g + 1) * gw]).astype(o_ref.dtype)


def _ssd_scan(proj, dt, conv_w, conv_b, dt_bias, a_log, d_skip, norm_g, batch, seq, *, out_dtype):
    t = proj.shape[0]
    C = SSD_CHUNK
    steps = seq // C
    npairs = SSM_HEADS // 2

    def row(b, n):
        return b * steps + n

    bcol = 2 * SSM_WIDTH // SSM_BC_WIDTH
    return pl.pallas_call(
        _ssd_kernel,
        grid=(batch, steps),
        in_specs=[
            pl.BlockSpec((C, SSM_WIDTH), lambda b, n: (row(b, n), 0)),
            pl.BlockSpec((C, SSM_WIDTH), lambda b, n: (row(b, n), 1)),
            pl.BlockSpec((C, SSM_BC_WIDTH), lambda b, n: (row(b, n), bcol)),
            pl.BlockSpec((C, SSM_BC_WIDTH), lambda b, n: (row(b, n), bcol + 1)),
            pl.BlockSpec((C, LANES), lambda b, n: (row(b, n), 0)),
            pl.BlockSpec((SSM_CONV, SSM_CONV_WIDTH), lambda b, n: (0, 0)),
            pl.BlockSpec((1, SSM_CONV_WIDTH), lambda b, n: (0, 0)),
            pl.BlockSpec((1, LANES), lambda b, n: (0, 0)),
            pl.BlockSpec((1, LANES), lambda b, n: (0, 0)),
            pl.BlockSpec((1, LANES), lambda b, n: (0, 0)),
            pl.BlockSpec((1, SSM_WIDTH), lambda b, n: (0, 0)),
        ],
        out_specs=pl.BlockSpec((C, SSM_WIDTH), lambda b, n: (row(b, n), 0)),
        out_shape=jax.ShapeDtypeStruct((t, SSM_WIDTH), out_dtype),
        scratch_shapes=[pltpu.VMEM((npairs, SSM_STATE, LANES), F32), pltpu.VMEM((C + SUBLANES, SSM_CONV_WIDTH), F32),
                        pltpu.VMEM((C, SSM_WIDTH), F32), pltpu.VMEM((C, SSM_BC_WIDTH), F32),
                        pltpu.VMEM((C, SSM_BC_WIDTH), F32)],
        compiler_params=_cparams(("arbitrary", "arbitrary")),
        name="ssd_scan",
    )(proj, proj, proj, proj, dt, conv_w, conv_b, dt_bias, a_log, d_skip, norm_g)


def _pad_cols(w, n):
    return jnp.pad(w, ((0, 0), (0, n - w.shape[1])))


def _pad_rows(w, n):
    return jnp.pad(w, ((0, n - w.shape[0]), (0, 0)))


ACT_DTYPE = BF16


def kernel(x, c, ada_w, ada_b, norm_g, final_g, rwkv_mu, rwkv_w_in, rwkv_dec_w1, rwkv_dec_w2, rwkv_dec_w0, rwkv_iclr_w1, rwkv_iclr_w2, rwkv_iclr_w0, rwkv_k_k, rwkv_k_a, rwkv_r_k, rwkv_gn_w, rwkv_gn_b, rwkv_w_out, gla_w_in, gla_gate_w2, gla_gate_b, gla_head_g, gla_w_out, ssd_w_in, ssd_conv_w, ssd_conv_b, ssd_dt_bias, ssd_a_log, ssd_d, ssd_norm_g, ssd_w_out):
    batch, seq, d = x.shape
    t = batch * seq
    xf = x.reshape(t, d)

    c_pad = jnp.pad(c, ((0, SUBLANES - batch % SUBLANES if batch % SUBLANES else 0), (0, 0)))
    mod = _ada_mod(c_pad, ada_w, ada_b)[:, :batch].reshape(DEPTH, batch, 3, d)

    for i in range(DEPTH):
        kind, j = i % N_MIXERS, i // N_MIXERS
        g = norm_g[i].reshape(1, d)
        mod_l = mod[i]
        if kind == 0:
            w1 = jnp.concatenate([_pad_cols(rwkv_dec_w1[j], LANES), _pad_cols(rwkv_iclr_w1[j], LANES)], axis=1)
            xs, lora = _rwkv_prenorm(xf, g, mod_l, rwkv_mu[j], w1.astype(BF16), seq, tm=ROW_TILE)
            rkvg = _proj(xs, rwkv_w_in, j, 4 * d, seq, tm=PROJ_ROW_TILE, tn=1024, out_dtype=ACT_DTYPE)
            row = lambda v: v.reshape(1, -1)
            params = dict(dec_w2=_pad_rows(rwkv_dec_w2[j], LANES), iclr_w2=_pad_rows(rwkv_iclr_w2[j], LANES),
                          dec_w0=row(rwkv_dec_w0[j]), iclr_w0=row(rwkv_iclr_w0[j]), k_k=row(rwkv_k_k[j]),
                          k_a=row(rwkv_k_a[j]), r_k=row(rwkv_r_k[j]), gn_w=row(rwkv_gn_w[j]), gn_b=row(rwkv_gn_b[j]))
            z = _rwkv_scan(rkvg, lora, params, batch, seq, out_dtype=ACT_DTYPE)
            xf = _outproj(z, rwkv_w_out, j, xf, mod_l, seq, tm=PROJ_ROW_TILE, tn=1024)
        elif kind == 1:
            nmain = 2 * GLA_KEY_WIDTH + 2 * GLA_VALUE_WIDTH
            w_side = _pad_cols(gla_w_in[j, :, nmain:], LANES).astype(BF16)
            h, low = _prenorm_call(xf, g, mod_l, w_side, seq, tm=ROW_TILE)
            proj = _proj(h[None], gla_w_in, j, nmain, seq, tm=PROJ_ROW_TILE, tn=1024, out_dtype=ACT_DTYPE)
            z = _gla_scan(proj, low, _pad_rows(gla_gate_w2[j], LANES), gla_gate_b[j].reshape(1, -1),
                          gla_head_g[j].reshape(1, -1), batch, seq, out_dtype=ACT_DTYPE)
            xf = _outproj(z, gla_w_out, j, xf, mod_l, seq, tm=PROJ_ROW_TILE, tn=1024)
        else:
            nmain = SSM_WIDTH + SSM_CONV_WIDTH
            w_side = _pad_cols(ssd_w_in[j, :, nmain:], LANES).astype(BF16)
            h, dt = _prenorm_call(xf, g, mod_l, w_side, seq, tm=ROW_TILE)
            proj = _proj(h[None], ssd_w_in, j, nmain, seq, tm=PROJ_ROW_TILE, tn=1024, out_dtype=ACT_DTYPE)
            padl = lambda v: _pad_cols(v.reshape(1, -1), LANES)
            z = _ssd_scan(proj, dt, ssd_conv_w[j], ssd_conv_b[j].reshape(1, -1),
                          padl(ssd_dt_bias[j]), padl(ssd_a_log[j]), padl(ssd_d[j]),
                          ssd_norm_g[j].reshape(1, -1), batch, seq, out_dtype=ACT_DTYPE)
            xf = _outproj(z, ssd_w_out, j, xf, mod_l, seq, tm=PROJ_ROW_TILE, tn=512)

    out = _final_norm(xf, final_g.reshape(1, d))
    return out.reshape(batch, seq, d)
```

```python
import functools
import math

import jax
import jax.numpy as jnp
from jax import lax
from jax.experimental import pallas as pl
from jax.experimental.pallas import tpu as pltpu

F32 = jnp.float32
BF16 = jnp.bfloat16

D_MODEL = 2048
DEPTH = 4
N_MIXERS = 3
NORM_EPS = 1e-6

RWKV_HEAD = 64
RWKV_LORA = 96
RWKV_GN_EPS = 64e-5

GLA_HEADS = 4
GLA_KEY_WIDTH = D_MODEL // 2
GLA_VALUE_WIDTH = D_MODEL
GLA_HEAD_K = GLA_KEY_WIDTH // GLA_HEADS
GLA_HEAD_V = GLA_VALUE_WIDTH // GLA_HEADS
GLA_GATE_RANK = 16
GLA_GATE_TAU = 16.0

SSM_WIDTH = 2 * D_MODEL
SSM_HEADDIM = 64
SSM_HEADS = SSM_WIDTH // SSM_HEADDIM
SSM_STATE = 128
SSM_GROUPS = 8
SSM_CONV = 4
SSM_NORM_EPS = 1e-5
SSM_BC_WIDTH = SSM_GROUPS * SSM_STATE
SSM_CONV_WIDTH = SSM_WIDTH + 2 * SSM_BC_WIDTH

LANES = 128
SUBLANES = 8
VMEM_LIMIT_BYTES = 52 * 1024 * 1024

RWKV_CHUNK = 64
RWKV_CHUNKS_PER_STEP = 4
RWKV_PAIRS_PER_STEP = 4
RWKV_CHUNK_GROUPS = 1
GLA_BLOCK = 128
SSD_CHUNK = 128

RWKV_PASSES = 1
GATE_PASSES = 3

ROW_TILE = 512
PROJ_ROW_TILE = 1024


def _cparams(sem):
    return pltpu.CompilerParams(dimension_semantics=sem, vmem_limit_bytes=VMEM_LIMIT_BYTES)


def _dot(a, b, *, nt=False, passes=1):
    dims = (((1,), (1,)), ((), ())) if nt else (((1,), (0,)), ((), ()))

    def d(x, y):
        return lax.dot_general(x, y, dims, preferred_element_type=F32)

    ah = a.astype(BF16)
    bh = b.astype(BF16)
    if passes == 1:
        return d(ah, bh)
    al = (a.astype(F32) - ah.astype(F32)).astype(BF16)
    bl = (b.astype(F32) - bh.astype(F32)).astype(BF16)
    return d(ah, bh) + (d(ah, bl) + d(al, bh))


def _dot_exact_lhs(m_bf16, x):
    x1 = x.astype(BF16)
    r1 = x - x1.astype(F32)
    x2 = r1.astype(BF16)
    x3 = (r1 - x2.astype(F32)).astype(BF16)

    def d(y):
        return jnp.dot(m_bf16, y, preferred_element_type=F32)

    return d(x1) + (d(x2) + d(x3))


def _silu(x):
    hx = 0.5 * x
    return hx + hx * jnp.tanh(hx)


def _softplus(x):
    return jnp.maximum(x, 0.0) + jnp.log(1.0 + jnp.exp(-jnp.abs(x)))


def _iota2(shape, dim):
    return lax.broadcasted_iota(jnp.int32, shape, dim)


def _ada_kernel(c_ref, w_ref, b_ref, o_ref):
    c = c_ref[...]
    o_ref[...] = _dot(_silu(c), w_ref[...]) + b_ref[...]


def _ada_mod(c_pad, ada_w, ada_b, tn=1024):
    depth, d, n = ada_w.shape
    rows = c_pad.shape[0]
    return pl.pallas_call(
        _ada_kernel,
        grid=(depth, n // tn),
        in_specs=[
            pl.BlockSpec((rows, d), lambda l, j: (0, 0)),
            pl.BlockSpec((None, d, tn), lambda l, j: (l, 0, j)),
            pl.BlockSpec((None, 1, tn), lambda l, j: (l, 0, j)),
        ],
        out_specs=pl.BlockSpec((None, rows, tn), lambda l, j: (l, 0, j)),
        out_shape=jax.ShapeDtypeStruct((depth, rows, n), F32),
        compiler_params=_cparams(("arbitrary", "arbitrary")),
        name="ada_mod",
    )(c_pad, ada_w, ada_b.reshape(depth, 1, n))


def _prenorm(x, g, mod):
    ms = jnp.mean(x * x, axis=-1, keepdims=True)
    return x * lax.rsqrt(ms + NORM_EPS) * g * (1.0 + mod[1:2, :]) + mod[0:1, :]


def _prenorm_kernel(x_ref, g_ref, mod_ref, ws_ref, h_ref, os_ref):
    h = _prenorm(x_ref[...], g_ref[...], mod_ref[...]).astype(BF16)
    h_ref[...] = h
    os_ref[...] = jnp.dot(h, ws_ref[...], preferred_element_type=F32)


def _prenorm_call(x, g, mod_l, w_side, seq, *, tm):
    t, d = x.shape
    ns = w_side.shape[1]
    tm = min(tm, seq)
    tiles_per_seq = seq // tm
    return pl.pallas_call(
        _prenorm_kernel,
        grid=(t // tm,),
        in_specs=[
            pl.BlockSpec((tm, d), lambda i: (i, 0)),
            pl.BlockSpec((1, d), lambda i: (0, 0)),
            pl.BlockSpec((None, 3, d), lambda i: (i // tiles_per_seq, 0, 0)),
            pl.BlockSpec((d, ns), lambda i: (0, 0)),
        ],
        out_specs=[pl.BlockSpec((tm, d), lambda i: (i, 0)), pl.BlockSpec((tm, ns), lambda i: (i, 0))],
        out_shape=[jax.ShapeDtypeStruct((t, d), BF16), jax.ShapeDtypeStruct((t, ns), F32)],
        compiler_params=_cparams(("arbitrary",)),
        name="prenorm",
    )(x, g, mod_l, w_side)


def _proj_kernel(a_ref, w_ref, o_ref, wb_scr):
    @pl.when(pl.program_id(1) == 0)
    def _():
        wb_scr[...] = w_ref[...].astype(BF16)

    o_ref[...] = jnp.dot(a_ref[...], wb_scr[...], preferred_element_type=F32).astype(o_ref.dtype)


def _proj(a, w, layer, n, seq, *, tm, tn, out_dtype):
    groups, t, kd = a.shape
    tm = min(tm, seq)
    tiles_per_group = (n // tn) // groups
    return pl.pallas_call(
        _proj_kernel,
        grid=(n // tn, t // tm),
        in_specs=[
            pl.BlockSpec((None, tm, kd), lambda j, i: (j // tiles_per_group, i, 0)),
            pl.BlockSpec((None, kd, tn), lambda j, i: (layer, 0, j)),
        ],
        out_specs=pl.BlockSpec((tm, tn), lambda j, i: (i, j)),
        out_shape=jax.ShapeDtypeStruct((t, n), out_dtype),
        scratch_shapes=[pltpu.VMEM((kd, tn), BF16)],
        compiler_params=_cparams(("arbitrary", "arbitrary")),
        name="proj",
    )(a, w)


def _outproj_kernel(z_ref, w_ref, x_ref, mod_ref, o_ref, wb_scr):
    @pl.when(pl.program_id(1) == 0)
    def _():
        wb_scr[...] = w_ref[...].astype(BF16)

    acc = jnp.dot(z_ref[...], wb_scr[...], preferred_element_type=F32)
    o_ref[...] = x_ref[...] + mod_ref[2:3, :] * acc


def _outproj(z, w, layer, x, mod_l, seq, *, tm, tn):
    t, kd = z.shape
    n = w.shape[2]
    tm = min(tm, seq)
    tiles_per_seq = seq // tm
    return pl.pallas_call(
        _outproj_kernel,
        grid=(n // tn, t // tm),
        in_specs=[
            pl.BlockSpec((tm, kd), lambda j, i: (i, 0)),
            pl.BlockSpec((None, kd, tn), lambda j, i: (layer, 0, j)),
            pl.BlockSpec((tm, tn), lambda j, i: (i, j)),
            pl.BlockSpec((None, 3, tn), lambda j, i: (i // tiles_per_seq, 0, j)),
        ],
        out_specs=pl.BlockSpec((tm, tn), lambda j, i: (i, j)),
        out_shape=jax.ShapeDtypeStruct((t, n), F32),
        scratch_shapes=[pltpu.VMEM((kd, tn), BF16)],
        compiler_params=_cparams(("arbitrary", "arbitrary")),
        name="outproj",
    )(z, w, x, mod_l)


def _final_norm_kernel(x_ref, g_ref, o_ref):
    x = x_ref[...]
    ms = jnp.mean(x * x, axis=-1, keepdims=True)
    o_ref[...] = x * lax.rsqrt(ms + NORM_EPS) * g_ref[...]


def _final_norm(x, g, *, tm=512):
    t, d = x.shape
    tm = min(tm, t)
    return pl.pallas_call(
        _final_norm_kernel,
        grid=(t // tm,),
        in_specs=[pl.BlockSpec((tm, d), lambda i: (i, 0)), pl.BlockSpec((1, d), lambda i: (0, 0))],
        out_specs=pl.BlockSpec((tm, d), lambda i: (i, 0)),
        out_shape=jax.ShapeDtypeStruct((t, d), F32),
        compiler_params=_cparams(("arbitrary",)),
        name="final_norm",
    )(x, g)


def _rwkv_prenorm_kernel(x_ref, xh_ref, g_ref, mod_ref, mu_ref, w1_ref, xs_ref, lora_ref, *, tiles_per_seq):
    i = pl.program_id(0)
    g = g_ref[...]
    mod = mod_ref[...]
    h = _prenorm(x_ref[...], g, mod)
    hp8 = _prenorm(xh_ref[...], g, mod)
    first = (i % tiles_per_seq) == 0
    prev_row = jnp.where(first, 0.0, hp8[SUBLANES - 1:SUBLANES, :])
    rolled = pltpu.roll(h, 1, 0)
    row0 = _iota2(h.shape, 0) == 0
    dh = jnp.where(row0, prev_row, rolled) - h
    mu = mu_ref[...]
    for c in range(4):
        xs_ref[c] = (h + dh * mu[c:c + 1, :]).astype(BF16)
    xw = (h + dh * mu[4:5, :]).astype(BF16)
    xa = (h + dh * mu[5:6, :]).astype(BF16)
    w1 = w1_ref[...]
    dec_h = jnp.tanh(jnp.dot(xw, w1[:, :LANES], preferred_element_type=F32))
    icl_h = jnp.dot(xa, w1[:, LANES:], preferred_element_type=F32)
    lora_ref[...] = jnp.concatenate([dec_h, icl_h], axis=1)


def _rwkv_prenorm(x, g, mod_l, mu, w1, seq, *, tm):
    t, d = x.shape
    tm = min(tm, seq)
    tiles_per_seq = seq // tm
    halo_blocks = tm // SUBLANES
    kern = functools.partial(_rwkv_prenorm_kernel, tiles_per_seq=tiles_per_seq)
    return pl.pallas_call(
        kern,
        grid=(t // tm,),
        in_specs=[
            pl.BlockSpec((tm, d), lambda i: (i, 0)),
            pl.BlockSpec((SUBLANES, d), lambda i: (jnp.maximum(i * halo_blocks - 1, 0), 0)),
            pl.BlockSpec((1, d), lambda i: (0, 0)),
            pl.BlockSpec((None, 3, d), lambda i: (i // tiles_per_seq, 0, 0)),
            pl.BlockSpec((6, d), lambda i: (0, 0)),
            pl.BlockSpec((d, 2 * LANES), lambda i: (0, 0)),
        ],
        out_specs=[
            pl.BlockSpec((4, tm, d), lambda i: (0, i, 0)),
            pl.BlockSpec((tm, 2 * LANES), lambda i: (i, 0)),
        ],
        out_shape=[
            jax.ShapeDtypeStruct((4, t, d), BF16),
            jax.ShapeDtypeStruct((t, 2 * LANES), F32),
        ],
        compiler_params=_cparams(("arbitrary",)),
        name="rwkv_prenorm",
    )(x, x, g, mod_l, mu, w1)


def _interleave(*gens):
    live = list(gens)
    while live:
        for gen in list(live):
            try:
                next(gen)
            except StopIteration:
                live.remove(gen)


def _unit_lower_inverse_stages(a_list, eye, blk8, offdiag, passes, out):
    n = eye.shape[0]
    a8 = [jnp.where(blk8, a, 0.0) for a in a_list]
    a2 = [_dot(x, x, passes=passes) for x in a8]
    yield
    ia8 = [eye + x for x in a8]
    both = [_dot(jnp.concatenate([y, x], axis=0), y, passes=passes) for x, y in zip(ia8, a2)]
    yield
    inv = [_dot(x + b[n:], eye + b[:n], passes=passes) for x, b in zip(ia8, both)]
    yield
    for m in offdiag:
        t = [_dot(jnp.where(m, a, 0.0), i, passes=passes) for a, i in zip(a_list, inv)]
        yield
        inv = [i + _dot(i, x, passes=passes) for i, x in zip(inv, t)]
        yield
    out.extend(inv)


def _rwkv_scan_kernel(r_ref, k_ref, v_ref, g_ref, lora_ref, dw2_ref, iw2_ref, dw0_ref, iw0_ref,
                      kk_ref, ka_ref, rk_ref, gnw_ref, gnb_ref, z_ref, s_scr, cum_scr, y_scr,
                      *, chunks, pairs, groups, passes):
    L = RWKV_CHUNK
    N = RWKV_HEAD
    R = L * chunks
    P = 2 * L

    @pl.when(pl.program_id(2) == 0)
    def _():
        s_scr[...] = jnp.zeros_like(s_scr)

    lane = _iota2((1, LANES), 1)
    m0 = lane < N
    mf0 = m0.astype(F32)
    mf1 = 1.0 - mf0

    def headsum(x):
        s0 = jnp.sum(jnp.where(m0, x, 0.0), axis=-1, keepdims=True)
        s1 = jnp.sum(jnp.where(m0, 0.0, x), axis=-1, keepdims=True)
        return jnp.where(m0, s0, s1)

    def stack(x):
        return jnp.concatenate([x * mf0, x * mf1], axis=0)

    lora = lora_ref[...]
    dec = dw0_ref[...] + _dot(lora[:, :LANES], dw2_ref[...], passes=passes)
    lw_all = -jnp.exp(-_softplus(-dec) - 0.5)
    a_all = jax.nn.sigmoid(iw0_ref[...] + _dot(lora[:, LANES:], iw2_ref[...], passes=passes))
    tri = jnp.where(_iota2((L, L), 1) <= _iota2((L, L), 0), 1.0, 0.0).astype(BF16)
    cum_all = jnp.concatenate([_dot_exact_lhs(tri, lw_all[c * L:(c + 1) * L]) for c in range(chunks)], axis=0)
    cum_scr[...] = cum_all

    pair_vals = []
    for p in range(pairs):
        cols = slice(p * LANES, (p + 1) * LANES)
        r = r_ref[:, cols].astype(F32)
        k = k_ref[:, cols].astype(F32)
        v = v_ref[:, cols].astype(F32)
        a = a_all[:, cols]
        kkr = k * kk_ref[:, cols]
        kk = kkr / jnp.maximum(jnp.sqrt(headsum(kkr * kkr)), 1e-12)
        k2 = k * (1.0 + (a - 1.0) * ka_ref[:, cols])
        cum = cum_all[:, cols]
        pair_vals.append(dict(r=r, k2=k2, v=v, av=-kk, bv=kk * a, cum=cum, cumex=cum - lw_all[:, cols]))

    rp = _iota2((P, P), 0)
    cp = _iota2((P, P), 1)
    strict = (rp % L) > (cp % L)
    incl = (rp % L) >= (cp % L)
    eye = jnp.where(rp == cp, 1.0, 0.0)
    blk8 = (rp // 8) == (cp // 8)
    offdiag = [((rp // (2 * b)) == (cp // (2 * b))) & ((rp // b) != (cp // b)) for b in (8, 16, 32)]
    fac = {}

    def factor_stages(items):
        pre = []
        for p, c in items:
            pv = pair_vals[p]
            cols = slice(p * LANES, (p + 1) * LANES)
            sl = slice(c * L, (c + 1) * L)
            cref = cum_scr[pl.ds(c * L + L // 2 - 1, 1), cols]
            clast = cum_scr[pl.ds(c * L + L - 1, 1), cols]
            cum_c, cumex_c = pv["cum"][sl], pv["cumex"][sl]
            r_c, k_c, v_c, av_c, bv_c = pv["r"][sl], pv["k2"][sl], pv["v"][sl], pv["av"][sl], pv["bv"][sl]
            e_out = jnp.exp(cref - cum_c)
            e_end = jnp.exp(clast - cum_c)
            pre.append(dict(
                lhs1=jnp.concatenate([stack(av_c * jnp.exp(cumex_c - cref)), stack(r_c * jnp.exp(cum_c - cref))],
                                     axis=0),
                rhs1=jnp.concatenate([stack(bv_c * e_out), stack(k_c * e_out)], axis=0),
                v_st=stack(v_c), a0_st=stack(av_c * jnp.exp(cumex_c)), r0_st=stack(r_c * jnp.exp(cum_c)),
                bk=jnp.concatenate([stack(bv_c * e_end), stack(k_c * e_end)], axis=0), decay=jnp.exp(clast)))
        yield
        x1 = [_dot(f["lhs1"], f["rhs1"], nt=True, passes=passes) for f in pre]
        yield
        a_ab = [jnp.where(strict, x[:P, :P], 0.0) for x in x1]
        a_kr = [jnp.concatenate([jnp.where(strict, x[:P, P:], 0.0), jnp.where(incl, x[P:, P:], 0.0)], axis=0)
                for x in x1]
        a_rb = [jnp.where(incl, x[P:, :P], 0.0) for x in x1]
        m1 = [_dot(a, f["v_st"], passes=passes) for a, f in zip(a_kr, pre)]
        yield
        tinv = []
        yield from _unit_lower_inverse_stages(a_ab, eye, blk8, offdiag, passes, tinv)
        m2 = [_dot(t, jnp.concatenate([f["a0_st"], m[:P]], axis=1), passes=passes) for t, f, m in zip(tinv, pre, m1)]
        for it, f, m, mm, arb in zip(items, pre, m1, m2, a_rb):
            w_st, uv_st = mm[:, :LANES], mm[:, LANES:]
            fac[it] = dict(wr=jnp.concatenate([w_st, f["r0_st"]], axis=0), uv_st=uv_st,
                           yv_st=m[P:], a_rb=arb, v_t=f["v_st"].T, decay=f["decay"], bk=f["bk"])

    state = [s_scr[p] for p in range(pairs)]

    def state_stages(chunk_ids):
        for c in chunk_ids:
            fs = [fac[(p, c)] for p in range(pairs)]
            m3 = [_dot(fs[p]["wr"], state[p], nt=True, passes=passes) for p in range(pairs)]
            yield
            u_st = [m3[p][:P] + fs[p]["uv_st"] for p in range(pairs)]
            for p in range(pairs):
                state[p] = state[p] * fs[p]["decay"] + _dot(
                    jnp.concatenate([u_st[p].T, fs[p]["v_t"]], axis=1), fs[p]["bk"], passes=passes)
            yield
            for p in range(pairs):
                y_st = m3[p][P:] + fs[p]["yv_st"] + _dot(fs[p]["a_rb"], u_st[p], passes=passes)
                y_scr[c * L:(c + 1) * L, p * LANES:(p + 1) * LANES] = y_st[:L] + y_st[L:]
            yield

    per_group = chunks // groups
    chunk_groups = [list(range(gi * per_group, (gi + 1) * per_group)) for gi in range(groups)]
    group_items = [[(p, c) for p in range(pairs) for c in cg] for cg in chunk_groups]
    _interleave(factor_stages(group_items[0]))
    for gi in range(1, groups):
        _interleave(factor_stages(group_items[gi]), state_stages(chunk_groups[gi - 1]))
    _interleave(state_stages(chunk_groups[-1]))
    for p in range(pairs):
        s_scr[p] = state[p]

    for p in range(pairs):
        cols = slice(p * LANES, (p + 1) * LANES)
        pv = pair_vals[p]
        y = y_scr[:, cols]
        mean = headsum(y) * (1.0 / N)
        yc = y - mean
        var = headsum(yc * yc) * (1.0 / N)
        yn = yc * lax.rsqrt(var + RWKV_GN_EPS) * gnw_ref[:, cols] + gnb_ref[:, cols]
        bonus = headsum(pv["r"] * pv["k2"] * rk_ref[:, cols]) * pv["v"]
        z_ref[:, cols] = ((yn + bonus) * _silu(g_ref[:, cols].astype(F32))).astype(z_ref.dtype)


def _rwkv_scan(rkvg, lora, p, batch, seq, *, out_dtype):
    t = rkvg.shape[0]
    w = D_MODEL
    pairs = RWKV_PAIRS_PER_STEP
    bw = pairs * LANES
    nblk = w // bw
    chunks = min(RWKV_CHUNKS_PER_STEP, seq // RWKV_CHUNK)
    rows = RWKV_CHUNK * chunks
    steps = seq // rows

    def act(col0):
        return pl.BlockSpec((rows, bw), lambda b, h, n: (b * steps + n, col0 + h))

    def vec():
        return pl.BlockSpec((1, bw), lambda b, h, n: (0, h))

    def w2():
        return pl.BlockSpec((LANES, bw), lambda b, h, n: (0, h))

    kern = functools.partial(_rwkv_scan_kernel, chunks=chunks, pairs=pairs, groups=min(RWKV_CHUNK_GROUPS, chunks),
                             passes=RWKV_PASSES)
    return pl.pallas_call(
        kern,
        grid=(batch, nblk, steps),
        in_specs=[act(0), act(nblk), act(2 * nblk), act(3 * nblk),
                  pl.BlockSpec((rows, 2 * LANES), lambda b, h, n: (b * steps + n, 0)),
                  w2(), w2(), vec(), vec(), vec(), vec(), vec(), vec(), vec()],
        out_specs=pl.BlockSpec((rows, bw), lambda b, h, n: (b * steps + n, h)),
        out_shape=jax.ShapeDtypeStruct((t, w), out_dtype),
        scratch_shapes=[pltpu.VMEM((pairs, LANES, LANES), F32), pltpu.VMEM((rows, bw), F32),
                        pltpu.VMEM((rows, bw), F32)],
        compiler_params=_cparams(("arbitrary", "arbitrary", "arbitrary")),
        name="rwkv_scan",
    )(rkvg, rkvg, rkvg, rkvg, lora, p["dec_w2"], p["iclr_w2"], p["dec_w0"], p["iclr_w0"],
      p["k_k"], p["k_a"], p["r_k"], p["gn_w"], p["gn_b"])


def _gla_kernel(q_ref, k_ref, v_ref, g_ref, low_ref, w2_ref, b_ref, hg_ref, o_ref, s_scr):
    R = GLA_BLOCK
    Hf = R // 2

    @pl.when(pl.program_id(1) == 0)
    def _():
        s_scr[...] = jnp.zeros_like(s_scr)

    DK, DV = GLA_HEAD_K, GLA_HEAD_V
    heads = range(GLA_HEADS)
    ri = _iota2((R, R), 0)
    ci = _iota2((R, R), 1)
    tri = jnp.where(ci <= ri, 1.0, 0.0).astype(BF16)
    rh = _iota2((Hf, Hf), 0)
    ch = _iota2((Hf, Hf), 1)
    causal = ch <= rh

    la = -_softplus(-(_dot(low_ref[...], w2_ref[...], passes=GATE_PASSES) + b_ref[...])) * (1.0 / GLA_GATE_TAU)
    bcum = _dot_exact_lhs(tri, la)
    q = [q_ref[:, h * DK:(h + 1) * DK].astype(F32) * (DK ** -0.5) for h in heads]
    k = [k_ref[:, h * DK:(h + 1) * DK].astype(F32) for h in heads]
    v = [v_ref[:, h * DV:(h + 1) * DV].astype(F32) for h in heads]
    bc = [bcum[:, h * DK:(h + 1) * DK] for h in heads]
    ref_t = [b[Hf // 2 - 1:Hf // 2] for b in bc]
    ref_m = [b[Hf - 1:Hf] for b in bc]
    ref_b = [b[Hf + Hf // 2 - 1:Hf + Hf // 2] for b in bc]
    last = [b[R - 1:R] for b in bc]
    s00 = [jnp.where(causal, _dot(q[h][:Hf] * jnp.exp(bc[h][:Hf] - ref_t[h]),
                                  k[h][:Hf] * jnp.exp(ref_t[h] - bc[h][:Hf]), nt=True), 0.0) for h in heads]
    s11 = [jnp.where(causal, _dot(q[h][Hf:] * jnp.exp(bc[h][Hf:] - ref_b[h]),
                                  k[h][Hf:] * jnp.exp(ref_b[h] - bc[h][Hf:]), nt=True), 0.0) for h in heads]
    s10 = [_dot(q[h][Hf:] * jnp.exp(bc[h][Hf:] - ref_m[h]), k[h][:Hf] * jnp.exp(ref_m[h] - bc[h][:Hf]), nt=True)
           for h in heads]
    st = [s_scr[h] for h in heads]
    o_int = [_dot(q[h] * jnp.exp(bc[h]), st[h], nt=True) for h in heads]
    o_top = [_dot(s00[h], v[h][:Hf]) for h in heads]
    o_bot = [_dot(s10[h], v[h][:Hf]) + _dot(s11[h], v[h][Hf:]) for h in heads]
    upd = [_dot(v[h].T, k[h] * jnp.exp(last[h] - bc[h])) for h in heads]
    for h in heads:
        s_scr[h] = st[h] * jnp.exp(last[h]) + upd[h]
        o = jnp.concatenate([o_top[h], o_bot[h]], axis=0) + o_int[h]
        ms = jnp.mean(o * o, axis=-1, keepdims=True)
        on = o * lax.rsqrt(ms + NORM_EPS) * hg_ref[...]
        cols = slice(h * DV, (h + 1) * DV)
        o_ref[:, cols] = (on * _silu(g_ref[:, cols].astype(F32))).astype(o_ref.dtype)


def _gla_scan(proj, low, gate_w2, gate_b, head_g, batch, seq, *, out_dtype):
    t = proj.shape[0]
    R = GLA_BLOCK
    steps = seq // R
    kw, vw = GLA_KEY_WIDTH, GLA_VALUE_WIDTH

    def row(b, n):
        return b * steps + n

    return pl.pallas_call(
        _gla_kernel,
        grid=(batch, steps),
        in_specs=[
            pl.BlockSpec((R, kw), lambda b, n: (row(b, n), 0)),
            pl.BlockSpec((R, kw), lambda b, n: (row(b, n), 1)),
            pl.BlockSpec((R, vw), lambda b, n: (row(b, n), 2 * kw // vw)),
            pl.BlockSpec((R, vw), lambda b, n: (row(b, n), 2 * kw // vw + 1)),
            pl.BlockSpec((R, LANES), lambda b, n: (row(b, n), 0)),
            pl.BlockSpec((LANES, kw), lambda b, n: (0, 0)),
            pl.BlockSpec((1, kw), lambda b, n: (0, 0)),
            pl.BlockSpec((1, GLA_HEAD_V), lambda b, n: (0, 0)),
        ],
        out_specs=pl.BlockSpec((R, vw), lambda b, n: (row(b, n), 0)),
        out_shape=jax.ShapeDtypeStruct((t, vw), out_dtype),
        scratch_shapes=[pltpu.VMEM((GLA_HEADS, GLA_HEAD_V, GLA_HEAD_K), F32)],
        compiler_params=_cparams(("arbitrary", "arbitrary")),
        name="gla_scan",
    )(proj, proj, proj, proj, low, gate_w2, gate_b, head_g)


def _ssd_kernel(z_ref, xr_ref, br_ref, cr_ref, dt_ref, cw_ref, cbias_ref, dtb_ref, alog_ref, dsk_ref, ng_ref, o_ref,
                st_scr, raw_scr, xs_ref, bm_ref, cm_ref):
    C = SSD_CHUNK
    P = SSM_HEADDIM
    pairs_per_group = (SSM_HEADS // SSM_GROUPS) // 2
    gw = SSM_WIDTH // SSM_GROUPS

    @pl.when(pl.program_id(1) == 0)
    def _():
        st_scr[...] = jnp.zeros_like(st_scr)
        raw_scr[...] = jnp.zeros_like(raw_scr)

    raw_scr[0:SUBLANES, :] = raw_scr[C:C + SUBLANES, :]
    col = 0
    for src_ref, act_ref in ((xr_ref, xs_ref), (br_ref, bm_ref), (cr_ref, cm_ref)):
        width = src_ref.shape[1]
        cols = slice(col, col + width)
        raw_scr[SUBLANES:, cols] = src_ref[...].astype(F32)
        acc = cbias_ref[:, cols]
        for s in range(SSM_CONV):
            tap = raw_scr[SUBLANES - s:SUBLANES - s + C, cols]
            acc = acc + tap * cw_ref[SSM_CONV - 1 - s:SSM_CONV - s, cols]
        act_ref[...] = _silu(acc)
        col += width

    dt = _softplus(dt_ref[...] + dtb_ref[...])
    da = dt * (-jnp.exp(alog_ref[...]))
    ri = _iota2((C, C), 0)
    ci = _iota2((C, C), 1)
    causal = ci <= ri
    tri = jnp.where(causal, 1.0, 0.0).astype(BF16)
    acum = _dot_exact_lhs(tri, da)
    acum_t = acum.T
    alast = acum[C - 1:C, :]
    dsk = dsk_ref[...]

    lane = _iota2((1, LANES), 1)
    m0 = lane < P
    mf0 = m0.astype(F32)
    mf1 = 1.0 - mf0

    def pair_cols(x, h0):
        return jnp.where(m0, x[:, h0:h0 + 1], x[:, h0 + 1:h0 + 2])

    for g in range(SSM_GROUPS):
        bm = bm_ref[:, g * SSM_STATE:(g + 1) * SSM_STATE].astype(F32)
        cm = cm_ref[:, g * SSM_STATE:(g + 1) * SSM_STATE].astype(F32)
        cb = _dot(cm, bm, nt=True)
        bm_t = bm.T
        ys = []
        for pp in range(pairs_per_group):
            pidx = g * pairs_per_group + pp
            h0 = 2 * pidx
            cols = slice(pidx * LANES, (pidx + 1) * LANES)
            x_p = xs_ref[:, cols].astype(F32)
            dt_p = pair_cols(dt, h0)
            ac_p = pair_cols(acum, h0)
            al_p = pair_cols(alast, h0)
            xc = x_p * dt_p
            dec0 = jnp.where(causal, jnp.exp(acum[:, h0:h0 + 1] - acum_t[h0:h0 + 1, :]), 0.0)
            dec1 = jnp.where(causal, jnp.exp(acum[:, h0 + 1:h0 + 2] - acum_t[h0 + 1:h0 + 2, :]), 0.0)
            lhs = jnp.concatenate([cb * dec0, cb * dec1], axis=1)
            rhs = jnp.concatenate([xc * mf0, xc * mf1], axis=0)
            prev = st_scr[pidx]
            y = _dot(lhs, rhs) + _dot(cm, prev) * jnp.exp(ac_p) + pair_cols(dsk, h0) * x_p
            st_scr[pidx] = prev * jnp.exp(al_p) + _dot(bm_t, xc * jnp.exp(al_p - ac_p))
            ys.append(y * _silu(z_ref[:, cols].astype(F32)))
        yg = jnp.concatenate(ys, axis=1)
        ms = jnp.mean(yg * yg, axis=-1, keepdims=True)
        o_ref[:, g * gw:(g + 1) * gw] = (yg * lax.rsqrt(ms + SSM_NORM_EPS) * ng_ref[:, g * gw:(g + 1) * gw]).astype(o_ref.dtype)


def _ssd_scan(proj, dt, conv_w, conv_b, dt_bias, a_log, d_skip, norm_g, batch, seq, *, out_dtype):
    t = proj.shape[0]
    C = SSD_CHUNK
    steps = seq // C
    npairs = SSM_HEADS // 2

    def row(b, n):
        return b * steps + n

    bcol = 2 * SSM_WIDTH // SSM_BC_WIDTH
    return pl.pallas_call(
        _ssd_kernel,
        grid=(batch, steps),
        in_specs=[
            pl.BlockSpec((C, SSM_WIDTH), lambda b, n: (row(b, n), 0)),
            pl.BlockSpec((C, SSM_WIDTH), lambda b, n: (row(b, n), 1)),
            pl.BlockSpec((C, SSM_BC_WIDTH), lambda b, n: (row(b, n), bcol)),
            pl.BlockSpec((C, SSM_BC_WIDTH), lambda b, n: (row(b, n), bcol + 1)),
            pl.BlockSpec((C, LANES), lambda b, n: (row(b, n), 0)),
            pl.BlockSpec((SSM_CONV, SSM_CONV_WIDTH), lambda b, n: (0, 0)),
            pl.BlockSpec((1, SSM_CONV_WIDTH), lambda b, n: (0, 0)),
            pl.BlockSpec((1, LANES), lambda b, n: (0, 0)),
            pl.BlockSpec((1, LANES), lambda b, n: (0, 0)),
            pl.BlockSpec((1, LANES), lambda b, n: (0, 0)),
            pl.BlockSpec((1, SSM_WIDTH), lambda b, n: (0, 0)),
        ],
        out_specs=pl.BlockSpec((C, SSM_WIDTH), lambda b, n: (row(b, n), 0)),
        out_shape=jax.ShapeDtypeStruct((t, SSM_WIDTH), out_dtype),
        scratch_shapes=[pltpu.VMEM((npairs, SSM_STATE, LANES), F32), pltpu.VMEM((C + SUBLANES, SSM_CONV_WIDTH), F32),
                        pltpu.VMEM((C, SSM_WIDTH), F32), pltpu.VMEM((C, SSM_BC_WIDTH), F32),
                        pltpu.VMEM((C, SSM_BC_WIDTH), F32)],
        compiler_params=_cparams(("arbitrary", "arbitrary")),
        name="ssd_scan",
    )(proj, proj, proj, proj, dt, conv_w, conv_b, dt_bias, a_log, d_skip, norm_g)


def _pad_cols(w, n):
    return jnp.pad(w, ((0, 0), (0, n - w.shape[1])))


def _pad_rows(w, n):
    return jnp.pad(w, ((0, n - w.shape[0]), (0, 0)))


ACT_DTYPE = BF16


def kernel(x, c, ada_w, ada_b, norm_g, final_g, rwkv_mu, rwkv_w_in, rwkv_dec_w1, rwkv_dec_w2, rwkv_dec_w0, rwkv_iclr_w1, rwkv_iclr_w2, rwkv_iclr_w0, rwkv_k_k, rwkv_k_a, rwkv_r_k, rwkv_gn_w, rwkv_gn_b, rwkv_w_out, gla_w_in, gla_gate_w2, gla_gate_b, gla_head_g, gla_w_out, ssd_w_in, ssd_conv_w, ssd_conv_b, ssd_dt_bias, ssd_a_log, ssd_d, ssd_norm_g, ssd_w_out):
    batch, seq, d = x.shape
    t = batch * seq
    xf = x.reshape(t, d)

    c_pad = jnp.pad(c, ((0, SUBLANES - batch % SUBLANES if batch % SUBLANES else 0), (0, 0)))
    mod = _ada_mod(c_pad, ada_w, ada_b)[:, :batch].reshape(DEPTH, batch, 3, d)

    for i in range(DEPTH):
        kind, j = i % N_MIXERS, i // N_MIXERS
        g = norm_g[i].reshape(1, d)
        mod_l = mod[i]
        if kind == 0:
            w1 = jnp.concatenate([_pad_cols(rwkv_dec_w1[j], LANES), _pad_cols(rwkv_iclr_w1[j], LANES)], axis=1)
            xs, lora = _rwkv_prenorm(xf, g, mod_l, rwkv_mu[j], w1.astype(BF16), seq, tm=ROW_TILE)
            rkvg = _proj(xs, rwkv_w_in, j, 4 * d, seq, tm=PROJ_ROW_TILE, tn=1024, out_dtype=ACT_DTYPE)
            row = lambda v: v.reshape(1, -1)
            params = dict(dec_w2=_pad_rows(rwkv_dec_w2[j], LANES), iclr_w2=_pad_rows(rwkv_iclr_w2[j], LANES),
                          dec_w0=row(rwkv_dec_w0[j]), iclr_w0=row(rwkv_iclr_w0[j]), k_k=row(rwkv_k_k[j]),
                          k_a=row(rwkv_k_a[j]), r_k=row(rwkv_r_k[j]), gn_w=row(rwkv_gn_w[j]), gn_b=row(rwkv_gn_b[j]))
            z = _rwkv_scan(rkvg, lora, params, batch, seq, out_dtype=ACT_DTYPE)
            xf = _outproj(z, rwkv_w_out, j, xf, mod_l, seq, tm=PROJ_ROW_TILE, tn=1024)
        elif kind == 1:
            nmain = 2 * GLA_KEY_WIDTH + 2 * GLA_VALUE_WIDTH
            w_bf = gla_w_in[j:j + 1].astype(BF16)
            h, low = _prenorm_call(xf, g, mod_l, _pad_cols(w_bf[0, :, nmain:], LANES), seq, tm=ROW_TILE)
            proj = _proj(h[None], w_bf, 0, nmain, seq, tm=PROJ_ROW_TILE, tn=1024, out_dtype=ACT_DTYPE)
            z = _gla_scan(proj, low, _pad_rows(gla_gate_w2[j], LANES), gla_gate_b[j].reshape(1, -1),
                          gla_head_g[j].reshape(1, -1), batch, seq, out_dtype=ACT_DTYPE)
            xf = _outproj(z, gla_w_out, j, xf, mod_l, seq, tm=PROJ_ROW_TILE, tn=1024)
        else:
            nmain = SSM_WIDTH + SSM_CONV_WIDTH
            w_bf = ssd_w_in[j:j + 1].astype(BF16)
            h, dt = _prenorm_call(xf, g, mod_l, _pad_cols(w_bf[0, :, nmain:], LANES), seq, tm=ROW_TILE)
            proj = _proj(h[None], w_bf, 0, nmain, seq, tm=PROJ_ROW_TILE, tn=1024, out_dtype=ACT_DTYPE)
            padl = lambda v: _pad_cols(v.reshape(1, -1), LANES)
            z = _ssd_scan(proj, dt, ssd_conv_w[j], ssd_conv_b[j].reshape(1, -1),
                          padl(ssd_dt_bias[j]), padl(ssd_a_log[j]), padl(ssd_d[j]),
                          ssd_norm_g[j].reshape(1, -1), batch, seq, out_dtype=ACT_DTYPE)
            xf = _outproj(z, ssd_w_out, j, xf, mod_l, seq, tm=PROJ_ROW_TILE, tn=512)

    out = _final_norm(xf, final_g.reshape(1, d))
    return out.reshape(batch, seq, d)
```

```python
import functools
import math

import jax
import jax.numpy as jnp
from jax import lax
from jax.experimental import pallas as pl
from jax.experimental.pallas import tpu as pltpu

F32 = jnp.float32
BF16 = jnp.bfloat16

D_MODEL = 2048
DEPTH = 4
N_MIXERS = 3
NORM_EPS = 1e-6

RWKV_HEAD = 64
RWKV_LORA = 96
RWKV_GN_EPS = 64e-5

GLA_HEADS = 4
GLA_KEY_WIDTH = D_MODEL // 2
GLA_VALUE_WIDTH = D_MODEL
GLA_HEAD_K = GLA_KEY_WIDTH // GLA_HEADS
GLA_HEAD_V = GLA_VALUE_WIDTH // GLA_HEADS
GLA_GATE_RANK = 16
GLA_GATE_TAU = 16.0

SSM_WIDTH = 2 * D_MODEL
SSM_HEADDIM = 64
SSM_HEADS = SSM_WIDTH // SSM_HEADDIM
SSM_STATE = 128
SSM_GROUPS = 8
SSM_CONV = 4
SSM_NORM_EPS = 1e-5
SSM_BC_WIDTH = SSM_GROUPS * SSM_STATE
SSM_CONV_WIDTH = SSM_WIDTH + 2 * SSM_BC_WIDTH

LANES = 128
SUBLANES = 8
VMEM_LIMIT_BYTES = 52 * 1024 * 1024

RWKV_CHUNK = 64
RWKV_CHUNKS_PER_STEP = 2
RWKV_PAIRS_PER_STEP = 16
RWKV_CHUNK_GROUPS = 1
GLA_BLOCK = 128
SSD_CHUNK = 128

RWKV_PASSES = 1
GATE_PASSES = 3

ROW_TILE = 512
PROJ_ROW_TILE = 1024


def _cparams(sem):
    return pltpu.CompilerParams(dimension_semantics=sem, vmem_limit_bytes=VMEM_LIMIT_BYTES)


def _dot(a, b, *, nt=False, passes=1):
    dims = (((1,), (1,)), ((), ())) if nt else (((1,), (0,)), ((), ()))

    def d(x, y):
        return lax.dot_general(x, y, dims, preferred_element_type=F32)

    ah = a.astype(BF16)
    bh = b.astype(BF16)
    if passes == 1:
        return d(ah, bh)
    al = (a.astype(F32) - ah.astype(F32)).astype(BF16)
    bl = (b.astype(F32) - bh.astype(F32)).astype(BF16)
    return d(ah, bh) + (d(ah, bl) + d(al, bh))


def _dot_exact_lhs(m_bf16, x):
    x1 = x.astype(BF16)
    r1 = x - x1.astype(F32)
    x2 = r1.astype(BF16)
    x3 = (r1 - x2.astype(F32)).astype(BF16)

    def d(y):
        return jnp.dot(m_bf16, y, preferred_element_type=F32)

    return d(x1) + (d(x2) + d(x3))


def _silu(x):
    hx = 0.5 * x
    return hx + hx * jnp.tanh(hx)


def _softplus(x):
    return jnp.maximum(x, 0.0) + jnp.log(1.0 + jnp.exp(-jnp.abs(x)))


def _iota2(shape, dim):
    return lax.broadcasted_iota(jnp.int32, shape, dim)


def _ada_kernel(c_ref, w_ref, b_ref, o_ref):
    c = c_ref[...]
    o_ref[...] = _dot(_silu(c), w_ref[...]) + b_ref[...]


def _ada_mod(c_pad, ada_w, ada_b, tn=1024):
    depth, d, n = ada_w.shape
    rows = c_pad.shape[0]
    return pl.pallas_call(
        _ada_kernel,
        grid=(depth, n // tn),
        in_specs=[
            pl.BlockSpec((rows, d), lambda l, j: (0, 0)),
            pl.BlockSpec((None, d, tn), lambda l, j: (l, 0, j)),
            pl.BlockSpec((None, 1, tn), lambda l, j: (l, 0, j)),
        ],
        out_specs=pl.BlockSpec((None, rows, tn), lambda l, j: (l, 0, j)),
        out_shape=jax.ShapeDtypeStruct((depth, rows, n), F32),
        compiler_params=_cparams(("arbitrary", "arbitrary")),
        name="ada_mod",
    )(c_pad, ada_w, ada_b.reshape(depth, 1, n))


def _prenorm(x, g, mod):
    ms = jnp.mean(x * x, axis=-1, keepdims=True)
    return x * lax.rsqrt(ms + NORM_EPS) * g * (1.0 + mod[1:2, :]) + mod[0:1, :]


def _prenorm_kernel(x_ref, g_ref, mod_ref, ws_ref, h_ref, os_ref):
    h = _prenorm(x_ref[...], g_ref[...], mod_ref[...]).astype(BF16)
    h_ref[...] = h
    os_ref[...] = jnp.dot(h, ws_ref[...], preferred_element_type=F32)


def _prenorm_call(x, g, mod_l, w_side, seq, *, tm):
    t, d = x.shape
    ns = w_side.shape[1]
    tm = min(tm, seq)
    tiles_per_seq = seq // tm
    return pl.pallas_call(
        _prenorm_kernel,
        grid=(t // tm,),
        in_specs=[
            pl.BlockSpec((tm, d), lambda i: (i, 0)),
            pl.BlockSpec((1, d), lambda i: (0, 0)),
            pl.BlockSpec((None, 3, d), lambda i: (i // tiles_per_seq, 0, 0)),
            pl.BlockSpec((d, ns), lambda i: (0, 0)),
        ],
        out_specs=[pl.BlockSpec((tm, d), lambda i: (i, 0)), pl.BlockSpec((tm, ns), lambda i: (i, 0))],
        out_shape=[jax.ShapeDtypeStruct((t, d), BF16), jax.ShapeDtypeStruct((t, ns), F32)],
        compiler_params=_cparams(("arbitrary",)),
        name="prenorm",
    )(x, g, mod_l, w_side)


def _proj_kernel(a_ref, w_ref, o_ref, wb_scr):
    @pl.when(pl.program_id(1) == 0)
    def _():
        wb_scr[...] = w_ref[...].astype(BF16)

    o_ref[...] = jnp.dot(a_ref[...], wb_scr[...], preferred_element_type=F32).astype(o_ref.dtype)


def _proj(a, w, layer, n, seq, *, tm, tn, out_dtype):
    groups, t, kd = a.shape
    tm = min(tm, seq)
    tiles_per_group = (n // tn) // groups
    return pl.pallas_call(
        _proj_kernel,
        grid=(n // tn, t // tm),
        in_specs=[
            pl.BlockSpec((None, tm, kd), lambda j, i: (j // tiles_per_group, i, 0)),
            pl.BlockSpec((None, kd, tn), lambda j, i: (layer, 0, j)),
        ],
        out_specs=pl.BlockSpec((tm, tn), lambda j, i: (i, j)),
        out_shape=jax.ShapeDtypeStruct((t, n), out_dtype),
        scratch_shapes=[pltpu.VMEM((kd, tn), BF16)],
        compiler_params=_cparams(("arbitrary", "arbitrary")),
        name="proj",
    )(a, w)


def _outproj_kernel(z_ref, w_ref, x_ref, mod_ref, o_ref, wb_scr):
    @pl.when(pl.program_id(1) == 0)
    def _():
        wb_scr[...] = w_ref[...].astype(BF16)

    acc = jnp.dot(z_ref[...], wb_scr[...], preferred_element_type=F32)
    o_ref[...] = x_ref[...] + mod_ref[2:3, :] * acc


def _outproj(z, w, layer, x, mod_l, seq, *, tm, tn):
    t, kd = z.shape
    n = w.shape[2]
    tm = min(tm, seq)
    tiles_per_seq = seq // tm
    return pl.pallas_call(
        _outproj_kernel,
        grid=(n // tn, t // tm),
        in_specs=[
            pl.BlockSpec((tm, kd), lambda j, i: (i, 0)),
            pl.BlockSpec((None, kd, tn), lambda j, i: (layer, 0, j)),
            pl.BlockSpec((tm, tn), lambda j, i: (i, j)),
            pl.BlockSpec((None, 3, tn), lambda j, i: (i // tiles_per_seq, 0, j)),
        ],
        out_specs=pl.BlockSpec((tm, tn), lambda j, i: (i, j)),
        out_shape=jax.ShapeDtypeStruct((t, n), F32),
        scratch_shapes=[pltpu.VMEM((kd, tn), BF16)],
        compiler_params=_cparams(("arbitrary", "arbitrary")),
        name="outproj",
    )(z, w, x, mod_l)


def _final_norm_kernel(x_ref, g_ref, o_ref):
    x = x_ref[...]
    ms = jnp.mean(x * x, axis=-1, keepdims=True)
    o_ref[...] = x * lax.rsqrt(ms + NORM_EPS) * g_ref[...]


def _final_norm(x, g, *, tm=512):
    t, d = x.shape
    tm = min(tm, t)
    return pl.pallas_call(
        _final_norm_kernel,
        grid=(t // tm,),
        in_specs=[pl.BlockSpec((tm, d), lambda i: (i, 0)), pl.BlockSpec((1, d), lambda i: (0, 0))],
        out_specs=pl.BlockSpec((tm, d), lambda i: (i, 0)),
        out_shape=jax.ShapeDtypeStruct((t, d), F32),
        compiler_params=_cparams(("arbitrary",)),
        name="final_norm",
    )(x, g)


def _rwkv_prenorm_kernel(x_ref, xh_ref, g_ref, mod_ref, mu_ref, w1_ref, xs_ref, lora_ref, *, tiles_per_seq):
    i = pl.program_id(0)
    g = g_ref[...]
    mod = mod_ref[...]
    h = _prenorm(x_ref[...], g, mod)
    hp8 = _prenorm(xh_ref[...], g, mod)
    first = (i % tiles_per_seq) == 0
    prev_row = jnp.where(first, 0.0, hp8[SUBLANES - 1:SUBLANES, :])
    rolled = pltpu.roll(h, 1, 0)
    row0 = _iota2(h.shape, 0) == 0
    dh = jnp.where(row0, prev_row, rolled) - h
    mu = mu_ref[...]
    for c in range(4):
        xs_ref[c] = (h + dh * mu[c:c + 1, :]).astype(BF16)
    xw = (h + dh * mu[4:5, :]).astype(BF16)
    xa = (h + dh * mu[5:6, :]).astype(BF16)
    w1 = w1_ref[...]
    dec_h = jnp.tanh(jnp.dot(xw, w1[:, :LANES], preferred_element_type=F32))
    icl_h = jnp.dot(xa, w1[:, LANES:], preferred_element_type=F32)
    lora_ref[...] = jnp.concatenate([dec_h, icl_h], axis=1)


def _rwkv_prenorm(x, g, mod_l, mu, w1, seq, *, tm):
    t, d = x.shape
    tm = min(tm, seq)
    tiles_per_seq = seq // tm
    halo_blocks = tm // SUBLANES
    kern = functools.partial(_rwkv_prenorm_kernel, tiles_per_seq=tiles_per_seq)
    return pl.pallas_call(
        kern,
        grid=(t // tm,),
        in_specs=[
            pl.BlockSpec((tm, d), lambda i: (i, 0)),
            pl.BlockSpec((SUBLANES, d), lambda i: (jnp.maximum(i * halo_blocks - 1, 0), 0)),
            pl.BlockSpec((1, d), lambda i: (0, 0)),
            pl.BlockSpec((None, 3, d), lambda i: (i // tiles_per_seq, 0, 0)),
            pl.BlockSpec((6, d), lambda i: (0, 0)),
            pl.BlockSpec((d, 2 * LANES), lambda i: (0, 0)),
        ],
        out_specs=[
            pl.BlockSpec((4, tm, d), lambda i: (0, i, 0)),
            pl.BlockSpec((tm, 2 * LANES), lambda i: (i, 0)),
        ],
        out_shape=[
            jax.ShapeDtypeStruct((4, t, d), BF16),
            jax.ShapeDtypeStruct((t, 2 * LANES), F32),
        ],
        compiler_params=_cparams(("arbitrary",)),
        name="rwkv_prenorm",
    )(x, x, g, mod_l, mu, w1)


def _interleave(*gens):
    live = list(gens)
    while live:
        for gen in list(live):
            try:
                next(gen)
            except StopIteration:
                live.remove(gen)


def _unit_lower_inverse_stages(a_list, eye, blk8, offdiag, passes, out):
    n = eye.shape[0]
    a8 = [jnp.where(blk8, a, 0.0) for a in a_list]
    a2 = [_dot(x, x, passes=passes) for x in a8]
    yield
    ia8 = [eye + x for x in a8]
    both = [_dot(jnp.concatenate([y, x], axis=0), y, passes=passes) for x, y in zip(ia8, a2)]
    yield
    inv = [_dot(x + b[n:], eye + b[:n], passes=passes) for x, b in zip(ia8, both)]
    yield
    for m in offdiag:
        t = [_dot(jnp.where(m, a, 0.0), i, passes=passes) for a, i in zip(a_list, inv)]
        yield
        inv = [i + _dot(i, x, passes=passes) for i, x in zip(inv, t)]
        yield
    out.extend(inv)


def _rwkv_scan_kernel(r_ref, k_ref, v_ref, g_ref, lora_ref, dw2_ref, iw2_ref, dw0_ref, iw0_ref,
                      kk_ref, ka_ref, rk_ref, gnw_ref, gnb_ref, z_ref, s_scr, cum_scr, y_scr,
                      *, chunks, pairs, groups, passes):
    L = RWKV_CHUNK
    N = RWKV_HEAD
    R = L * chunks
    P = 2 * L

    @pl.when(pl.program_id(2) == 0)
    def _():
        s_scr[...] = jnp.zeros_like(s_scr)

    lane = _iota2((1, LANES), 1)
    m0 = lane < N
    mf0 = m0.astype(F32)
    mf1 = 1.0 - mf0

    def headsum(x):
        s0 = jnp.sum(jnp.where(m0, x, 0.0), axis=-1, keepdims=True)
        s1 = jnp.sum(jnp.where(m0, 0.0, x), axis=-1, keepdims=True)
        return jnp.where(m0, s0, s1)

    def stack(x):
        return jnp.concatenate([x * mf0, x * mf1], axis=0)

    lora = lora_ref[...]
    dec = dw0_ref[...] + _dot(lora[:, :LANES], dw2_ref[...], passes=passes)
    lw_all = -jnp.exp(-_softplus(-dec) - 0.5)
    a_all = jax.nn.sigmoid(iw0_ref[...] + _dot(lora[:, LANES:], iw2_ref[...], passes=passes))
    tri = jnp.where(_iota2((L, L), 1) <= _iota2((L, L), 0), 1.0, 0.0).astype(BF16)
    cum_all = jnp.concatenate([_dot_exact_lhs(tri, lw_all[c * L:(c + 1) * L]) for c in range(chunks)], axis=0)
    cum_scr[...] = cum_all

    pair_vals = []
    for p in range(pairs):
        cols = slice(p * LANES, (p + 1) * LANES)
        r = r_ref[:, cols].astype(F32)
        k = k_ref[:, cols].astype(F32)
        v = v_ref[:, cols].astype(F32)
        a = a_all[:, cols]
        kkr = k * kk_ref[:, cols]
        kk = kkr / jnp.maximum(jnp.sqrt(headsum(kkr * kkr)), 1e-12)
        k2 = k * (1.0 + (a - 1.0) * ka_ref[:, cols])
        cum = cum_all[:, cols]
        pair_vals.append(dict(r=r, k2=k2, v=v, av=-kk, bv=kk * a, cum=cum, cumex=cum - lw_all[:, cols]))

    rp = _iota2((P, P), 0)
    cp = _iota2((P, P), 1)
    strict = (rp % L) > (cp % L)
    incl = (rp % L) >= (cp % L)
    eye = jnp.where(rp == cp, 1.0, 0.0)
    blk8 = (rp // 8) == (cp // 8)
    offdiag = [((rp // (2 * b)) == (cp // (2 * b))) & ((rp // b) != (cp // b)) for b in (8, 16, 32)]
    fac = {}

    def factor_stages(items):
        pre = []
        for p, c in items:
            pv = pair_vals[p]
            cols = slice(p * LANES, (p + 1) * LANES)
            sl = slice(c * L, (c + 1) * L)
            cref = cum_scr[pl.ds(c * L + L // 2 - 1, 1), cols]
            clast = cum_scr[pl.ds(c * L + L - 1, 1), cols]
            cum_c, cumex_c = pv["cum"][sl], pv["cumex"][sl]
            r_c, k_c, v_c, av_c, bv_c = pv["r"][sl], pv["k2"][sl], pv["v"][sl], pv["av"][sl], pv["bv"][sl]
            e_out = jnp.exp(cref - cum_c)
            e_end = jnp.exp(clast - cum_c)
            pre.append(dict(
                lhs1=jnp.concatenate([stack(av_c * jnp.exp(cumex_c - cref)), stack(r_c * jnp.exp(cum_c - cref))],
                                     axis=0),
                rhs1=jnp.concatenate([stack(bv_c * e_out), stack(k_c * e_out)], axis=0),
                v_st=stack(v_c), a0_st=stack(av_c * jnp.exp(cumex_c)), r0_st=stack(r_c * jnp.exp(cum_c)),
                bk=jnp.concatenate([stack(bv_c * e_end), stack(k_c * e_end)], axis=0), decay=jnp.exp(clast)))
        yield
        x1 = [_dot(f["lhs1"], f["rhs1"], nt=True, passes=passes) for f in pre]
        yield
        a_ab = [jnp.where(strict, x[:P, :P], 0.0) for x in x1]
        a_kr = [jnp.concatenate([jnp.where(strict, x[:P, P:], 0.0), jnp.where(incl, x[P:, P:], 0.0)], axis=0)
                for x in x1]
        a_rb = [jnp.where(incl, x[P:, :P], 0.0) for x in x1]
        m1 = [_dot(a, f["v_st"], passes=passes) for a, f in zip(a_kr, pre)]
        yield
        tinv = []
        yield from _unit_lower_inverse_stages(a_ab, eye, blk8, offdiag, passes, tinv)
        m2 = [_dot(t, jnp.concatenate([f["a0_st"], m[:P]], axis=1), passes=passes) for t, f, m in zip(tinv, pre, m1)]
        for it, f, m, mm, arb in zip(items, pre, m1, m2, a_rb):
            w_st, uv_st = mm[:, :LANES], mm[:, LANES:]
            fac[it] = dict(wr=jnp.concatenate([w_st, f["r0_st"]], axis=0), uv_st=uv_st,
                           yv_st=m[P:], a_rb=arb, v_t=f["v_st"].T, decay=f["decay"], bk=f["bk"])

    state = [s_scr[p] for p in range(pairs)]

    def state_stages(chunk_ids):
        for c in chunk_ids:
            fs = [fac[(p, c)] for p in range(pairs)]
            m3 = [_dot(fs[p]["wr"], state[p], nt=True, passes=passes) for p in range(pairs)]
            yield
            u_st = [m3[p][:P] + fs[p]["uv_st"] for p in range(pairs)]
            for p in range(pairs):
                state[p] = state[p] * fs[p]["decay"] + _dot(
                    jnp.concatenate([u_st[p].T, fs[p]["v_t"]], axis=1), fs[p]["bk"], passes=passes)
            yield
            for p in range(pairs):
                y_st = m3[p][P:] + fs[p]["yv_st"] + _dot(fs[p]["a_rb"], u_st[p], passes=passes)
                y_scr[c * L:(c + 1) * L, p * LANES:(p + 1) * LANES] = y_st[:L] + y_st[L:]
            yield

    per_group = chunks // groups
    chunk_groups = [list(range(gi * per_group, (gi + 1) * per_group)) for gi in range(groups)]
    group_items = [[(p, c) for p in range(pairs) for c in cg] for cg in chunk_groups]
    _interleave(factor_stages(group_items[0]))
    for gi in range(1, groups):
        _interleave(factor_stages(group_items[gi]), state_stages(chunk_groups[gi - 1]))
    _interleave(state_stages(chunk_groups[-1]))
    for p in range(pairs):
        s_scr[p] = state[p]

    for p in range(pairs):
        cols = slice(p * LANES, (p + 1) * LANES)
        pv = pair_vals[p]
        y = y_scr[:, cols]
        mean = headsum(y) * (1.0 / N)
        yc = y - mean
        var = headsum(yc * yc) * (1.0 / N)
        yn = yc * lax.rsqrt(var + RWKV_GN_EPS) * gnw_ref[:, cols] + gnb_ref[:, cols]
        bonus = headsum(pv["r"] * pv["k2"] * rk_ref[:, cols]) * pv["v"]
        z_ref[:, cols] = ((yn + bonus) * _silu(g_ref[:, cols].astype(F32))).astype(z_ref.dtype)


def _rwkv_scan(rkvg, lora, p, batch, seq, *, out_dtype):
    t = rkvg.shape[0]
    w = D_MODEL
    pairs = RWKV_PAIRS_PER_STEP
    bw = pairs * LANES
    nblk = w // bw
    chunks = min(RWKV_CHUNKS_PER_STEP, seq // RWKV_CHUNK)
    rows = RWKV_CHUNK * chunks
    steps = seq // rows

    def act(col0):
        return pl.BlockSpec((rows, bw), lambda b, h, n: (b * steps + n, col0 + h))

    def vec():
        return pl.BlockSpec((1, bw), lambda b, h, n: (0, h))

    def w2():
        return pl.BlockSpec((LANES, bw), lambda b, h, n: (0, h))

    kern = functools.partial(_rwkv_scan_kernel, chunks=chunks, pairs=pairs, groups=min(RWKV_CHUNK_GROUPS, chunks),
                             passes=RWKV_PASSES)
    return pl.pallas_call(
        kern,
        grid=(batch, nblk, steps),
        in_specs=[act(0), act(nblk), act(2 * nblk), act(3 * nblk),
                  pl.BlockSpec((rows, 2 * LANES), lambda b, h, n: (b * steps + n, 0)),
                  w2(), w2(), vec(), vec(), vec(), vec(), vec(), vec(), vec()],
        out_specs=pl.BlockSpec((rows, bw), lambda b, h, n: (b * steps + n, h)),
        out_shape=jax.ShapeDtypeStruct((t, w), out_dtype),
        scratch_shapes=[pltpu.VMEM((pairs, LANES, LANES), F32), pltpu.VMEM((rows, bw), F32),
                        pltpu.VMEM((rows, bw), F32)],
        compiler_params=_cparams(("arbitrary", "arbitrary", "arbitrary")),
        name="rwkv_scan",
    )(rkvg, rkvg, rkvg, rkvg, lora, p["dec_w2"], p["iclr_w2"], p["dec_w0"], p["iclr_w0"],
      p["k_k"], p["k_a"], p["r_k"], p["gn_w"], p["gn_b"])


def _gla_kernel(q_ref, k_ref, v_ref, g_ref, low_ref, w2_ref, b_ref, hg_ref, o_ref, s_scr):
    R = GLA_BLOCK
    Hf = R // 2

    @pl.when(pl.program_id(1) == 0)
    def _():
        s_scr[...] = jnp.zeros_like(s_scr)

    DK, DV = GLA_HEAD_K, GLA_HEAD_V
    heads = range(GLA_HEADS)
    ri = _iota2((R, R), 0)
    ci = _iota2((R, R), 1)
    tri = jnp.where(ci <= ri, 1.0, 0.0).astype(BF16)
    rh = _iota2((Hf, Hf), 0)
    ch = _iota2((Hf, Hf), 1)
    causal = ch <= rh

    la = -_softplus(-(_dot(low_ref[...], w2_ref[...], passes=GATE_PASSES) + b_ref[...])) * (1.0 / GLA_GATE_TAU)
    bcum = _dot_exact_lhs(tri, la)
    q = [q_ref[:, h * DK:(h + 1) * DK].astype(F32) * (DK ** -0.5) for h in heads]
    k = [k_ref[:, h * DK:(h + 1) * DK].astype(F32) for h in heads]
    v = [v_ref[:, h * DV:(h + 1) * DV].astype(F32) for h in heads]
    bc = [bcum[:, h * DK:(h + 1) * DK] for h in heads]
    ref_t = [b[Hf // 2 - 1:Hf // 2] for b in bc]
    ref_m = [b[Hf - 1:Hf] for b in bc]
    ref_b = [b[Hf + Hf // 2 - 1:Hf + Hf // 2] for b in bc]
    last = [b[R - 1:R] for b in bc]
    s00 = [jnp.where(causal, _dot(q[h][:Hf] * jnp.exp(bc[h][:Hf] - ref_t[h]),
                                  k[h][:Hf] * jnp.exp(ref_t[h] - bc[h][:Hf]), nt=True), 0.0) for h in heads]
    s11 = [jnp.where(causal, _dot(q[h][Hf:] * jnp.exp(bc[h][Hf:] - ref_b[h]),
                                  k[h][Hf:] * jnp.exp(ref_b[h] - bc[h][Hf:]), nt=True), 0.0) for h in heads]
    s10 = [_dot(q[h][Hf:] * jnp.exp(bc[h][Hf:] - ref_m[h]), k[h][:Hf] * jnp.exp(ref_m[h] - bc[h][:Hf]), nt=True)
           for h in heads]
    st = [s_scr[h] for h in heads]
    o_int = [_dot(q[h] * jnp.exp(bc[h]), st[h], nt=True) for h in heads]
    o_top = [_dot(s00[h], v[h][:Hf]) for h in heads]
    o_bot = [_dot(s10[h], v[h][:Hf]) + _dot(s11[h], v[h][Hf:]) for h in heads]
    upd = [_dot(v[h].T, k[h] * jnp.exp(last[h] - bc[h])) for h in heads]
    for h in heads:
        s_scr[h] = st[h] * jnp.exp(last[h]) + upd[h]
        o = jnp.concatenate([o_top[h], o_bot[h]], axis=0) + o_int[h]
        ms = jnp.mean(o * o, axis=-1, keepdims=True)
        on = o * lax.rsqrt(ms + NORM_EPS) * hg_ref[...]
        cols = slice(h * DV, (h + 1) * DV)
        o_ref[:, cols] = (on * _silu(g_ref[:, cols].astype(F32))).astype(o_ref.dtype)


def _gla_scan(proj, low, gate_w2, gate_b, head_g, batch, seq, *, out_dtype):
    t = proj.shape[0]
    R = GLA_BLOCK
    steps = seq // R
    kw, vw = GLA_KEY_WIDTH, GLA_VALUE_WIDTH

    def row(b, n):
        return b * steps + n

    return pl.pallas_call(
        _gla_kernel,
        grid=(batch, steps),
        in_specs=[
            pl.BlockSpec((R, kw), lambda b, n: (row(b, n), 0)),
            pl.BlockSpec((R, kw), lambda b, n: (row(b, n), 1)),
            pl.BlockSpec((R, vw), lambda b, n: (row(b, n), 2 * kw // vw)),
            pl.BlockSpec((R, vw), lambda b, n: (row(b, n), 2 * kw // vw + 1)),
            pl.BlockSpec((R, LANES), lambda b, n: (row(b, n), 0)),
            pl.BlockSpec((LANES, kw), lambda b, n: (0, 0)),
            pl.BlockSpec((1, kw), lambda b, n: (0, 0)),
            pl.BlockSpec((1, GLA_HEAD_V), lambda b, n: (0, 0)),
        ],
        out_specs=pl.BlockSpec((R, vw), lambda b, n: (row(b, n), 0)),
        out_shape=jax.ShapeDtypeStruct((t, vw), out_dtype),
        scratch_shapes=[pltpu.VMEM((GLA_HEADS, GLA_HEAD_V, GLA_HEAD_K), F32)],
        compiler_params=_cparams(("arbitrary", "arbitrary")),
        name="gla_scan",
    )(proj, proj, proj, proj, low, gate_w2, gate_b, head_g)


def _ssd_kernel(z_ref, xr_ref, br_ref, cr_ref, dt_ref, cw_ref, cbias_ref, dtb_ref, alog_ref, dsk_ref, ng_ref, o_ref,
                st_scr, raw_scr, xs_ref, bm_ref, cm_ref):
    C = SSD_CHUNK
    P = SSM_HEADDIM
    pairs_per_group = (SSM_HEADS // SSM_GROUPS) // 2
    gw = SSM_WIDTH // SSM_GROUPS

    @pl.when(pl.program_id(1) == 0)
    def _():
        st_scr[...] = jnp.zeros_like(st_scr)
        raw_scr[...] = jnp.zeros_like(raw_scr)

    raw_scr[0:SUBLANES, :] = raw_scr[C:C + SUBLANES, :]
    col = 0
    for src_ref, act_ref in ((xr_ref, xs_ref), (br_ref, bm_ref), (cr_ref, cm_ref)):
        width = src_ref.shape[1]
        cols = slice(col, col + width)
        raw_scr[SUBLANES:, cols] = src_ref[...].astype(F32)
        acc = cbias_ref[:, cols]
        for s in range(SSM_CONV):
            tap = raw_scr[SUBLANES - s:SUBLANES - s + C, cols]
            acc = acc + tap * cw_ref[SSM_CONV - 1 - s:SSM_CONV - s, cols]
        act_ref[...] = _silu(acc)
        col += width

    dt = _softplus(dt_ref[...] + dtb_ref[...])
    da = dt * (-jnp.exp(alog_ref[...]))
    ri = _iota2((C, C), 0)
    ci = _iota2((C, C), 1)
    causal = ci <= ri
    tri = jnp.where(causal, 1.0, 0.0).astype(BF16)
    acum = _dot_exact_lhs(tri, da)
    acum_t = acum.T
    alast = acum[C - 1:C, :]
    dsk = dsk_ref[...]

    lane = _iota2((1, LANES), 1)
    m0 = lane < P
    mf0 = m0.astype(F32)
    mf1 = 1.0 - mf0

    def pair_cols(x, h0):
        return jnp.where(m0, x[:, h0:h0 + 1], x[:, h0 + 1:h0 + 2])

    for g in range(SSM_GROUPS):
        bm = bm_ref[:, g * SSM_STATE:(g + 1) * SSM_STATE].astype(F32)
        cm = cm_ref[:, g * SSM_STATE:(g + 1) * SSM_STATE].astype(F32)
        cb = _dot(cm, bm, nt=True)
        bm_t = bm.T
        ys = []
        for pp in range(pairs_per_group):
            pidx = g * pairs_per_group + pp
            h0 = 2 * pidx
            cols = slice(pidx * LANES, (pidx + 1) * LANES)
            x_p = xs_ref[:, cols].astype(F32)
            dt_p = pair_cols(dt, h0)
            ac_p = pair_cols(acum, h0)
            al_p = pair_cols(alast, h0)
            xc = x_p * dt_p
            dec0 = jnp.where(causal, jnp.exp(acum[:, h0:h0 + 1] - acum_t[h0:h0 + 1, :]), 0.0)
            dec1 = jnp.where(causal, jnp.exp(acum[:, h0 + 1:h0 + 2] - acum_t[h0 + 1:h0 + 2, :]), 0.0)
            lhs = jnp.concatenate([cb * dec0, cb * dec1], axis=1)
            rhs = jnp.concatenate([xc * mf0, xc * mf1], axis=0)
            prev = st_scr[pidx]
            y = _dot(lhs, rhs) + _dot(cm, prev) * jnp.exp(ac_p) + pair_cols(dsk, h0) * x_p
            st_scr[pidx] = prev * jnp.exp(al_p) + _dot(bm_t, xc * jnp.exp(al_p - ac_p))
            ys.append(y * _silu(z_ref[:, cols].astype(F32)))
        yg = jnp.concatenate(ys, axis=1)
        ms = jnp.mean(yg * yg, axis=-1, keepdims=True)
        o_ref[:, g * gw:(g + 1) * gw] = (yg * lax.rsqrt(ms + SSM_NORM_EPS) * ng_ref[:, g * gw:(g + 1) * gw]).astype(o_ref.dtype)


def _ssd_scan(proj, dt, conv_w, conv_b, dt_bias, a_log, d_skip, norm_g, batch, seq, *, out_dtype):
    t = proj.shape[0]
    C = SSD_CHUNK
    steps = seq // C
    npairs = SSM_HEADS // 2

    def row(b, n):
        return b * steps + n

    bcol = 2 * SSM_WIDTH // SSM_BC_WIDTH
    return pl.pallas_call(
        _ssd_kernel,
        grid=(batch, steps),
        in_specs=[
            pl.BlockSpec((C, SSM_WIDTH), lambda b, n: (row(b, n), 0)),
            pl.BlockSpec((C, SSM_WIDTH), lambda b, n: (row(b, n), 1)),
            pl.BlockSpec((C, SSM_BC_WIDTH), lambda b, n: (row(b, n), bcol)),
            pl.BlockSpec((C, SSM_BC_WIDTH), lambda b, n: (row(b, n), bcol + 1)),
            pl.BlockSpec((C, LANES), lambda b, n: (row(b, n), 0)),
            pl.BlockSpec((SSM_CONV, SSM_CONV_WIDTH), lambda b, n: (0, 0)),
            pl.BlockSpec((1, SSM_CONV_WIDTH), lambda b, n: (0, 0)),
            pl.BlockSpec((1, LANES), lambda b, n: (0, 0)),
            pl.BlockSpec((1, LANES), lambda b, n: (0, 0)),
            pl.BlockSpec((1, LANES), lambda b, n: (0, 0)),
            pl.BlockSpec((1, SSM_WIDTH), lambda b, n: (0, 0)),
        ],
        out_specs=pl.BlockSpec((C, SSM_WIDTH), lambda b, n: (row(b, n), 0)),
        out_shape=jax.ShapeDtypeStruct((t, SSM_WIDTH), out_dtype),
        scratch_shapes=[pltpu.VMEM((npairs, SSM_STATE, LANES), F32), pltpu.VMEM((C + SUBLANES, SSM_CONV_WIDTH), F32),
                        pltpu.VMEM((C, SSM_WIDTH), F32), pltpu.VMEM((C, SSM_BC_WIDTH), F32),
                        pltpu.VMEM((C, SSM_BC_WIDTH), F32)],
        compiler_params=_cparams(("arbitrary", "arbitrary")),
        name="ssd_scan",
    )(proj, proj, proj, proj, dt, conv_w, conv_b, dt_bias, a_log, d_skip, norm_g)


def _pad_cols(w, n):
    return jnp.pad(w, ((0, 0), (0, n - w.shape[1])))


def _pad_rows(w, n):
    return jnp.pad(w, ((0, n - w.shape[0]), (0, 0)))


ACT_DTYPE = BF16


def kernel(x, c, ada_w, ada_b, norm_g, final_g, rwkv_mu, rwkv_w_in, rwkv_dec_w1, rwkv_dec_w2, rwkv_dec_w0, rwkv_iclr_w1, rwkv_iclr_w2, rwkv_iclr_w0, rwkv_k_k, rwkv_k_a, rwkv_r_k, rwkv_gn_w, rwkv_gn_b, rwkv_w_out, gla_w_in, gla_gate_w2, gla_gate_b, gla_head_g, gla_w_out, ssd_w_in, ssd_conv_w, ssd_conv_b, ssd_dt_bias, ssd_a_log, ssd_d, ssd_norm_g, ssd_w_out):
    batch, seq, d = x.shape
    t = batch * seq
    xf = x.reshape(t, d)

    c_pad = jnp.pad(c, ((0, SUBLANES - batch % SUBLANES if batch % SUBLANES else 0), (0, 0)))
    mod = _ada_mod(c_pad, ada_w, ada_b)[:, :batch].reshape(DEPTH, batch, 3, d)

    for i in range(DEPTH):
        kind, j = i % N_MIXERS, i // N_MIXERS
        g = norm_g[i].reshape(1, d)
        mod_l = mod[i]
        if kind == 0:
            w1 = jnp.concatenate([_pad_cols(rwkv_dec_w1[j], LANES), _pad_cols(rwkv_iclr_w1[j], LANES)], axis=1)
            xs, lora = _rwkv_prenorm(xf, g, mod_l, rwkv_mu[j], w1.astype(BF16), seq, tm=ROW_TILE)
            rkvg = _proj(xs, rwkv_w_in, j, 4 * d, seq, tm=PROJ_ROW_TILE, tn=1024, out_dtype=ACT_DTYPE)
            row = lambda v: v.reshape(1, -1)
            params = dict(dec_w2=_pad_rows(rwkv_dec_w2[j], LANES), iclr_w2=_pad_rows(rwkv_iclr_w2[j], LANES),
                          dec_w0=row(rwkv_dec_w0[j]), iclr_w0=row(rwkv_iclr_w0[j]), k_k=row(rwkv_k_k[j]),
                          k_a=row(rwkv_k_a[j]), r_k=row(rwkv_r_k[j]), gn_w=row(rwkv_gn_w[j]), gn_b=row(rwkv_gn_b[j]))
            z = _rwkv_scan(rkvg, lora, params, batch, seq, out_dtype=ACT_DTYPE)
            xf = _outproj(z, rwkv_w_out, j, xf, mod_l, seq, tm=PROJ_ROW_TILE, tn=1024)
        elif kind == 1:
            nmain = 2 * GLA_KEY_WIDTH + 2 * GLA_VALUE_WIDTH
            w_bf = gla_w_in[j:j + 1].astype(BF16)
            h, low = _prenorm_call(xf, g, mod_l, _pad_cols(w_bf[0, :, nmain:], LANES), seq, tm=ROW_TILE)
            proj = _proj(h[None], w_bf, 0, nmain, seq, tm=PROJ_ROW_TILE, tn=1024, out_dtype=ACT_DTYPE)
            z = _gla_scan(proj, low, _pad_rows(gla_gate_w2[j], LANES), gla_gate_b[j].reshape(1, -1),
                          gla_head_g[j].reshape(1, -1), batch, seq, out_dtype=ACT_DTYPE)
            xf = _outproj(z, gla_w_out, j, xf, mod_l, seq, tm=PROJ_ROW_TILE, tn=1024)
        else:
            nmain = SSM_WIDTH + SSM_CONV_WIDTH
            w_bf = ssd_w_in[j:j + 1].astype(BF16)
            h, dt = _prenorm_call(xf, g, mod_l, _pad_cols(w_bf[0, :, nmain:], LANES), seq, tm=ROW_TILE)
            proj = _proj(h[None], w_bf, 0, nmain, seq, tm=PROJ_ROW_TILE, tn=1024, out_dtype=ACT_DTYPE)
            padl = lambda v: _pad_cols(v.reshape(1, -1), LANES)
            z = _ssd_scan(proj, dt, ssd_conv_w[j], ssd_conv_b[j].reshape(1, -1),
                          padl(ssd_dt_bias[j]), padl(ssd_a_log[j]), padl(ssd_d[j]),
                          ssd_norm_g[j].reshape(1, -1), batch, seq, out_dtype=ACT_DTYPE)
            xf = _outproj(z, ssd_w_out, j, xf, mod_l, seq, tm=PROJ_ROW_TILE, tn=512)

    out = _final_norm(xf, final_g.reshape(1, d))
    return out.reshape(batch, seq, d)
```

```python
import functools
import math

import jax
import jax.numpy as jnp
from jax import lax
from jax.experimental import pallas as pl
from jax.experimental.pallas import tpu as pltpu

F32 = jnp.float32
BF16 = jnp.bfloat16

D_MODEL = 2048
DEPTH = 4
N_MIXERS = 3
NORM_EPS = 1e-6

RWKV_HEAD = 64
RWKV_LORA = 96
RWKV_GN_EPS = 64e-5

GLA_HEADS = 4
GLA_KEY_WIDTH = D_MODEL // 2
GLA_VALUE_WIDTH = D_MODEL
GLA_HEAD_K = GLA_KEY_WIDTH // GLA_HEADS
GLA_HEAD_V = GLA_VALUE_WIDTH // GLA_HEADS
GLA_GATE_RANK = 16
GLA_GATE_TAU = 16.0

SSM_WIDTH = 2 * D_MODEL
SSM_HEADDIM = 64
SSM_HEADS = SSM_WIDTH // SSM_HEADDIM
SSM_STATE = 128
SSM_GROUPS = 8
SSM_CONV = 4
SSM_NORM_EPS = 1e-5
SSM_BC_WIDTH = SSM_GROUPS * SSM_STATE
SSM_CONV_WIDTH = SSM_WIDTH + 2 * SSM_BC_WIDTH

LANES = 128
SUBLANES = 8
VMEM_LIMIT_BYTES = 52 * 1024 * 1024

RWKV_CHUNK = 64
RWKV_CHUNKS_PER_STEP = 2
RWKV_PAIRS_PER_STEP = 16
RWKV_CHUNK_GROUPS = 1
GLA_BLOCK = 128
SSD_CHUNK = 128

RWKV_PASSES = 1
GATE_PASSES = 3

LOG2E = math.log2(math.e)

ROW_TILE = 512
PROJ_ROW_TILE = 1024


def _cparams(sem):
    return pltpu.CompilerParams(dimension_semantics=sem, vmem_limit_bytes=VMEM_LIMIT_BYTES)


def _dot(a, b, *, nt=False, passes=1):
    dims = (((1,), (1,)), ((), ())) if nt else (((1,), (0,)), ((), ()))

    def d(x, y):
        return lax.dot_general(x, y, dims, preferred_element_type=F32)

    ah = a.astype(BF16)
    bh = b.astype(BF16)
    if passes == 1:
        return d(ah, bh)
    al = (a.astype(F32) - ah.astype(F32)).astype(BF16)
    bl = (b.astype(F32) - bh.astype(F32)).astype(BF16)
    return d(ah, bh) + (d(ah, bl) + d(al, bh))


def _dot_exact_lhs(m_bf16, x):
    x1 = x.astype(BF16)
    r1 = x - x1.astype(F32)
    x2 = r1.astype(BF16)
    x3 = (r1 - x2.astype(F32)).astype(BF16)

    def d(y):
        return jnp.dot(m_bf16, y, preferred_element_type=F32)

    return d(x1) + (d(x2) + d(x3))


def _silu(x):
    hx = 0.5 * x
    return hx + hx * jnp.tanh(hx)


def _softplus(x):
    return jnp.maximum(x, 0.0) + jnp.log(1.0 + jnp.exp(-jnp.abs(x)))


def _iota2(shape, dim):
    return lax.broadcasted_iota(jnp.int32, shape, dim)


def _ada_kernel(c_ref, w_ref, b_ref, o_ref):
    c = c_ref[...]
    o_ref[...] = _dot(_silu(c), w_ref[...]) + b_ref[...]


def _ada_mod(c_pad, ada_w, ada_b, tn=1024):
    depth, d, n = ada_w.shape
    rows = c_pad.shape[0]
    return pl.pallas_call(
        _ada_kernel,
        grid=(depth, n // tn),
        in_specs=[
            pl.BlockSpec((rows, d), lambda l, j: (0, 0)),
            pl.BlockSpec((None, d, tn), lambda l, j: (l, 0, j)),
            pl.BlockSpec((None, 1, tn), lambda l, j: (l, 0, j)),
        ],
        out_specs=pl.BlockSpec((None, rows, tn), lambda l, j: (l, 0, j)),
        out_shape=jax.ShapeDtypeStruct((depth, rows, n), F32),
        compiler_params=_cparams(("arbitrary", "arbitrary")),
        name="ada_mod",
    )(c_pad, ada_w, ada_b.reshape(depth, 1, n))


def _prenorm(x, g, mod):
    ms = jnp.mean(x * x, axis=-1, keepdims=True)
    return x * lax.rsqrt(ms + NORM_EPS) * g * (1.0 + mod[1:2, :]) + mod[0:1, :]


def _prenorm_kernel(x_ref, g_ref, mod_ref, ws_ref, h_ref, os_ref):
    h = _prenorm(x_ref[...], g_ref[...], mod_ref[...]).astype(BF16)
    h_ref[...] = h
    os_ref[...] = jnp.dot(h, ws_ref[...], preferred_element_type=F32)


def _prenorm_call(x, g, mod_l, w_side, seq, *, tm):
    t, d = x.shape
    ns = w_side.shape[1]
    tm = min(tm, seq)
    tiles_per_seq = seq // tm
    return pl.pallas_call(
        _prenorm_kernel,
        grid=(t // tm,),
        in_specs=[
            pl.BlockSpec((tm, d), lambda i: (i, 0)),
            pl.BlockSpec((1, d), lambda i: (0, 0)),
            pl.BlockSpec((None, 3, d), lambda i: (i // tiles_per_seq, 0, 0)),
            pl.BlockSpec((d, ns), lambda i: (0, 0)),
        ],
        out_specs=[pl.BlockSpec((tm, d), lambda i: (i, 0)), pl.BlockSpec((tm, ns), lambda i: (i, 0))],
        out_shape=[jax.ShapeDtypeStruct((t, d), BF16), jax.ShapeDtypeStruct((t, ns), F32)],
        compiler_params=_cparams(("arbitrary",)),
        name="prenorm",
    )(x, g, mod_l, w_side)


def _proj_kernel(a_ref, w_ref, o_ref, wb_scr):
    @pl.when(pl.program_id(1) == 0)
    def _():
        wb_scr[...] = w_ref[...].astype(BF16)

    o_ref[...] = jnp.dot(a_ref[...], wb_scr[...], preferred_element_type=F32).astype(o_ref.dtype)


def _proj(a, w, layer, n, seq, *, tm, tn, out_dtype):
    groups, t, kd = a.shape
    tm = min(tm, seq)
    tiles_per_group = (n // tn) // groups
    return pl.pallas_call(
        _proj_kernel,
        grid=(n // tn, t // tm),
        in_specs=[
            pl.BlockSpec((None, tm, kd), lambda j, i: (j // tiles_per_group, i, 0)),
            pl.BlockSpec((None, kd, tn), lambda j, i: (layer, 0, j)),
        ],
        out_specs=pl.BlockSpec((tm, tn), lambda j, i: (i, j)),
        out_shape=jax.ShapeDtypeStruct((t, n), out_dtype),
        scratch_shapes=[pltpu.VMEM((kd, tn), BF16)],
        compiler_params=_cparams(("arbitrary", "arbitrary")),
        name="proj",
    )(a, w)


def _outproj_kernel(z_ref, w_ref, x_ref, mod_ref, o_ref, wb_scr):
    @pl.when(pl.program_id(1) == 0)
    def _():
        wb_scr[...] = w_ref[...].astype(BF16)

    acc = jnp.dot(z_ref[...], wb_scr[...], preferred_element_type=F32)
    o_ref[...] = x_ref[...] + mod_ref[2:3, :] * acc


def _outproj(z, w, layer, x, mod_l, seq, *, tm, tn):
    t, kd = z.shape
    n = w.shape[2]
    tm = min(tm, seq)
    tiles_per_seq = seq // tm
    return pl.pallas_call(
        _outproj_kernel,
        grid=(n // tn, t // tm),
        in_specs=[
            pl.BlockSpec((tm, kd), lambda j, i: (i, 0)),
            pl.BlockSpec((None, kd, tn), lambda j, i: (layer, 0, j)),
            pl.BlockSpec((tm, tn), lambda j, i: (i, j)),
            pl.BlockSpec((None, 3, tn), lambda j, i: (i // tiles_per_seq, 0, j)),
        ],
        out_specs=pl.BlockSpec((tm, tn), lambda j, i: (i, j)),
        out_shape=jax.ShapeDtypeStruct((t, n), F32),
        scratch_shapes=[pltpu.VMEM((kd, tn), BF16)],
        compiler_params=_cparams(("arbitrary", "arbitrary")),
        name="outproj",
    )(z, w, x, mod_l)


def _final_norm_kernel(x_ref, g_ref, o_ref):
    x = x_ref[...]
    ms = jnp.mean(x * x, axis=-1, keepdims=True)
    o_ref[...] = x * lax.rsqrt(ms + NORM_EPS) * g_ref[...]


def _final_norm(x, g, *, tm=512):
    t, d = x.shape
    tm = min(tm, t)
    return pl.pallas_call(
        _final_norm_kernel,
        grid=(t // tm,),
        in_specs=[pl.BlockSpec((tm, d), lambda i: (i, 0)), pl.BlockSpec((1, d), lambda i: (0, 0))],
        out_specs=pl.BlockSpec((tm, d), lambda i: (i, 0)),
        out_shape=jax.ShapeDtypeStruct((t, d), F32),
        compiler_params=_cparams(("arbitrary",)),
        name="final_norm",
    )(x, g)


def _rwkv_prenorm_kernel(x_ref, xh_ref, g_ref, mod_ref, mu_ref, w1_ref, xs_ref, lora_ref, *, tiles_per_seq):
    i = pl.program_id(0)
    g = g_ref[...]
    mod = mod_ref[...]
    h = _prenorm(x_ref[...], g, mod)
    hp8 = _prenorm(xh_ref[...], g, mod)
    first = (i % tiles_per_seq) == 0
    prev_row = jnp.where(first, 0.0, hp8[SUBLANES - 1:SUBLANES, :])
    rolled = pltpu.roll(h, 1, 0)
    row0 = _iota2(h.shape, 0) == 0
    dh = jnp.where(row0, prev_row, rolled) - h
    mu = mu_ref[...]
    for c in range(4):
        xs_ref[c] = (h + dh * mu[c:c + 1, :]).astype(BF16)
    xw = (h + dh * mu[4:5, :]).astype(BF16)
    xa = (h + dh * mu[5:6, :]).astype(BF16)
    w1 = w1_ref[...]
    dec_h = jnp.tanh(jnp.dot(xw, w1[:, :LANES], preferred_element_type=F32))
    icl_h = jnp.dot(xa, w1[:, LANES:], preferred_element_type=F32)
    lora_ref[...] = jnp.concatenate([dec_h, icl_h], axis=1)


def _rwkv_prenorm(x, g, mod_l, mu, w1, seq, *, tm):
    t, d = x.shape
    tm = min(tm, seq)
    tiles_per_seq = seq // tm
    halo_blocks = tm // SUBLANES
    kern = functools.partial(_rwkv_prenorm_kernel, tiles_per_seq=tiles_per_seq)
    return pl.pallas_call(
        kern,
        grid=(t // tm,),
        in_specs=[
            pl.BlockSpec((tm, d), lambda i: (i, 0)),
            pl.BlockSpec((SUBLANES, d), lambda i: (jnp.maximum(i * halo_blocks - 1, 0), 0)),
            pl.BlockSpec((1, d), lambda i: (0, 0)),
            pl.BlockSpec((None, 3, d), lambda i: (i // tiles_per_seq, 0, 0)),
            pl.BlockSpec((6, d), lambda i: (0, 0)),
            pl.BlockSpec((d, 2 * LANES), lambda i: (0, 0)),
        ],
        out_specs=[
            pl.BlockSpec((4, tm, d), lambda i: (0, i, 0)),
            pl.BlockSpec((tm, 2 * LANES), lambda i: (i, 0)),
        ],
        out_shape=[
            jax.ShapeDtypeStruct((4, t, d), BF16),
            jax.ShapeDtypeStruct((t, 2 * LANES), F32),
        ],
        compiler_params=_cparams(("arbitrary",)),
        name="rwkv_prenorm",
    )(x, x, g, mod_l, mu, w1)


def _interleave(*gens):
    live = list(gens)
    while live:
        for gen in list(live):
            try:
                next(gen)
            except StopIteration:
                live.remove(gen)


def _unit_lower_inverse_stages(a_list, eye, blk8, offdiag, passes, out):
    n = eye.shape[0]
    a8 = [jnp.where(blk8, a, 0.0) for a in a_list]
    a2 = [_dot(x, x, passes=passes) for x in a8]
    yield
    ia8 = [eye + x for x in a8]
    both = [_dot(jnp.concatenate([y, x], axis=0), y, passes=passes) for x, y in zip(ia8, a2)]
    yield
    inv = [_dot(x + b[n:], eye + b[:n], passes=passes) for x, b in zip(ia8, both)]
    yield
    for m in offdiag:
        t = [_dot(jnp.where(m, a, 0.0), i, passes=passes) for a, i in zip(a_list, inv)]
        yield
        inv = [i + _dot(i, x, passes=passes) for i, x in zip(inv, t)]
        yield
    out.extend(inv)


def _rwkv_scan_kernel(r_ref, k_ref, v_ref, g_ref, lora_ref, dw2_ref, iw2_ref, dw0_ref, iw0_ref,
                      kk_ref, ka_ref, rk_ref, gnw_ref, gnb_ref, z_ref, s_scr, cum_scr, y_scr,
                      *, chunks, pairs, groups, passes):
    L = RWKV_CHUNK
    N = RWKV_HEAD
    R = L * chunks
    P = 2 * L

    @pl.when(pl.program_id(2) == 0)
    def _():
        s_scr[...] = jnp.zeros_like(s_scr)

    lane = _iota2((1, LANES), 1)
    m0 = lane < N
    mf0 = m0.astype(F32)
    mf1 = 1.0 - mf0

    def headsum(x):
        s0 = jnp.sum(jnp.where(m0, x, 0.0), axis=-1, keepdims=True)
        s1 = jnp.sum(jnp.where(m0, 0.0, x), axis=-1, keepdims=True)
        return jnp.where(m0, s0, s1)

    def stack(x):
        return jnp.concatenate([x * mf0, x * mf1], axis=0)

    lora = lora_ref[...]
    dec = dw0_ref[...] + _dot(lora[:, :LANES], dw2_ref[...], passes=passes)
    lw_all = -jnp.exp(-_softplus(-dec) - 0.5) * LOG2E
    a_all = jax.nn.sigmoid(iw0_ref[...] + _dot(lora[:, LANES:], iw2_ref[...], passes=passes))
    tri = jnp.where(_iota2((L, L), 1) <= _iota2((L, L), 0), 1.0, 0.0).astype(BF16)
    cum_all = jnp.concatenate([_dot_exact_lhs(tri, lw_all[c * L:(c + 1) * L]) for c in range(chunks)], axis=0)
    cum_scr[...] = cum_all

    pair_vals = []
    for p in range(pairs):
        cols = slice(p * LANES, (p + 1) * LANES)
        r = r_ref[:, cols].astype(F32)
        k = k_ref[:, cols].astype(F32)
        v = v_ref[:, cols].astype(F32)
        a = a_all[:, cols]
        kkr = k * kk_ref[:, cols]
        kk = kkr / jnp.maximum(jnp.sqrt(headsum(kkr * kkr)), 1e-12)
        k2 = k * (1.0 + (a - 1.0) * ka_ref[:, cols])
        cum = cum_all[:, cols]
        pair_vals.append(dict(r=r, k2=k2, v=v, av=-kk, bv=kk * a, cum=cum, cumex=cum - lw_all[:, cols]))

    rp = _iota2((P, P), 0)
    cp = _iota2((P, P), 1)
    strict = (rp % L) > (cp % L)
    incl = (rp % L) >= (cp % L)
    eye = jnp.where(rp == cp, 1.0, 0.0)
    blk8 = (rp // 8) == (cp // 8)
    offdiag = [((rp // (2 * b)) == (cp // (2 * b))) & ((rp // b) != (cp // b)) for b in (8, 16, 32)]
    fac = {}

    def factor_stages(items):
        pre = []
        for p, c in items:
            pv = pair_vals[p]
            cols = slice(p * LANES, (p + 1) * LANES)
            sl = slice(c * L, (c + 1) * L)
            cref = cum_scr[pl.ds(c * L + L // 2 - 1, 1), cols]
            clast = cum_scr[pl.ds(c * L + L - 1, 1), cols]
            cum_c, cumex_c = pv["cum"][sl], pv["cumex"][sl]
            r_c, k_c, v_c, av_c, bv_c = pv["r"][sl], pv["k2"][sl], pv["v"][sl], pv["av"][sl], pv["bv"][sl]
            e_out = jnp.exp2(cref - cum_c)
            e_end = jnp.exp2(clast - cum_c)
            pre.append(dict(
                lhs1=jnp.concatenate([stack(av_c * jnp.exp2(cumex_c - cref)), stack(r_c * jnp.exp2(cum_c - cref))],
                                     axis=0),
                rhs1=jnp.concatenate([stack(bv_c * e_out), stack(k_c * e_out)], axis=0),
                v_st=stack(v_c), a0_st=stack(av_c * jnp.exp2(cumex_c)), r0_st=stack(r_c * jnp.exp2(cum_c)),
                bk=jnp.concatenate([stack(bv_c * e_end), stack(k_c * e_end)], axis=0), decay=jnp.exp2(clast)))
        yield
        x1 = [_dot(f["lhs1"], f["rhs1"], nt=True, passes=passes) for f in pre]
        yield
        a_ab = [jnp.where(strict, x[:P, :P], 0.0) for x in x1]
        a_kr = [jnp.concatenate([jnp.where(strict, x[:P, P:], 0.0), jnp.where(incl, x[P:, P:], 0.0)], axis=0)
                for x in x1]
        a_rb = [jnp.where(incl, x[P:, :P], 0.0) for x in x1]
        m1 = [_dot(a, f["v_st"], passes=passes) for a, f in zip(a_kr, pre)]
        yield
        tinv = []
        yield from _unit_lower_inverse_stages(a_ab, eye, blk8, offdiag, passes, tinv)
        m2 = [_dot(t, jnp.concatenate([f["a0_st"], m[:P]], axis=1), passes=passes) for t, f, m in zip(tinv, pre, m1)]
        for it, f, m, mm, arb in zip(items, pre, m1, m2, a_rb):
            w_st, uv_st = mm[:, :LANES], mm[:, LANES:]
            fac[it] = dict(wr=jnp.concatenate([w_st, f["r0_st"]], axis=0), uv_st=uv_st,
                           yv_st=m[P:], a_rb=arb, v_t=f["v_st"].T, decay=f["decay"], bk=f["bk"])

    state = [s_scr[p] for p in range(pairs)]

    def state_stages(chunk_ids):
        for c in chunk_ids:
            fs = [fac[(p, c)] for p in range(pairs)]
            m3 = [_dot(fs[p]["wr"], state[p], nt=True, passes=passes) for p in range(pairs)]
            yield
            u_st = [m3[p][:P] + fs[p]["uv_st"] for p in range(pairs)]
            for p in range(pairs):
                state[p] = state[p] * fs[p]["decay"] + _dot(
                    jnp.concatenate([u_st[p].T, fs[p]["v_t"]], axis=1), fs[p]["bk"], passes=passes)
            yield
            for p in range(pairs):
                y_st = m3[p][P:] + fs[p]["yv_st"] + _dot(fs[p]["a_rb"], u_st[p], passes=passes)
                y_scr[c * L:(c + 1) * L, p * LANES:(p + 1) * LANES] = y_st[:L] + y_st[L:]
            yield

    per_group = chunks // groups
    chunk_groups = [list(range(gi * per_group, (gi + 1) * per_group)) for gi in range(groups)]
    group_items = [[(p, c) for p in range(pairs) for c in cg] for cg in chunk_groups]
    _interleave(factor_stages(group_items[0]))
    for gi in range(1, groups):
        _interleave(factor_stages(group_items[gi]), state_stages(chunk_groups[gi - 1]))
    _interleave(state_stages(chunk_groups[-1]))
    for p in range(pairs):
        s_scr[p] = state[p]

    for p in range(pairs):
        cols = slice(p * LANES, (p + 1) * LANES)
        pv = pair_vals[p]
        y = y_scr[:, cols]
        mean = headsum(y) * (1.0 / N)
        yc = y - mean
        var = headsum(yc * yc) * (1.0 / N)
        yn = yc * lax.rsqrt(var + RWKV_GN_EPS) * gnw_ref[:, cols] + gnb_ref[:, cols]
        bonus = headsum(pv["r"] * pv["k2"] * rk_ref[:, cols]) * pv["v"]
        z_ref[:, cols] = ((yn + bonus) * _silu(g_ref[:, cols].astype(F32))).astype(z_ref.dtype)


def _rwkv_scan(rkvg, lora, p, batch, seq, *, out_dtype):
    t = rkvg.shape[0]
    w = D_MODEL
    pairs = RWKV_PAIRS_PER_STEP
    bw = pairs * LANES
    nblk = w // bw
    chunks = min(RWKV_CHUNKS_PER_STEP, seq // RWKV_CHUNK)
    rows = RWKV_CHUNK * chunks
    steps = seq // rows

    def act(col0):
        return pl.BlockSpec((rows, bw), lambda b, h, n: (b * steps + n, col0 + h))

    def vec():
        return pl.BlockSpec((1, bw), lambda b, h, n: (0, h))

    def w2():
        return pl.BlockSpec((LANES, bw), lambda b, h, n: (0, h))

    kern = functools.partial(_rwkv_scan_kernel, chunks=chunks, pairs=pairs, groups=min(RWKV_CHUNK_GROUPS, chunks),
                             passes=RWKV_PASSES)
    return pl.pallas_call(
        kern,
        grid=(batch, nblk, steps),
        in_specs=[act(0), act(nblk), act(2 * nblk), act(3 * nblk),
                  pl.BlockSpec((rows, 2 * LANES), lambda b, h, n: (b * steps + n, 0)),
                  w2(), w2(), vec(), vec(), vec(), vec(), vec(), vec(), vec()],
        out_specs=pl.BlockSpec((rows, bw), lambda b, h, n: (b * steps + n, h)),
        out_shape=jax.ShapeDtypeStruct((t, w), out_dtype),
        scratch_shapes=[pltpu.VMEM((pairs, LANES, LANES), F32), pltpu.VMEM((rows, bw), F32),
                        pltpu.VMEM((rows, bw), F32)],
        compiler_params=_cparams(("arbitrary", "arbitrary", "arbitrary")),
        name="rwkv_scan",
    )(rkvg, rkvg, rkvg, rkvg, lora, p["dec_w2"], p["iclr_w2"], p["dec_w0"], p["iclr_w0"],
      p["k_k"], p["k_a"], p["r_k"], p["gn_w"], p["gn_b"])


def _gla_kernel(q_ref, k_ref, v_ref, g_ref, low_ref, w2_ref, b_ref, hg_ref, o_ref, s_scr):
    R = GLA_BLOCK
    Hf = R // 2

    @pl.when(pl.program_id(1) == 0)
    def _():
        s_scr[...] = jnp.zeros_like(s_scr)

    DK, DV = GLA_HEAD_K, GLA_HEAD_V
    heads = range(GLA_HEADS)
    ri = _iota2((R, R), 0)
    ci = _iota2((R, R), 1)
    tri = jnp.where(ci <= ri, 1.0, 0.0).astype(BF16)
    rh = _iota2((Hf, Hf), 0)
    ch = _iota2((Hf, Hf), 1)
    causal = ch <= rh

    la = -_softplus(-(_dot(low_ref[...], w2_ref[...], passes=GATE_PASSES) + b_ref[...])) * (LOG2E / GLA_GATE_TAU)
    bcum = _dot_exact_lhs(tri, la)
    q = [q_ref[:, h * DK:(h + 1) * DK].astype(F32) * (DK ** -0.5) for h in heads]
    k = [k_ref[:, h * DK:(h + 1) * DK].astype(F32) for h in heads]
    v = [v_ref[:, h * DV:(h + 1) * DV].astype(F32) for h in heads]
    bc = [bcum[:, h * DK:(h + 1) * DK] for h in heads]
    ref_t = [b[Hf // 2 - 1:Hf // 2] for b in bc]
    ref_m = [b[Hf - 1:Hf] for b in bc]
    ref_b = [b[Hf + Hf // 2 - 1:Hf + Hf // 2] for b in bc]
    last = [b[R - 1:R] for b in bc]
    s00 = [jnp.where(causal, _dot(q[h][:Hf] * jnp.exp2(bc[h][:Hf] - ref_t[h]),
                                  k[h][:Hf] * jnp.exp2(ref_t[h] - bc[h][:Hf]), nt=True), 0.0) for h in heads]
    s11 = [jnp.where(causal, _dot(q[h][Hf:] * jnp.exp2(bc[h][Hf:] - ref_b[h]),
                                  k[h][Hf:] * jnp.exp2(ref_b[h] - bc[h][Hf:]), nt=True), 0.0) for h in heads]
    s10 = [_dot(q[h][Hf:] * jnp.exp2(bc[h][Hf:] - ref_m[h]), k[h][:Hf] * jnp.exp2(ref_m[h] - bc[h][:Hf]), nt=True)
           for h in heads]
    st = [s_scr[h] for h in heads]
    o_int = [_dot(q[h] * jnp.exp2(bc[h]), st[h], nt=True) for h in heads]
    o_top = [_dot(s00[h], v[h][:Hf]) for h in heads]
    o_bot = [_dot(s10[h], v[h][:Hf]) + _dot(s11[h], v[h][Hf:]) for h in heads]
    upd = [_dot(v[h].T, k[h] * jnp.exp2(last[h] - bc[h])) for h in heads]
    for h in heads:
        s_scr[h] = st[h] * jnp.exp2(last[h]) + upd[h]
        o = jnp.concatenate([o_top[h], o_bot[h]], axis=0) + o_int[h]
        ms = jnp.mean(o * o, axis=-1, keepdims=True)
        on = o * lax.rsqrt(ms + NORM_EPS) * hg_ref[...]
        cols = slice(h * DV, (h + 1) * DV)
        o_ref[:, cols] = (on * _silu(g_ref[:, cols].astype(F32))).astype(o_ref.dtype)


def _gla_scan(proj, low, gate_w2, gate_b, head_g, batch, seq, *, out_dtype):
    t = proj.shape[0]
    R = GLA_BLOCK
    steps = seq // R
    kw, vw = GLA_KEY_WIDTH, GLA_VALUE_WIDTH

    def row(b, n):
        return b * steps + n

    return pl.pallas_call(
        _gla_kernel,
        grid=(batch, steps),
        in_specs=[
            pl.BlockSpec((R, kw), lambda b, n: (row(b, n), 0)),
            pl.BlockSpec((R, kw), lambda b, n: (row(b, n), 1)),
            pl.BlockSpec((R, vw), lambda b, n: (row(b, n), 2 * kw // vw)),
            pl.BlockSpec((R, vw), lambda b, n: (row(b, n), 2 * kw // vw + 1)),
            pl.BlockSpec((R, LANES), lambda b, n: (row(b, n), 0)),
            pl.BlockSpec((LANES, kw), lambda b, n: (0, 0)),
            pl.BlockSpec((1, kw), lambda b, n: (0, 0)),
            pl.BlockSpec((1, GLA_HEAD_V), lambda b, n: (0, 0)),
        ],
        out_specs=pl.BlockSpec((R, vw), lambda b, n: (row(b, n), 0)),
        out_shape=jax.ShapeDtypeStruct((t, vw), out_dtype),
        scratch_shapes=[pltpu.VMEM((GLA_HEADS, GLA_HEAD_V, GLA_HEAD_K), F32)],
        compiler_params=_cparams(("arbitrary", "arbitrary")),
        name="gla_scan",
    )(proj, proj, proj, proj, low, gate_w2, gate_b, head_g)


def _ssd_kernel(z_ref, xr_ref, br_ref, cr_ref, dt_ref, cw_ref, cbias_ref, dtb_ref, alog_ref, dsk_ref, ng_ref, o_ref,
                st_scr, raw_scr, xs_ref, bm_ref, cm_ref):
    C = SSD_CHUNK
    P = SSM_HEADDIM
    pairs_per_group = (SSM_HEADS // SSM_GROUPS) // 2
    gw = SSM_WIDTH // SSM_GROUPS

    @pl.when(pl.program_id(1) == 0)
    def _():
        st_scr[...] = jnp.zeros_like(st_scr)
        raw_scr[...] = jnp.zeros_like(raw_scr)

    raw_scr[0:SUBLANES, :] = raw_scr[C:C + SUBLANES, :]
    col = 0
    for src_ref, act_ref in ((xr_ref, xs_ref), (br_ref, bm_ref), (cr_ref, cm_ref)):
        width = src_ref.shape[1]
        cols = slice(col, col + width)
        raw_scr[SUBLANES:, cols] = src_ref[...].astype(F32)
        acc = cbias_ref[:, cols]
        for s in range(SSM_CONV):
            tap = raw_scr[SUBLANES - s:SUBLANES - s + C, cols]
            acc = acc + tap * cw_ref[SSM_CONV - 1 - s:SSM_CONV - s, cols]
        act_ref[...] = _silu(acc)
        col += width

    dt = _softplus(dt_ref[...] + dtb_ref[...])
    da = dt * (-LOG2E * jnp.exp(alog_ref[...]))
    ri = _iota2((C, C), 0)
    ci = _iota2((C, C), 1)
    causal = ci <= ri
    tri = jnp.where(causal, 1.0, 0.0).astype(BF16)
    acum = _dot_exact_lhs(tri, da)
    acum_t = acum.T
    alast = acum[C - 1:C, :]
    dsk = dsk_ref[...]

    lane = _iota2((1, LANES), 1)
    m0 = lane < P
    mf0 = m0.astype(F32)
    mf1 = 1.0 - mf0

    def pair_cols(x, h0):
        return jnp.where(m0, x[:, h0:h0 + 1], x[:, h0 + 1:h0 + 2])

    for g in range(SSM_GROUPS):
        bm = bm_ref[:, g * SSM_STATE:(g + 1) * SSM_STATE].astype(F32)
        cm = cm_ref[:, g * SSM_STATE:(g + 1) * SSM_STATE].astype(F32)
        cb = _dot(cm, bm, nt=True)
        bm_t = bm.T
        ys = []
        for pp in range(pairs_per_group):
            pidx = g * pairs_per_group + pp
            h0 = 2 * pidx
            cols = slice(pidx * LANES, (pidx + 1) * LANES)
            x_p = xs_ref[:, cols].astype(F32)
            dt_p = pair_cols(dt, h0)
            ac_p = pair_cols(acum, h0)
            al_p = pair_cols(alast, h0)
            xc = x_p * dt_p
            dec0 = jnp.where(causal, jnp.exp2(acum[:, h0:h0 + 1] - acum_t[h0:h0 + 1, :]), 0.0)
            dec1 = jnp.where(causal, jnp.exp2(acum[:, h0 + 1:h0 + 2] - acum_t[h0 + 1:h0 + 2, :]), 0.0)
            lhs = jnp.concatenate([cb * dec0, cb * dec1], axis=1)
            rhs = jnp.concatenate([xc * mf0, xc * mf1], axis=0)
            prev = st_scr[pidx]
            y = _dot(lhs, rhs) + _dot(cm, prev) * jnp.exp2(ac_p) + pair_cols(dsk, h0) * x_p
            st_scr[pidx] = prev * jnp.exp2(al_p) + _dot(bm_t, xc * jnp.exp2(al_p - ac_p))
            ys.append(y * _silu(z_ref[:, cols].astype(F32)))
        yg = jnp.concatenate(ys, axis=1)
        ms = jnp.mean(yg * yg, axis=-1, keepdims=True)
        o_ref[:, g * gw:(g + 1) * gw] = (yg * lax.rsqrt(ms + SSM_NORM_EPS) * ng_ref[:, g * gw:(g + 1) * gw]).astype(o_ref.dtype)


def _ssd_scan(proj, dt, conv_w, conv_b, dt_bias, a_log, d_skip, norm_g, batch, seq, *, out_dtype):
    t = proj.shape[0]
    C = SSD_CHUNK
    steps = seq // C
    npairs = SSM_HEADS // 2

    def row(b, n):
        return b * steps + n

    bcol = 2 * SSM_WIDTH // SSM_BC_WIDTH
    return pl.pallas_call(
        _ssd_kernel,
        grid=(batch, steps),
        in_specs=[
            pl.BlockSpec((C, SSM_WIDTH), lambda b, n: (row(b, n), 0)),
            pl.BlockSpec((C, SSM_WIDTH), lambda b, n: (row(b, n), 1)),
            pl.BlockSpec((C, SSM_BC_WIDTH), lambda b, n: (row(b, n), bcol)),
            pl.BlockSpec((C, SSM_BC_WIDTH), lambda b, n: (row(b, n), bcol + 1)),
            pl.BlockSpec((C, LANES), lambda b, n: (row(b, n), 0)),
            pl.BlockSpec((SSM_CONV, SSM_CONV_WIDTH), lambda b, n: (0, 0)),
            pl.BlockSpec((1, SSM_CONV_WIDTH), lambda b, n: (0, 0)),
            pl.BlockSpec((1, LANES), lambda b, n: (0, 0)),
            pl.BlockSpec((1, LANES), lambda b, n: (0, 0)),
            pl.BlockSpec((1, LANES), lambda b, n: (0, 0)),
            pl.BlockSpec((1, SSM_WIDTH), lambda b, n: (0, 0)),
        ],
        out_specs=pl.BlockSpec((C, SSM_WIDTH), lambda b, n: (row(b, n), 0)),
        out_shape=jax.ShapeDtypeStruct((t, SSM_WIDTH), out_dtype),
        scratch_shapes=[pltpu.VMEM((npairs, SSM_STATE, LANES), F32), pltpu.VMEM((C + SUBLANES, SSM_CONV_WIDTH), F32),
                        pltpu.VMEM((C, SSM_WIDTH), F32), pltpu.VMEM((C, SSM_BC_WIDTH), F32),
                        pltpu.VMEM((C, SSM_BC_WIDTH), F32)],
        compiler_params=_cparams(("arbitrary", "arbitrary")),
        name="ssd_scan",
    )(proj, proj, proj, proj, dt, conv_w, conv_b, dt_bias, a_log, d_skip, norm_g)


def _pad_cols(w, n):
    return jnp.pad(w, ((0, 0), (0, n - w.shape[1])))


def _pad_rows(w, n):
    return jnp.pad(w, ((0, n - w.shape[0]), (0, 0)))


ACT_DTYPE = BF16


def kernel(x, c, ada_w, ada_b, norm_g, final_g, rwkv_mu, rwkv_w_in, rwkv_dec_w1, rwkv_dec_w2, rwkv_dec_w0, rwkv_iclr_w1, rwkv_iclr_w2, rwkv_iclr_w0, rwkv_k_k, rwkv_k_a, rwkv_r_k, rwkv_gn_w, rwkv_gn_b, rwkv_w_out, gla_w_in, gla_gate_w2, gla_gate_b, gla_head_g, gla_w_out, ssd_w_in, ssd_conv_w, ssd_conv_b, ssd_dt_bias, ssd_a_log, ssd_d, ssd_norm_g, ssd_w_out):
    batch, seq, d = x.shape
    t = batch * seq
    xf = x.reshape(t, d)

    c_pad = jnp.pad(c, ((0, SUBLANES - batch % SUBLANES if batch % SUBLANES else 0), (0, 0)))
    mod = _ada_mod(c_pad, ada_w, ada_b)[:, :batch].reshape(DEPTH, batch, 3, d)

    for i in range(DEPTH):
        kind, j = i % N_MIXERS, i // N_MIXERS
        g = norm_g[i].reshape(1, d)
        mod_l = mod[i]
        if kind == 0:
            w1 = jnp.concatenate([_pad_cols(rwkv_dec_w1[j], LANES), _pad_cols(rwkv_iclr_w1[j], LANES)], axis=1)
            xs, lora = _rwkv_prenorm(xf, g, mod_l, rwkv_mu[j], w1.astype(BF16), seq, tm=ROW_TILE)
            rkvg = _proj(xs, rwkv_w_in, j, 4 * d, seq, tm=PROJ_ROW_TILE, tn=1024, out_dtype=ACT_DTYPE)
            row = lambda v: v.reshape(1, -1)
            params = dict(dec_w2=_pad_rows(rwkv_dec_w2[j], LANES), iclr_w2=_pad_rows(rwkv_iclr_w2[j], LANES),
                          dec_w0=row(rwkv_dec_w0[j]), iclr_w0=row(rwkv_iclr_w0[j]), k_k=row(rwkv_k_k[j]),
                          k_a=row(rwkv_k_a[j]), r_k=row(rwkv_r_k[j]), gn_w=row(rwkv_gn_w[j]), gn_b=row(rwkv_gn_b[j]))
            z = _rwkv_scan(rkvg, lora, params, batch, seq, out_dtype=ACT_DTYPE)
            xf = _outproj(z, rwkv_w_out, j, xf, mod_l, seq, tm=PROJ_ROW_TILE, tn=1024)
        elif kind == 1:
            nmain = 2 * GLA_KEY_WIDTH + 2 * GLA_VALUE_WIDTH
            w_bf = gla_w_in[j:j + 1].astype(BF16)
            h, low = _prenorm_call(xf, g, mod_l, _pad_cols(w_bf[0, :, nmain:], LANES), seq, tm=ROW_TILE)
            proj = _proj(h[None], w_bf, 0, nmain, seq, tm=PROJ_ROW_TILE, tn=1024, out_dtype=ACT_DTYPE)
            z = _gla_scan(proj, low, _pad_rows(gla_gate_w2[j], LANES), gla_gate_b[j].reshape(1, -1),
                          gla_head_g[j].reshape(1, -1), batch, seq, out_dtype=ACT_DTYPE)
            xf = _outproj(z, gla_w_out, j, xf, mod_l, seq, tm=PROJ_ROW_TILE, tn=1024)
        else:
            nmain = SSM_WIDTH + SSM_CONV_WIDTH
            w_bf = ssd_w_in[j:j + 1].astype(BF16)
            h, dt = _prenorm_call(xf, g, mod_l, _pad_cols(w_bf[0, :, nmain:], LANES), seq, tm=ROW_TILE)
            proj = _proj(h[None], w_bf, 0, nmain, seq, tm=PROJ_ROW_TILE, tn=1024, out_dtype=ACT_DTYPE)
            padl = lambda v: _pad_cols(v.reshape(1, -1), LANES)
            z = _ssd_scan(proj, dt, ssd_conv_w[j], ssd_conv_b[j].reshape(1, -1),
                          padl(ssd_dt_bias[j]), padl(ssd_a_log[j]), padl(ssd_d[j]),
                          ssd_norm_g[j].reshape(1, -1), batch, seq, out_dtype=ACT_DTYPE)
            xf = _outproj(z, ssd_w_out, j, xf, mod_l, seq, tm=PROJ_ROW_TILE, tn=512)

    out = _final_norm(xf, final_g.reshape(1, d))
    return out.reshape(batch, seq, d)
```

```python
import functools
import math

import jax
import jax.numpy as jnp
from jax import lax
from jax.experimental import pallas as pl
from jax.experimental.pallas import tpu as pltpu

F32 = jnp.float32
BF16 = jnp.bfloat16

D_MODEL = 2048
DEPTH = 4
N_MIXERS = 3
NORM_EPS = 1e-6

RWKV_HEAD = 64
RWKV_LORA = 96
RWKV_GN_EPS = 64e-5

GLA_HEADS = 4
GLA_KEY_WIDTH = D_MODEL // 2
GLA_VALUE_WIDTH = D_MODEL
GLA_HEAD_K = GLA_KEY_WIDTH // GLA_HEADS
GLA_HEAD_V = GLA_VALUE_WIDTH // GLA_HEADS
GLA_GATE_RANK = 16
GLA_GATE_TAU = 16.0

SSM_WIDTH = 2 * D_MODEL
SSM_HEADDIM = 64
SSM_HEADS = SSM_WIDTH // SSM_HEADDIM
SSM_STATE = 128
SSM_GROUPS = 8
SSM_CONV = 4
SSM_NORM_EPS = 1e-5
SSM_BC_WIDTH = SSM_GROUPS * SSM_STATE
SSM_CONV_WIDTH = SSM_WIDTH + 2 * SSM_BC_WIDTH

LANES = 128
SUBLANES = 8
VMEM_LIMIT_BYTES = 52 * 1024 * 1024

RWKV_CHUNK = 64
RWKV_CHUNKS_PER_STEP = 2
RWKV_PAIRS_PER_STEP = 16
RWKV_CHUNK_GROUPS = 1
GLA_BLOCK = 128
SSD_CHUNK = 128

RWKV_PASSES = 1
GATE_PASSES = 3

LOG2E = math.log2(math.e)

ROW_TILE = 512
PROJ_ROW_TILE = 1024


def _cparams(sem):
    return pltpu.CompilerParams(dimension_semantics=sem, vmem_limit_bytes=VMEM_LIMIT_BYTES)


def _dot(a, b, *, nt=False, passes=1):
    dims = (((1,), (1,)), ((), ())) if nt else (((1,), (0,)), ((), ()))

    def d(x, y):
        return lax.dot_general(x, y, dims, preferred_element_type=F32)

    ah = a.astype(BF16)
    bh = b.astype(BF16)
    if passes == 1:
        return d(ah, bh)
    al = (a.astype(F32) - ah.astype(F32)).astype(BF16)
    bl = (b.astype(F32) - bh.astype(F32)).astype(BF16)
    return d(ah, bh) + (d(ah, bl) + d(al, bh))


def _dot_exact_lhs(m_bf16, x):
    x1 = x.astype(BF16)
    r1 = x - x1.astype(F32)
    x2 = r1.astype(BF16)
    x3 = (r1 - x2.astype(F32)).astype(BF16)

    def d(y):
        return jnp.dot(m_bf16, y, preferred_element_type=F32)

    return d(x1) + (d(x2) + d(x3))


def _silu(x):
    hx = 0.5 * x
    return hx + hx * jnp.tanh(hx)


def _softplus(x):
    return jnp.maximum(x, 0.0) + jnp.log(1.0 + jnp.exp(-jnp.abs(x)))


def _iota2(shape, dim):
    return lax.broadcasted_iota(jnp.int32, shape, dim)


def _ada_kernel(c_ref, w_ref, b_ref, o_ref):
    c = c_ref[...]
    o_ref[...] = _dot(_silu(c), w_ref[...]) + b_ref[...]


def _ada_mod(c_pad, ada_w, ada_b, tn=1024):
    depth, d, n = ada_w.shape
    rows = c_pad.shape[0]
    return pl.pallas_call(
        _ada_kernel,
        grid=(depth, n // tn),
        in_specs=[
            pl.BlockSpec((rows, d), lambda l, j: (0, 0)),
            pl.BlockSpec((None, d, tn), lambda l, j: (l, 0, j)),
            pl.BlockSpec((None, 1, tn), lambda l, j: (l, 0, j)),
        ],
        out_specs=pl.BlockSpec((None, rows, tn), lambda l, j: (l, 0, j)),
        out_shape=jax.ShapeDtypeStruct((depth, rows, n), F32),
        compiler_params=_cparams(("arbitrary", "arbitrary")),
        name="ada_mod",
    )(c_pad, ada_w, ada_b.reshape(depth, 1, n))


def _prenorm(x, g, mod):
    ms = jnp.mean(x * x, axis=-1, keepdims=True)
    return x * lax.rsqrt(ms + NORM_EPS) * g * (1.0 + mod[1:2, :]) + mod[0:1, :]


def _prenorm_kernel(x_ref, g_ref, mod_ref, ws_ref, h_ref, os_ref):
    h = _prenorm(x_ref[...], g_ref[...], mod_ref[...]).astype(BF16)
    h_ref[...] = h
    os_ref[...] = jnp.dot(h, ws_ref[...], preferred_element_type=F32)


def _prenorm_call(x, g, mod_l, w_side, seq, *, tm):
    t, d = x.shape
    ns = w_side.shape[1]
    tm = min(tm, seq)
    tiles_per_seq = seq // tm
    return pl.pallas_call(
        _prenorm_kernel,
        grid=(t // tm,),
        in_specs=[
            pl.BlockSpec((tm, d), lambda i: (i, 0)),
            pl.BlockSpec((1, d), lambda i: (0, 0)),
            pl.BlockSpec((None, 3, d), lambda i: (i // tiles_per_seq, 0, 0)),
            pl.BlockSpec((d, ns), lambda i: (0, 0)),
        ],
        out_specs=[pl.BlockSpec((tm, d), lambda i: (i, 0)), pl.BlockSpec((tm, ns), lambda i: (i, 0))],
        out_shape=[jax.ShapeDtypeStruct((t, d), BF16), jax.ShapeDtypeStruct((t, ns), F32)],
        compiler_params=_cparams(("arbitrary",)),
        name="prenorm",
    )(x, g, mod_l, w_side)


def _proj_kernel(a_ref, w_ref, o_ref, wb_scr):
    @pl.when(pl.program_id(1) == 0)
    def _():
        wb_scr[...] = w_ref[...].astype(BF16)

    o_ref[...] = jnp.dot(a_ref[...], wb_scr[...], preferred_element_type=F32).astype(o_ref.dtype)


def _proj(a, w, layer, n, seq, *, tm, tn, out_dtype):
    groups, t, kd = a.shape
    tm = min(tm, seq)
    tiles_per_group = (n // tn) // groups
    return pl.pallas_call(
        _proj_kernel,
        grid=(n // tn, t // tm),
        in_specs=[
            pl.BlockSpec((None, tm, kd), lambda j, i: (j // tiles_per_group, i, 0)),
            pl.BlockSpec((None, kd, tn), lambda j, i: (layer, 0, j)),
        ],
        out_specs=pl.BlockSpec((tm, tn), lambda j, i: (i, j)),
        out_shape=jax.ShapeDtypeStruct((t, n), out_dtype),
        scratch_shapes=[pltpu.VMEM((kd, tn), BF16)],
        compiler_params=_cparams(("arbitrary", "arbitrary")),
        name="proj",
    )(a, w)


def _outproj_kernel(z_ref, w_ref, x_ref, mod_ref, o_ref, wb_scr):
    @pl.when(pl.program_id(1) == 0)
    def _():
        wb_scr[...] = w_ref[...].astype(BF16)

    acc = jnp.dot(z_ref[...], wb_scr[...], preferred_element_type=F32)
    o_ref[...] = x_ref[...] + mod_ref[2:3, :] * acc


def _outproj(z, w, layer, x, mod_l, seq, *, tm, tn):
    t, kd = z.shape
    n = w.shape[2]
    tm = min(tm, seq)
    tiles_per_seq = seq // tm
    return pl.pallas_call(
        _outproj_kernel,
        grid=(n // tn, t // tm),
        in_specs=[
            pl.BlockSpec((tm, kd), lambda j, i: (i, 0)),
            pl.BlockSpec((None, kd, tn), lambda j, i: (layer, 0, j)),
            pl.BlockSpec((tm, tn), lambda j, i: (i, j)),
            pl.BlockSpec((None, 3, tn), lambda j, i: (i // tiles_per_seq, 0, j)),
        ],
        out_specs=pl.BlockSpec((tm, tn), lambda j, i: (i, j)),
        out_shape=jax.ShapeDtypeStruct((t, n), F32),
        scratch_shapes=[pltpu.VMEM((kd, tn), BF16)],
        compiler_params=_cparams(("arbitrary", "arbitrary")),
        name="outproj",
    )(z, w, x, mod_l)


def _final_norm_kernel(x_ref, g_ref, o_ref):
    x = x_ref[...]
    ms = jnp.mean(x * x, axis=-1, keepdims=True)
    o_ref[...] = x * lax.rsqrt(ms + NORM_EPS) * g_ref[...]


def _final_norm(x, g, *, tm=512):
    t, d = x.shape
    tm = min(tm, t)
    return pl.pallas_call(
        _final_norm_kernel,
        grid=(t // tm,),
        in_specs=[pl.BlockSpec((tm, d), lambda i: (i, 0)), pl.BlockSpec((1, d), lambda i: (0, 0))],
        out_specs=pl.BlockSpec((tm, d), lambda i: (i, 0)),
        out_shape=jax.ShapeDtypeStruct((t, d), F32),
        compiler_params=_cparams(("arbitrary",)),
        name="final_norm",
    )(x, g)


def _rwkv_prenorm_kernel(x_ref, xh_ref, g_ref, mod_ref, mu_ref, w1_ref, xs_ref, lora_ref, *, tiles_per_seq):
    i = pl.program_id(0)
    g = g_ref[...]
    mod = mod_ref[...]
    h = _prenorm(x_ref[...], g, mod)
    hp8 = _prenorm(xh_ref[...], g, mod)
    first = (i % tiles_per_seq) == 0
    prev_row = jnp.where(first, 0.0, hp8[SUBLANES - 1:SUBLANES, :])
    rolled = pltpu.roll(h, 1, 0)
    row0 = _iota2(h.shape, 0) == 0
    dh = jnp.where(row0, prev_row, rolled) - h
    mu = mu_ref[...]
    for c in range(4):
        xs_ref[c] = (h + dh * mu[c:c + 1, :]).astype(BF16)
    xw = (h + dh * mu[4:5, :]).astype(BF16)
    xa = (h + dh * mu[5:6, :]).astype(BF16)
    w1 = w1_ref[...]
    dec_h = jnp.tanh(jnp.dot(xw, w1[:, :LANES], preferred_element_type=F32))
    icl_h = jnp.dot(xa, w1[:, LANES:], preferred_element_type=F32)
    lora_ref[...] = jnp.concatenate([dec_h, icl_h], axis=1)


def _rwkv_prenorm(x, g, mod_l, mu, w1, seq, *, tm):
    t, d = x.shape
    tm = min(tm, seq)
    tiles_per_seq = seq // tm
    halo_blocks = tm // SUBLANES
    kern = functools.partial(_rwkv_prenorm_kernel, tiles_per_seq=tiles_per_seq)
    return pl.pallas_call(
        kern,
        grid=(t // tm,),
        in_specs=[
            pl.BlockSpec((tm, d), lambda i: (i, 0)),
            pl.BlockSpec((SUBLANES, d), lambda i: (jnp.maximum(i * halo_blocks - 1, 0), 0)),
            pl.BlockSpec((1, d), lambda i: (0, 0)),
            pl.BlockSpec((None, 3, d), lambda i: (i // tiles_per_seq, 0, 0)),
            pl.BlockSpec((6, d), lambda i: (0, 0)),
            pl.BlockSpec((d, 2 * LANES), lambda i: (0, 0)),
        ],
        out_specs=[
            pl.BlockSpec((4, tm, d), lambda i: (0, i, 0)),
            pl.BlockSpec((tm, 2 * LANES), lambda i: (i, 0)),
        ],
        out_shape=[
            jax.ShapeDtypeStruct((4, t, d), BF16),
            jax.ShapeDtypeStruct((t, 2 * LANES), F32),
        ],
        compiler_params=_cparams(("arbitrary",)),
        name="rwkv_prenorm",
    )(x, x, g, mod_l, mu, w1)


def _interleave(*gens):
    live = list(gens)
    while live:
        for gen in list(live):
            try:
                next(gen)
            except StopIteration:
                live.remove(gen)


def _unit_lower_inverse_stages(a_list, eye, blk8, offdiag, passes, out):
    n = eye.shape[0]
    eye_b = eye.astype(BF16)
    a8 = [a * blk8 for a in a_list]
    a2 = [_dot(x, x, passes=passes).astype(BF16) for x in a8]
    yield
    ia8 = [eye_b + x for x in a8]
    both = [_dot(jnp.concatenate([y, x], axis=0), y, passes=passes) for x, y in zip(ia8, a2)]
    yield
    inv = [_dot(x + b[n:], eye + b[:n], passes=passes) for x, b in zip(ia8, both)]
    yield
    for m in offdiag:
        t = [_dot(a * m, i, passes=passes) for a, i in zip(a_list, inv)]
        yield
        inv = [i + _dot(i, x, passes=passes) for i, x in zip(inv, t)]
        yield
    out.extend(inv)


def _rwkv_scan_kernel(r_ref, k_ref, v_ref, g_ref, lora_ref, dw2_ref, iw2_ref, dw0_ref, iw0_ref,
                      kk_ref, ka_ref, rk_ref, gnw_ref, gnb_ref, z_ref, s_scr, cum_scr, y_scr,
                      *, chunks, pairs, groups, passes):
    L = RWKV_CHUNK
    N = RWKV_HEAD
    R = L * chunks
    P = 2 * L

    @pl.when(pl.program_id(2) == 0)
    def _():
        s_scr[...] = jnp.zeros_like(s_scr)

    lane = _iota2((1, LANES), 1)
    m0 = lane < N
    mf0 = m0.astype(F32)
    mf1 = 1.0 - mf0

    def headsum(x):
        s0 = jnp.sum(jnp.where(m0, x, 0.0), axis=-1, keepdims=True)
        s1 = jnp.sum(jnp.where(m0, 0.0, x), axis=-1, keepdims=True)
        return jnp.where(m0, s0, s1)

    mb0 = mf0.astype(BF16)
    mb1 = mf1.astype(BF16)

    def stack(x):
        xb = x.astype(BF16)
        return jnp.concatenate([xb * mb0, xb * mb1], axis=0)

    lora = lora_ref[...]
    dec = dw0_ref[...] + _dot(lora[:, :LANES], dw2_ref[...], passes=passes)
    lw_all = -jnp.exp(-_softplus(-dec) - 0.5) * LOG2E
    a_all = jax.nn.sigmoid(iw0_ref[...] + _dot(lora[:, LANES:], iw2_ref[...], passes=passes))
    tri = jnp.where(_iota2((L, L), 1) <= _iota2((L, L), 0), 1.0, 0.0).astype(BF16)
    cum_all = jnp.concatenate([_dot_exact_lhs(tri, lw_all[c * L:(c + 1) * L]) for c in range(chunks)], axis=0)
    cum_scr[...] = cum_all

    pair_vals = []
    for p in range(pairs):
        cols = slice(p * LANES, (p + 1) * LANES)
        r = r_ref[:, cols].astype(F32)
        k = k_ref[:, cols].astype(F32)
        v = v_ref[:, cols].astype(F32)
        a = a_all[:, cols]
        kkr = k * kk_ref[:, cols]
        kk = kkr / jnp.maximum(jnp.sqrt(headsum(kkr * kkr)), 1e-12)
        k2 = k * (1.0 + (a - 1.0) * ka_ref[:, cols])
        cum = cum_all[:, cols]
        pair_vals.append(dict(r=r, k2=k2, v=v, av=-kk, bv=kk * a, cum=cum, cumex=cum - lw_all[:, cols]))

    rp = _iota2((P, P), 0)
    cp = _iota2((P, P), 1)
    strict = (rp % L) > (cp % L)
    incl = (rp % L) >= (cp % L)
    eye = jnp.where(rp == cp, 1.0, 0.0)
    blk8 = jnp.where((rp // 8) == (cp // 8), 1.0, 0.0).astype(BF16)
    offdiag = [jnp.where(((rp // (2 * b)) == (cp // (2 * b))) & ((rp // b) != (cp // b)), 1.0, 0.0).astype(BF16)
               for b in (8, 16, 32)]
    fac = {}

    def factor_stages(items):
        pre = []
        for p, c in items:
            pv = pair_vals[p]
            cols = slice(p * LANES, (p + 1) * LANES)
            sl = slice(c * L, (c + 1) * L)
            cref = cum_scr[pl.ds(c * L + L // 2 - 1, 1), cols]
            clast = cum_scr[pl.ds(c * L + L - 1, 1), cols]
            cum_c, cumex_c = pv["cum"][sl], pv["cumex"][sl]
            r_c, k_c, v_c, av_c, bv_c = pv["r"][sl], pv["k2"][sl], pv["v"][sl], pv["av"][sl], pv["bv"][sl]
            e_out = jnp.exp2(cref - cum_c)
            e_end = jnp.exp2(clast - cum_c)
            pre.append(dict(
                lhs1=jnp.concatenate([stack(av_c * jnp.exp2(cumex_c - cref)), stack(r_c * jnp.exp2(cum_c - cref))],
                                     axis=0),
                rhs1=jnp.concatenate([stack(bv_c * e_out), stack(k_c * e_out)], axis=0),
                v_st=stack(v_c), a0_st=stack(av_c * jnp.exp2(cumex_c)), r0_st=stack(r_c * jnp.exp2(cum_c)),
                bk=jnp.concatenate([stack(bv_c * e_end), stack(k_c * e_end)], axis=0), decay=jnp.exp2(clast)))
        yield
        x1 = [_dot(f["lhs1"], f["rhs1"], nt=True, passes=passes) for f in pre]
        yield
        a_ab = [jnp.where(strict, x[:P, :P], 0.0).astype(BF16) for x in x1]
        a_kr = [jnp.concatenate([jnp.where(strict, x[:P, P:], 0.0), jnp.where(incl, x[P:, P:], 0.0)], axis=0)
                for x in x1]
        a_rb = [jnp.where(incl, x[P:, :P], 0.0) for x in x1]
        m1 = [_dot(a, f["v_st"], passes=passes) for a, f in zip(a_kr, pre)]
        yield
        tinv = []
        yield from _unit_lower_inverse_stages(a_ab, eye, blk8, offdiag, passes, tinv)
        m2 = [_dot(t, jnp.concatenate([f["a0_st"], m[:P].astype(BF16)], axis=1), passes=passes)
              for t, f, m in zip(tinv, pre, m1)]
        for it, f, m, mm, arb in zip(items, pre, m1, m2, a_rb):
            w_st, uv_st = mm[:, :LANES], mm[:, LANES:]
            fac[it] = dict(wr=jnp.concatenate([w_st.astype(BF16), f["r0_st"]], axis=0), uv_st=uv_st,
                           yv_st=m[P:], a_rb=arb, v_t=f["v_st"].astype(F32).T, decay=f["decay"], bk=f["bk"])

    state = [s_scr[p] for p in range(pairs)]

    def state_stages(chunk_ids):
        for c in chunk_ids:
            fs = [fac[(p, c)] for p in range(pairs)]
            m3 = [_dot(fs[p]["wr"], state[p], nt=True, passes=passes) for p in range(pairs)]
            yield
            u_st = [m3[p][:P] + fs[p]["uv_st"] for p in range(pairs)]
            for p in range(pairs):
                state[p] = state[p] * fs[p]["decay"] + _dot(
                    jnp.concatenate([u_st[p].T, fs[p]["v_t"]], axis=1), fs[p]["bk"], passes=passes)
            yield
            for p in range(pairs):
                y_st = m3[p][P:] + fs[p]["yv_st"] + _dot(fs[p]["a_rb"], u_st[p], passes=passes)
                y_scr[c * L:(c + 1) * L, p * LANES:(p + 1) * LANES] = y_st[:L] + y_st[L:]
            yield

    per_group = chunks // groups
    chunk_groups = [list(range(gi * per_group, (gi + 1) * per_group)) for gi in range(groups)]
    group_items = [[(p, c) for p in range(pairs) for c in cg] for cg in chunk_groups]
    _interleave(factor_stages(group_items[0]))
    for gi in range(1, groups):
        _interleave(factor_stages(group_items[gi]), state_stages(chunk_groups[gi - 1]))
    _interleave(state_stages(chunk_groups[-1]))
    for p in range(pairs):
        s_scr[p] = state[p]

    for p in range(pairs):
        cols = slice(p * LANES, (p + 1) * LANES)
        pv = pair_vals[p]
        y = y_scr[:, cols]
        mean = headsum(y) * (1.0 / N)
        yc = y - mean
        var = headsum(yc * yc) * (1.0 / N)
        yn = yc * lax.rsqrt(var + RWKV_GN_EPS) * gnw_ref[:, cols] + gnb_ref[:, cols]
        bonus = headsum(pv["r"] * pv["k2"] * rk_ref[:, cols]) * pv["v"]
        z_ref[:, cols] = ((yn + bonus) * _silu(g_ref[:, cols].astype(F32))).astype(z_ref.dtype)


def _rwkv_scan(rkvg, lora, p, batch, seq, *, out_dtype):
    t = rkvg.shape[0]
    w = D_MODEL
    pairs = RWKV_PAIRS_PER_STEP
    bw = pairs * LANES
    nblk = w // bw
    chunks = min(RWKV_CHUNKS_PER_STEP, seq // RWKV_CHUNK)
    rows = RWKV_CHUNK * chunks
    steps = seq // rows

    def act(col0):
        return pl.BlockSpec((rows, bw), lambda b, h, n: (b * steps + n, col0 + h))

    def vec():
        return pl.BlockSpec((1, bw), lambda b, h, n: (0, h))

    def w2():
        return pl.BlockSpec((LANES, bw), lambda b, h, n: (0, h))

    kern = functools.partial(_rwkv_scan_kernel, chunks=chunks, pairs=pairs, groups=min(RWKV_CHUNK_GROUPS, chunks),
                             passes=RWKV_PASSES)
    return pl.pallas_call(
        kern,
        grid=(batch, nblk, steps),
        in_specs=[act(0), act(nblk), act(2 * nblk), act(3 * nblk),
                  pl.BlockSpec((rows, 2 * LANES), lambda b, h, n: (b * steps + n, 0)),
                  w2(), w2(), vec(), vec(), vec(), vec(), vec(), vec(), vec()],
        out_specs=pl.BlockSpec((rows, bw), lambda b, h, n: (b * steps + n, h)),
        out_shape=jax.ShapeDtypeStruct((t, w), out_dtype),
        scratch_shapes=[pltpu.VMEM((pairs, LANES, LANES), F32), pltpu.VMEM((rows, bw), F32),
                        pltpu.VMEM((rows, bw), F32)],
        compiler_params=_cparams(("arbitrary", "arbitrary", "arbitrary")),
        name="rwkv_scan",
    )(rkvg, rkvg, rkvg, rkvg, lora, p["dec_w2"], p["iclr_w2"], p["dec_w0"], p["iclr_w0"],
      p["k_k"], p["k_a"], p["r_k"], p["gn_w"], p["gn_b"])


def _gla_kernel(q_ref, k_ref, v_ref, g_ref, low_ref, w2_ref, b_ref, hg_ref, o_ref, s_scr):
    R = GLA_BLOCK
    Hf = R // 2

    @pl.when(pl.program_id(1) == 0)
    def _():
        s_scr[...] = jnp.zeros_like(s_scr)

    DK, DV = GLA_HEAD_K, GLA_HEAD_V
    heads = range(GLA_HEADS)
    ri = _iota2((R, R), 0)
    ci = _iota2((R, R), 1)
    tri = jnp.where(ci <= ri, 1.0, 0.0).astype(BF16)
    rh = _iota2((Hf, Hf), 0)
    ch = _iota2((Hf, Hf), 1)
    causal = ch <= rh

    la = -_softplus(-(_dot(low_ref[...], w2_ref[...], passes=GATE_PASSES) + b_ref[...])) * (LOG2E / GLA_GATE_TAU)
    bcum = _dot_exact_lhs(tri, la)
    q = [q_ref[:, h * DK:(h + 1) * DK].astype(F32) * (DK ** -0.5) for h in heads]
    k = [k_ref[:, h * DK:(h + 1) * DK].astype(F32) for h in heads]
    v = [v_ref[:, h * DV:(h + 1) * DV].astype(F32) for h in heads]
    bc = [bcum[:, h * DK:(h + 1) * DK] for h in heads]
    ref_t = [b[Hf // 2 - 1:Hf // 2] for b in bc]
    ref_m = [b[Hf - 1:Hf] for b in bc]
    ref_b = [b[Hf + Hf // 2 - 1:Hf + Hf // 2] for b in bc]
    last = [b[R - 1:R] for b in bc]
    s00 = [jnp.where(causal, _dot(q[h][:Hf] * jnp.exp2(bc[h][:Hf] - ref_t[h]),
                                  k[h][:Hf] * jnp.exp2(ref_t[h] - bc[h][:Hf]), nt=True), 0.0) for h in heads]
    s11 = [jnp.where(causal, _dot(q[h][Hf:] * jnp.exp2(bc[h][Hf:] - ref_b[h]),
                                  k[h][Hf:] * jnp.exp2(ref_b[h] - bc[h][Hf:]), nt=True), 0.0) for h in heads]
    s10 = [_dot(q[h][Hf:] * jnp.exp2(bc[h][Hf:] - ref_m[h]), k[h][:Hf] * jnp.exp2(ref_m[h] - bc[h][:Hf]), nt=True)
           for h in heads]
    st = [s_scr[h] for h in heads]
    o_int = [_dot(q[h] * jnp.exp2(bc[h]), st[h], nt=True) for h in heads]
    o_top = [_dot(s00[h], v[h][:Hf]) for h in heads]
    o_bot = [_dot(s10[h], v[h][:Hf]) + _dot(s11[h], v[h][Hf:]) for h in heads]
    upd = [_dot(v[h].T, k[h] * jnp.exp2(last[h] - bc[h])) for h in heads]
    for h in heads:
        s_scr[h] = st[h] * jnp.exp2(last[h]) + upd[h]
        o = jnp.concatenate([o_top[h], o_bot[h]], axis=0) + o_int[h]
        ms = jnp.mean(o * o, axis=-1, keepdims=True)
        on = o * lax.rsqrt(ms + NORM_EPS) * hg_ref[...]
        cols = slice(h * DV, (h + 1) * DV)
        o_ref[:, cols] = (on * _silu(g_ref[:, cols].astype(F32))).astype(o_ref.dtype)


def _gla_scan(proj, low, gate_w2, gate_b, head_g, batch, seq, *, out_dtype):
    t = proj.shape[0]
    R = GLA_BLOCK
    steps = seq // R
    kw, vw = GLA_KEY_WIDTH, GLA_VALUE_WIDTH

    def row(b, n):
        return b * steps + n

    return pl.pallas_call(
        _gla_kernel,
        grid=(batch, steps),
        in_specs=[
            pl.BlockSpec((R, kw), lambda b, n: (row(b, n), 0)),
            pl.BlockSpec((R, kw), lambda b, n: (row(b, n), 1)),
            pl.BlockSpec((R, vw), lambda b, n: (row(b, n), 2 * kw // vw)),
            pl.BlockSpec((R, vw), lambda b, n: (row(b, n), 2 * kw // vw + 1)),
            pl.BlockSpec((R, LANES), lambda b, n: (row(b, n), 0)),
            pl.BlockSpec((LANES, kw), lambda b, n: (0, 0)),
            pl.BlockSpec((1, kw), lambda b, n: (0, 0)),
            pl.BlockSpec((1, GLA_HEAD_V), lambda b, n: (0, 0)),
        ],
        out_specs=pl.BlockSpec((R, vw), lambda b, n: (row(b, n), 0)),
        out_shape=jax.ShapeDtypeStruct((t, vw), out_dtype),
        scratch_shapes=[pltpu.VMEM((GLA_HEADS, GLA_HEAD_V, GLA_HEAD_K), F32)],
        compiler_params=_cparams(("arbitrary", "arbitrary")),
        name="gla_scan",
    )(proj, proj, proj, proj, low, gate_w2, gate_b, head_g)


def _ssd_kernel(z_ref, xr_ref, br_ref, cr_ref, dt_ref, cw_ref, cbias_ref, dtb_ref, alog_ref, dsk_ref, ng_ref, o_ref,
                st_scr, raw_scr, xs_ref, bm_ref, cm_ref):
    C = SSD_CHUNK
    P = SSM_HEADDIM
    pairs_per_group = (SSM_HEADS // SSM_GROUPS) // 2
    gw = SSM_WIDTH // SSM_GROUPS

    @pl.when(pl.program_id(1) == 0)
    def _():
        st_scr[...] = jnp.zeros_like(st_scr)
        raw_scr[...] = jnp.zeros_like(raw_scr)

    raw_scr[0:SUBLANES, :] = raw_scr[C:C + SUBLANES, :]
    col = 0
    for src_ref, act_ref in ((xr_ref, xs_ref), (br_ref, bm_ref), (cr_ref, cm_ref)):
        width = src_ref.shape[1]
        cols = slice(col, col + width)
        raw_scr[SUBLANES:, cols] = src_ref[...].astype(F32)
        acc = cbias_ref[:, cols]
        for s in range(SSM_CONV):
            tap = raw_scr[SUBLANES - s:SUBLANES - s + C, cols]
            acc = acc + tap * cw_ref[SSM_CONV - 1 - s:SSM_CONV - s, cols]
        act_ref[...] = _silu(acc)
        col += width

    dt = _softplus(dt_ref[...] + dtb_ref[...])
    da = dt * (-LOG2E * jnp.exp(alog_ref[...]))
    ri = _iota2((C, C), 0)
    ci = _iota2((C, C), 1)
    causal = ci <= ri
    tri = jnp.where(causal, 1.0, 0.0).astype(BF16)
    acum = _dot_exact_lhs(tri, da)
    acum_t = acum.T
    alast = acum[C - 1:C, :]
    dsk = dsk_ref[...]

    lane = _iota2((1, LANES), 1)
    m0 = lane < P

    def pair_cols(x, h0):
        return jnp.where(m0, x[:, h0:h0 + 1], x[:, h0 + 1:h0 + 2])

    for g in range(SSM_GROUPS):
        bm = bm_ref[:, g * SSM_STATE:(g + 1) * SSM_STATE].astype(F32)
        cm = cm_ref[:, g * SSM_STATE:(g + 1) * SSM_STATE].astype(F32)
        cb = _dot(cm, bm, nt=True)
        bm_t = bm.T
        ys = []
        for pp in range(pairs_per_group):
            pidx = g * pairs_per_group + pp
            h0 = 2 * pidx
            cols = slice(pidx * LANES, (pidx + 1) * LANES)
            x_p = xs_ref[:, cols].astype(F32)
            dt_p = pair_cols(dt, h0)
            ac_p = pair_cols(acum, h0)
            al_p = pair_cols(alast, h0)
            xc = x_p * dt_p
            dec0 = jnp.where(causal, jnp.exp2(acum[:, h0:h0 + 1] - acum_t[h0:h0 + 1, :]), 0.0)
            dec1 = jnp.where(causal, jnp.exp2(acum[:, h0 + 1:h0 + 2] - acum_t[h0 + 1:h0 + 2, :]), 0.0)
            y_diag = jnp.where(m0, _dot(cb * dec0, xc), _dot(cb * dec1, xc))
            prev = st_scr[pidx]
            y = y_diag + _dot(cm, prev) * jnp.exp2(ac_p) + pair_cols(dsk, h0) * x_p
            st_scr[pidx] = prev * jnp.exp2(al_p) + _dot(bm_t, xc * jnp.exp2(al_p - ac_p))
            ys.append(y * _silu(z_ref[:, cols].astype(F32)))
        yg = jnp.concatenate(ys, axis=1)
        ms = jnp.mean(yg * yg, axis=-1, keepdims=True)
        o_ref[:, g * gw:(g + 1) * gw] = (yg * lax.rsqrt(ms + SSM_NORM_EPS) * ng_ref[:, g * gw:(g + 1) * gw]).astype(o_ref.dtype)


def _ssd_scan(proj, dt, conv_w, conv_b, dt_bias, a_log, d_skip, norm_g, batch, seq, *, out_dtype):
    t = proj.shape[0]
    C = SSD_CHUNK
    steps = seq // C
    npairs = SSM_HEADS // 2

    def row(b, n):
        return b * steps + n

    bcol = 2 * SSM_WIDTH // SSM_BC_WIDTH
    return pl.pallas_call(
        _ssd_kernel,
        grid=(batch, steps),
        in_specs=[
            pl.BlockSpec((C, SSM_WIDTH), lambda b, n: (row(b, n), 0)),
            pl.BlockSpec((C, SSM_WIDTH), lambda b, n: (row(b, n), 1)),
            pl.BlockSpec((C, SSM_BC_WIDTH), lambda b, n: (row(b, n), bcol)),
            pl.BlockSpec((C, SSM_BC_WIDTH), lambda b, n: (row(b, n), bcol + 1)),
            pl.BlockSpec((C, LANES), lambda b, n: (row(b, n), 0)),
            pl.BlockSpec((SSM_CONV, SSM_CONV_WIDTH), lambda b, n: (0, 0)),
            pl.BlockSpec((1, SSM_CONV_WIDTH), lambda b, n: (0, 0)),
            pl.BlockSpec((1, LANES), lambda b, n: (0, 0)),
            pl.BlockSpec((1, LANES), lambda b, n: (0, 0)),
            pl.BlockSpec((1, LANES), lambda b, n: (0, 0)),
            pl.BlockSpec((1, SSM_WIDTH), lambda b, n: (0, 0)),
        ],
        out_specs=pl.BlockSpec((C, SSM_WIDTH), lambda b, n: (row(b, n), 0)),
        out_shape=jax.ShapeDtypeStruct((t, SSM_WIDTH), out_dtype),
        scratch_shapes=[pltpu.VMEM((npairs, SSM_STATE, LANES), F32), pltpu.VMEM((C + SUBLANES, SSM_CONV_WIDTH), F32),
                        pltpu.VMEM((C, SSM_WIDTH), F32), pltpu.VMEM((C, SSM_BC_WIDTH), F32),
                        pltpu.VMEM((C, SSM_BC_WIDTH), F32)],
        compiler_params=_cparams(("arbitrary", "arbitrary")),
        name="ssd_scan",
    )(proj, proj, proj, proj, dt, conv_w, conv_b, dt_bias, a_log, d_skip, norm_g)


def _pad_cols(w, n):
    return jnp.pad(w, ((0, 0), (0, n - w.shape[1])))


def _pad_rows(w, n):
    return jnp.pad(w, ((0, n - w.shape[0]), (0, 0)))


ACT_DTYPE = BF16


def kernel(x, c, ada_w, ada_b, norm_g, final_g, rwkv_mu, rwkv_w_in, rwkv_dec_w1, rwkv_dec_w2, rwkv_dec_w0, rwkv_iclr_w1, rwkv_iclr_w2, rwkv_iclr_w0, rwkv_k_k, rwkv_k_a, rwkv_r_k, rwkv_gn_w, rwkv_gn_b, rwkv_w_out, gla_w_in, gla_gate_w2, gla_gate_b, gla_head_g, gla_w_out, ssd_w_in, ssd_conv_w, ssd_conv_b, ssd_dt_bias, ssd_a_log, ssd_d, ssd_norm_g, ssd_w_out):
    batch, seq, d = x.shape
    t = batch * seq
    xf = x.reshape(t, d)

    c_pad = jnp.pad(c, ((0, SUBLANES - batch % SUBLANES if batch % SUBLANES else 0), (0, 0)))
    mod = _ada_mod(c_pad, ada_w, ada_b)[:, :batch].reshape(DEPTH, batch, 3, d)

    for i in range(DEPTH):
        kind, j = i % N_MIXERS, i // N_MIXERS
        g = norm_g[i].reshape(1, d)
        mod_l = mod[i]
        if kind == 0:
            w1 = jnp.concatenate([_pad_cols(rwkv_dec_w1[j], LANES), _pad_cols(rwkv_iclr_w1[j], LANES)], axis=1)
            xs, lora = _rwkv_prenorm(xf, g, mod_l, rwkv_mu[j], w1.astype(BF16), seq, tm=ROW_TILE)
            rkvg = _proj(xs, rwkv_w_in, j, 4 * d, seq, tm=PROJ_ROW_TILE, tn=1024, out_dtype=ACT_DTYPE)
            row = lambda v: v.reshape(1, -1)
            params = dict(dec_w2=_pad_rows(rwkv_dec_w2[j], LANES), iclr_w2=_pad_rows(rwkv_iclr_w2[j], LANES),
                          dec_w0=row(rwkv_dec_w0[j]), iclr_w0=row(rwkv_iclr_w0[j]), k_k=row(rwkv_k_k[j]),
                          k_a=row(rwkv_k_a[j]), r_k=row(rwkv_r_k[j]), gn_w=row(rwkv_gn_w[j]), gn_b=row(rwkv_gn_b[j]))
            z = _rwkv_scan(rkvg, lora, params, batch, seq, out_dtype=ACT_DTYPE)
            xf = _outproj(z, rwkv_w_out, j, xf, mod_l, seq, tm=PROJ_ROW_TILE, tn=1024)
        elif kind == 1:
            nmain = 2 * GLA_KEY_WIDTH + 2 * GLA_VALUE_WIDTH
            w_bf = gla_w_in[j:j + 1].astype(BF16)
            h, low = _prenorm_call(xf, g, mod_l, _pad_cols(w_bf[0, :, nmain:], LANES), seq, tm=ROW_TILE)
            proj = _proj(h[None], w_bf, 0, nmain, seq, tm=PROJ_ROW_TILE, tn=1024, out_dtype=ACT_DTYPE)
            z = _gla_scan(proj, low, _pad_rows(gla_gate_w2[j], LANES), gla_gate_b[j].reshape(1, -1),
                          gla_head_g[j].reshape(1, -1), batch, seq, out_dtype=ACT_DTYPE)
            xf = _outproj(z, gla_w_out, j, xf, mod_l, seq, tm=PROJ_ROW_TILE, tn=1024)
        else:
            nmain = SSM_WIDTH + SSM_CONV_WIDTH
            w_bf = ssd_w_in[j:j + 1].astype(BF16)
            h, dt = _prenorm_call(xf, g, mod_l, _pad_cols(w_bf[0, :, nmain:], LANES), seq, tm=ROW_TILE)
            proj = _proj(h[None], w_bf, 0, nmain, seq, tm=PROJ_ROW_TILE, tn=1024, out_dtype=ACT_DTYPE)
            padl = lambda v: _pad_cols(v.reshape(1, -1), LANES)
            z = _ssd_scan(proj, dt, ssd_conv_w[j], ssd_conv_b[j].reshape(1, -1),
                          padl(ssd_dt_bias[j]), padl(ssd_a_log[j]), padl(ssd_d[j]),
                          ssd_norm_g[j].reshape(1, -1), batch, seq, out_dtype=ACT_DTYPE)
            xf = _outproj(z, ssd_w_out, j, xf, mod_l, seq, tm=PROJ_ROW_TILE, tn=512)

    out = _final_norm(xf, final_g.reshape(1, d))
    return out.reshape(batch, seq, d)
```

```python
import functools
import math

import jax
import jax.numpy as jnp
from jax import lax
from jax.experimental import pallas as pl
from jax.experimental.pallas import tpu as pltpu

F32 = jnp.float32
BF16 = jnp.bfloat16

D_MODEL = 2048
DEPTH = 4
N_MIXERS = 3
NORM_EPS = 1e-6

RWKV_HEAD = 64
RWKV_LORA = 96
RWKV_GN_EPS = 64e-5

GLA_HEADS = 4
GLA_KEY_WIDTH = D_MODEL // 2
GLA_VALUE_WIDTH = D_MODEL
GLA_HEAD_K = GLA_KEY_WIDTH // GLA_HEADS
GLA_HEAD_V = GLA_VALUE_WIDTH // GLA_HEADS
GLA_GATE_RANK = 16
GLA_GATE_TAU = 16.0

SSM_WIDTH = 2 * D_MODEL
SSM_HEADDIM = 64
SSM_HEADS = SSM_WIDTH // SSM_HEADDIM
SSM_STATE = 128
SSM_GROUPS = 8
SSM_CONV = 4
SSM_NORM_EPS = 1e-5
SSM_BC_WIDTH = SSM_GROUPS * SSM_STATE
SSM_CONV_WIDTH = SSM_WIDTH + 2 * SSM_BC_WIDTH

LANES = 128
SUBLANES = 8
VMEM_LIMIT_BYTES = 52 * 1024 * 1024

RWKV_CHUNK = 64
RWKV_CHUNKS_PER_STEP = 2
RWKV_PAIRS_PER_STEP = 16
RWKV_CHUNK_GROUPS = 1
GLA_BLOCK = 128
SSD_CHUNK = 128

RWKV_PASSES = 1
GATE_PASSES = 3

LOG2E = math.log2(math.e)

ROW_TILE = 512
PROJ_ROW_TILE = 1024


def _cparams(sem):
    return pltpu.CompilerParams(dimension_semantics=sem, vmem_limit_bytes=VMEM_LIMIT_BYTES)


def _dot(a, b, *, nt=False, passes=1):
    dims = (((1,), (1,)), ((), ())) if nt else (((1,), (0,)), ((), ()))

    def d(x, y):
        return lax.dot_general(x, y, dims, preferred_element_type=F32)

    ah = a.astype(BF16)
    bh = b.astype(BF16)
    if passes == 1:
        return d(ah, bh)
    al = (a.astype(F32) - ah.astype(F32)).astype(BF16)
    bl = (b.astype(F32) - bh.astype(F32)).astype(BF16)
    return d(ah, bh) + (d(ah, bl) + d(al, bh))


def _dot_exact_lhs(m_bf16, x):
    x1 = x.astype(BF16)
    r1 = x - x1.astype(F32)
    x2 = r1.astype(BF16)
    x3 = (r1 - x2.astype(F32)).astype(BF16)

    def d(y):
        return jnp.dot(m_bf16, y, preferred_element_type=F32)

    return d(x1) + (d(x2) + d(x3))


def _silu(x):
    hx = 0.5 * x
    return hx + hx * jnp.tanh(hx)


def _softplus(x):
    return jnp.maximum(x, 0.0) + jnp.log(1.0 + jnp.exp(-jnp.abs(x)))


def _iota2(shape, dim):
    return lax.broadcasted_iota(jnp.int32, shape, dim)


def _ada_kernel(c_ref, w_ref, b_ref, o_ref):
    c = c_ref[...]
    o_ref[...] = _dot(_silu(c), w_ref[...]) + b_ref[...]


def _ada_mod(c_pad, ada_w, ada_b, tn=1024):
    depth, d, n = ada_w.shape
    rows = c_pad.shape[0]
    return pl.pallas_call(
        _ada_kernel,
        grid=(depth, n // tn),
        in_specs=[
            pl.BlockSpec((rows, d), lambda l, j: (0, 0)),
            pl.BlockSpec((None, d, tn), lambda l, j: (l, 0, j)),
            pl.BlockSpec((None, 1, tn), lambda l, j: (l, 0, j)),
        ],
        out_specs=pl.BlockSpec((None, rows, tn), lambda l, j: (l, 0, j)),
        out_shape=jax.ShapeDtypeStruct((depth, rows, n), F32),
        compiler_params=_cparams(("arbitrary", "arbitrary")),
        name="ada_mod",
    )(c_pad, ada_w, ada_b.reshape(depth, 1, n))


def _prenorm(x, g, mod):
    ms = jnp.mean(x * x, axis=-1, keepdims=True)
    return x * lax.rsqrt(ms + NORM_EPS) * g * (1.0 + mod[1:2, :]) + mod[0:1, :]


def _prenorm_kernel(x_ref, g_ref, mod_ref, ws_ref, h_ref, os_ref):
    h = _prenorm(x_ref[...], g_ref[...], mod_ref[...]).astype(BF16)
    h_ref[...] = h
    os_ref[...] = jnp.dot(h, ws_ref[...], preferred_element_type=F32)


def _prenorm_call(x, g, mod_l, w_side, seq, *, tm):
    t, d = x.shape
    ns = w_side.shape[1]
    tm = min(tm, seq)
    tiles_per_seq = seq // tm
    return pl.pallas_call(
        _prenorm_kernel,
        grid=(t // tm,),
        in_specs=[
            pl.BlockSpec((tm, d), lambda i: (i, 0)),
            pl.BlockSpec((1, d), lambda i: (0, 0)),
            pl.BlockSpec((None, 3, d), lambda i: (i // tiles_per_seq, 0, 0)),
            pl.BlockSpec((d, ns), lambda i: (0, 0)),
        ],
        out_specs=[pl.BlockSpec((tm, d), lambda i: (i, 0)), pl.BlockSpec((tm, ns), lambda i: (i, 0))],
        out_shape=[jax.ShapeDtypeStruct((t, d), BF16), jax.ShapeDtypeStruct((t, ns), F32)],
        compiler_params=_cparams(("arbitrary",)),
        name="prenorm",
    )(x, g, mod_l, w_side)


def _proj_kernel(a_ref, w_ref, o_ref, wb_scr):
    @pl.when(pl.program_id(1) == 0)
    def _():
        wb_scr[...] = w_ref[...].astype(BF16)

    o_ref[...] = jnp.dot(a_ref[...], wb_scr[...], preferred_element_type=F32).astype(o_ref.dtype)


def _proj(a, w, layer, n, seq, *, tm, tn, out_dtype):
    groups, t, kd = a.shape
    tm = min(tm, seq)
    tiles_per_group = (n // tn) // groups
    return pl.pallas_call(
        _proj_kernel,
        grid=(n // tn, t // tm),
        in_specs=[
            pl.BlockSpec((None, tm, kd), lambda j, i: (j // tiles_per_group, i, 0)),
            pl.BlockSpec((None, kd, tn), lambda j, i: (layer, 0, j)),
        ],
        out_specs=pl.BlockSpec((tm, tn), lambda j, i: (i, j)),
        out_shape=jax.ShapeDtypeStruct((t, n), out_dtype),
        scratch_shapes=[pltpu.VMEM((kd, tn), BF16)],
        compiler_params=_cparams(("arbitrary", "arbitrary")),
        name="proj",
    )(a, w)


def _outproj_kernel(z_ref, w_ref, x_ref, mod_ref, o_ref, wb_scr):
    @pl.when(pl.program_id(1) == 0)
    def _():
        wb_scr[...] = w_ref[...].astype(BF16)

    acc = jnp.dot(z_ref[...], wb_scr[...], preferred_element_type=F32)
    o_ref[...] = x_ref[...] + mod_ref[2:3, :] * acc


def _outproj(z, w, layer, x, mod_l, seq, *, tm, tn):
    t, kd = z.shape
    n = w.shape[2]
    tm = min(tm, seq)
    tiles_per_seq = seq // tm
    return pl.pallas_call(
        _outproj_kernel,
        grid=(n // tn, t // tm),
        in_specs=[
            pl.BlockSpec((tm, kd), lambda j, i: (i, 0)),
            pl.BlockSpec((None, kd, tn), lambda j, i: (layer, 0, j)),
            pl.BlockSpec((tm, tn), lambda j, i: (i, j)),
            pl.BlockSpec((None, 3, tn), lambda j, i: (i // tiles_per_seq, 0, j)),
        ],
        out_specs=pl.BlockSpec((tm, tn), lambda j, i: (i, j)),
        out_shape=jax.ShapeDtypeStruct((t, n), F32),
        scratch_shapes=[pltpu.VMEM((kd, tn), BF16)],
        compiler_params=_cparams(("arbitrary", "arbitrary")),
        name="outproj",
    )(z, w, x, mod_l)


def _final_norm_kernel(x_ref, g_ref, o_ref):
    x = x_ref[...]
    ms = jnp.mean(x * x, axis=-1, keepdims=True)
    o_ref[...] = x * lax.rsqrt(ms + NORM_EPS) * g_ref[...]


def _final_norm(x, g, *, tm=512):
    t, d = x.shape
    tm = min(tm, t)
    return pl.pallas_call(
        _final_norm_kernel,
        grid=(t // tm,),
        in_specs=[pl.BlockSpec((tm, d), lambda i: (i, 0)), pl.BlockSpec((1, d), lambda i: (0, 0))],
        out_specs=pl.BlockSpec((tm, d), lambda i: (i, 0)),
        out_shape=jax.ShapeDtypeStruct((t, d), F32),
        compiler_params=_cparams(("arbitrary",)),
        name="final_norm",
    )(x, g)


def _rwkv_prenorm_kernel(x_ref, xh_ref, g_ref, mod_ref, mu_ref, w1_ref, xs_ref, lora_ref, *, tiles_per_seq):
    i = pl.program_id(0)
    g = g_ref[...]
    mod = mod_ref[...]
    h = _prenorm(x_ref[...], g, mod)
    hp8 = _prenorm(xh_ref[...], g, mod)
    first = (i % tiles_per_seq) == 0
    prev_row = jnp.where(first, 0.0, hp8[SUBLANES - 1:SUBLANES, :])
    rolled = pltpu.roll(h, 1, 0)
    row0 = _iota2(h.shape, 0) == 0
    dh = jnp.where(row0, prev_row, rolled) - h
    mu = mu_ref[...]
    for c in range(4):
        xs_ref[c] = (h + dh * mu[c:c + 1, :]).astype(BF16)
    xw = (h + dh * mu[4:5, :]).astype(BF16)
    xa = (h + dh * mu[5:6, :]).astype(BF16)
    w1 = w1_ref[...]
    dec_h = jnp.tanh(jnp.dot(xw, w1[:, :LANES], preferred_element_type=F32))
    icl_h = jnp.dot(xa, w1[:, LANES:], preferred_element_type=F32)
    lora_ref[...] = jnp.concatenate([dec_h, icl_h], axis=1)


def _rwkv_prenorm(x, g, mod_l, mu, w1, seq, *, tm):
    t, d = x.shape
    tm = min(tm, seq)
    tiles_per_seq = seq // tm
    halo_blocks = tm // SUBLANES
    kern = functools.partial(_rwkv_prenorm_kernel, tiles_per_seq=tiles_per_seq)
    return pl.pallas_call(
        kern,
        grid=(t // tm,),
        in_specs=[
            pl.BlockSpec((tm, d), lambda i: (i, 0)),
            pl.BlockSpec((SUBLANES, d), lambda i: (jnp.maximum(i * halo_blocks - 1, 0), 0)),
            pl.BlockSpec((1, d), lambda i: (0, 0)),
            pl.BlockSpec((None, 3, d), lambda i: (i // tiles_per_seq, 0, 0)),
            pl.BlockSpec((6, d), lambda i: (0, 0)),
            pl.BlockSpec((d, 2 * LANES), lambda i: (0, 0)),
        ],
        out_specs=[
            pl.BlockSpec((4, tm, d), lambda i: (0, i, 0)),
            pl.BlockSpec((tm, 2 * LANES), lambda i: (i, 0)),
        ],
        out_shape=[
            jax.ShapeDtypeStruct((4, t, d), BF16),
            jax.ShapeDtypeStruct((t, 2 * LANES), F32),
        ],
        compiler_params=_cparams(("arbitrary",)),
        name="rwkv_prenorm",
    )(x, x, g, mod_l, mu, w1)


def _interleave(*gens):
    live = list(gens)
    while live:
        for gen in list(live):
            try:
                next(gen)
            except StopIteration:
                live.remove(gen)


def _unit_lower_inverse_stages(a_list, eye, blk8, offdiag, passes, out):
    n = eye.shape[0]
    eye_b = eye.astype(BF16)
    a8 = [a * blk8 for a in a_list]
    a2 = [_dot(x, x, passes=passes).astype(BF16) for x in a8]
    yield
    ia8 = [eye_b + x for x in a8]
    both = [_dot(jnp.concatenate([y, x], axis=0), y, passes=passes) for x, y in zip(ia8, a2)]
    yield
    inv = [_dot(x + b[n:], eye + b[:n], passes=passes) for x, b in zip(ia8, both)]
    yield
    for m in offdiag:
        t = [_dot(a * m, i, passes=passes) for a, i in zip(a_list, inv)]
        yield
        inv = [i + _dot(i, x, passes=passes) for i, x in zip(inv, t)]
        yield
    out.extend(inv)


def _rwkv_scan_kernel(r_ref, k_ref, v_ref, g_ref, lora_ref, dw2_ref, iw2_ref, dw0_ref, iw0_ref,
                      kk_ref, ka_ref, rk_ref, gnw_ref, gnb_ref, z_ref, s_scr, cum_scr, y_scr,
                      *, chunks, pairs, groups, passes):
    L = RWKV_CHUNK
    N = RWKV_HEAD
    R = L * chunks
    P = 2 * L

    @pl.when(pl.program_id(2) == 0)
    def _():
        s_scr[...] = jnp.zeros_like(s_scr)

    lane = _iota2((1, LANES), 1)
    m0 = lane < N
    mf0 = m0.astype(F32)
    mf1 = 1.0 - mf0

    def headsum(x):
        s0 = jnp.sum(jnp.where(m0, x, 0.0), axis=-1, keepdims=True)
        s1 = jnp.sum(jnp.where(m0, 0.0, x), axis=-1, keepdims=True)
        return jnp.where(m0, s0, s1)

    mb0 = mf0.astype(BF16)
    mb1 = mf1.astype(BF16)

    def stack(x):
        xb = x.astype(BF16)
        return jnp.concatenate([xb * mb0, xb * mb1], axis=0)

    lora = lora_ref[...]
    dec = dw0_ref[...] + _dot(lora[:, :LANES], dw2_ref[...], passes=passes)
    lw_all = -jnp.exp(-_softplus(-dec) - 0.5) * LOG2E
    a_all = jax.nn.sigmoid(iw0_ref[...] + _dot(lora[:, LANES:], iw2_ref[...], passes=passes))
    tri = jnp.where(_iota2((L, L), 1) <= _iota2((L, L), 0), 1.0, 0.0).astype(BF16)
    cum_all = jnp.concatenate([_dot_exact_lhs(tri, lw_all[c * L:(c + 1) * L]) for c in range(chunks)], axis=0)
    cum_scr[...] = cum_all

    pair_vals = []
    for p in range(pairs):
        cols = slice(p * LANES, (p + 1) * LANES)
        r = r_ref[:, cols].astype(F32)
        k = k_ref[:, cols].astype(F32)
        v = v_ref[:, cols].astype(F32)
        a = a_all[:, cols]
        kkr = k * kk_ref[:, cols]
        kk = kkr / jnp.maximum(jnp.sqrt(headsum(kkr * kkr)), 1e-12)
        k2 = k * (1.0 + (a - 1.0) * ka_ref[:, cols])
        cum = cum_all[:, cols]
        pair_vals.append(dict(r=r, k2=k2, v=v, av=-kk, bv=kk * a, cum=cum, cumex=cum - lw_all[:, cols]))

    rp = _iota2((P, P), 0)
    cp = _iota2((P, P), 1)
    strict = (rp % L) > (cp % L)
    incl = (rp % L) >= (cp % L)
    eye = jnp.where(rp == cp, 1.0, 0.0)
    blk8 = jnp.where((rp // 8) == (cp // 8), 1.0, 0.0).astype(BF16)
    offdiag = [jnp.where(((rp // (2 * b)) == (cp // (2 * b))) & ((rp // b) != (cp // b)), 1.0, 0.0).astype(BF16)
               for b in (8, 16, 32)]
    fac = {}

    def factor_stages(items):
        pre = []
        for p, c in items:
            pv = pair_vals[p]
            cols = slice(p * LANES, (p + 1) * LANES)
            sl = slice(c * L, (c + 1) * L)
            cref = cum_scr[pl.ds(c * L + L // 2 - 1, 1), cols]
            clast = cum_scr[pl.ds(c * L + L - 1, 1), cols]
            cum_c, cumex_c = pv["cum"][sl], pv["cumex"][sl]
            r_c, k_c, v_c, av_c, bv_c = pv["r"][sl], pv["k2"][sl], pv["v"][sl], pv["av"][sl], pv["bv"][sl]
            e_out = jnp.exp2(cref - cum_c)
            e_end = jnp.exp2(clast - cum_c)
            pre.append(dict(
                lhs1=jnp.concatenate([stack(av_c * jnp.exp2(cumex_c - cref)), stack(r_c * jnp.exp2(cum_c - cref))],
                                     axis=0),
                rhs1=jnp.concatenate([stack(bv_c * e_out), stack(k_c * e_out)], axis=0),
                v_st=stack(v_c), a0_st=stack(av_c * jnp.exp2(cumex_c)), r0_st=stack(r_c * jnp.exp2(cum_c)),
                bk=jnp.concatenate([stack(bv_c * e_end), stack(k_c * e_end)], axis=0), decay=jnp.exp2(clast)))
        yield
        x1 = [_dot(f["lhs1"], f["rhs1"], nt=True, passes=passes) for f in pre]
        yield
        a_ab = [jnp.where(strict, x[:P, :P], 0.0).astype(BF16) for x in x1]
        a_kr = [jnp.concatenate([jnp.where(strict, x[:P, P:], 0.0), jnp.where(incl, x[P:, P:], 0.0)], axis=0)
                for x in x1]
        a_rb = [jnp.where(incl, x[P:, :P], 0.0) for x in x1]
        m1 = [_dot(a, f["v_st"], passes=passes) for a, f in zip(a_kr, pre)]
        yield
        tinv = []
        yield from _unit_lower_inverse_stages(a_ab, eye, blk8, offdiag, passes, tinv)
        m2 = [_dot(t, jnp.concatenate([f["a0_st"], m[:P].astype(BF16)], axis=1), passes=passes)
              for t, f, m in zip(tinv, pre, m1)]
        for it, f, m, mm, arb in zip(items, pre, m1, m2, a_rb):
            w_st, uv_st = mm[:, :LANES], mm[:, LANES:]
            fac[it] = dict(wr=jnp.concatenate([w_st.astype(BF16), f["r0_st"]], axis=0), uv_st=uv_st,
                           yv_st=m[P:], a_rb=arb, v_t=f["v_st"].astype(F32).T, decay=f["decay"], bk=f["bk"])

    state = [s_scr[p] for p in range(pairs)]

    def state_stages(chunk_ids):
        for c in chunk_ids:
            fs = [fac[(p, c)] for p in range(pairs)]
            m3 = [_dot(fs[p]["wr"], state[p], nt=True, passes=passes) for p in range(pairs)]
            yield
            u_st = [m3[p][:P] + fs[p]["uv_st"] for p in range(pairs)]
            for p in range(pairs):
                state[p] = state[p] * fs[p]["decay"] + _dot(
                    jnp.concatenate([u_st[p].T, fs[p]["v_t"]], axis=1), fs[p]["bk"], passes=passes)
            yield
            for p in range(pairs):
                y_st = m3[p][P:] + fs[p]["yv_st"] + _dot(fs[p]["a_rb"], u_st[p], passes=passes)
                y_scr[c * L:(c + 1) * L, p * LANES:(p + 1) * LANES] = y_st[:L] + y_st[L:]
            yield

    per_group = chunks // groups
    chunk_groups = [list(range(gi * per_group, (gi + 1) * per_group)) for gi in range(groups)]
    group_items = [[(p, c) for p in range(pairs) for c in cg] for cg in chunk_groups]
    _interleave(factor_stages(group_items[0]))
    for gi in range(1, groups):
        _interleave(factor_stages(group_items[gi]), state_stages(chunk_groups[gi - 1]))
    _interleave(state_stages(chunk_groups[-1]))
    for p in range(pairs):
        s_scr[p] = state[p]

    for p in range(pairs):
        cols = slice(p * LANES, (p + 1) * LANES)
        pv = pair_vals[p]
        y = y_scr[:, cols]
        mean = headsum(y) * (1.0 / N)
        yc = y - mean
        var = headsum(yc * yc) * (1.0 / N)
        yn = yc * lax.rsqrt(var + RWKV_GN_EPS) * gnw_ref[:, cols] + gnb_ref[:, cols]
        bonus = headsum(pv["r"] * pv["k2"] * rk_ref[:, cols]) * pv["v"]
        z_ref[:, cols] = ((yn + bonus) * _silu(g_ref[:, cols].astype(F32))).astype(z_ref.dtype)


def _rwkv_scan(rkvg, lora, p, batch, seq, *, out_dtype):
    t = rkvg.shape[0]
    w = D_MODEL
    pairs = RWKV_PAIRS_PER_STEP
    bw = pairs * LANES
    nblk = w // bw
    chunks = min(RWKV_CHUNKS_PER_STEP, seq // RWKV_CHUNK)
    rows = RWKV_CHUNK * chunks
    steps = seq // rows

    def act(col0):
        return pl.BlockSpec((rows, bw), lambda b, h, n: (b * steps + n, col0 + h))

    def vec():
        return pl.BlockSpec((1, bw), lambda b, h, n: (0, h))

    def w2():
        return pl.BlockSpec((LANES, bw), lambda b, h, n: (0, h))

    kern = functools.partial(_rwkv_scan_kernel, chunks=chunks, pairs=pairs, groups=min(RWKV_CHUNK_GROUPS, chunks),
                             passes=RWKV_PASSES)
    return pl.pallas_call(
        kern,
        grid=(batch, nblk, steps),
        in_specs=[act(0), act(nblk), act(2 * nblk), act(3 * nblk),
                  pl.BlockSpec((rows, 2 * LANES), lambda b, h, n: (b * steps + n, 0)),
                  w2(), w2(), vec(), vec(), vec(), vec(), vec(), vec(), vec()],
        out_specs=pl.BlockSpec((rows, bw), lambda b, h, n: (b * steps + n, h)),
        out_shape=jax.ShapeDtypeStruct((t, w), out_dtype),
        scratch_shapes=[pltpu.VMEM((pairs, LANES, LANES), F32), pltpu.VMEM((rows, bw), F32),
                        pltpu.VMEM((rows, bw), F32)],
        compiler_params=_cparams(("arbitrary", "arbitrary", "arbitrary")),
        name="rwkv_scan",
    )(rkvg, rkvg, rkvg, rkvg, lora, p["dec_w2"], p["iclr_w2"], p["dec_w0"], p["iclr_w0"],
      p["k_k"], p["k_a"], p["r_k"], p["gn_w"], p["gn_b"])


def _gla_kernel(q_ref, k_ref, v_ref, g_ref, low_ref, w2_ref, b_ref, hg_ref, o_ref, s_scr):
    R = GLA_BLOCK
    Hf = R // 2

    @pl.when(pl.program_id(1) == 0)
    def _():
        s_scr[...] = jnp.zeros_like(s_scr)

    DK, DV = GLA_HEAD_K, GLA_HEAD_V
    heads = range(GLA_HEADS)
    ri = _iota2((R, R), 0)
    ci = _iota2((R, R), 1)
    tri = jnp.where(ci <= ri, 1.0, 0.0).astype(BF16)
    rh = _iota2((Hf, Hf), 0)
    ch = _iota2((Hf, Hf), 1)
    causal = ch <= rh

    la = -_softplus(-(_dot(low_ref[...], w2_ref[...], passes=GATE_PASSES) + b_ref[...])) * (LOG2E / GLA_GATE_TAU)
    bcum = _dot_exact_lhs(tri, la)
    q = [q_ref[:, h * DK:(h + 1) * DK].astype(F32) * (DK ** -0.5) for h in heads]
    k = [k_ref[:, h * DK:(h + 1) * DK].astype(F32) for h in heads]
    v = [v_ref[:, h * DV:(h + 1) * DV].astype(F32) for h in heads]
    bc = [bcum[:, h * DK:(h + 1) * DK] for h in heads]
    ref_t = [b[Hf // 2 - 1:Hf // 2] for b in bc]
    ref_m = [b[Hf - 1:Hf] for b in bc]
    ref_b = [b[Hf + Hf // 2 - 1:Hf + Hf // 2] for b in bc]
    last = [b[R - 1:R] for b in bc]
    s00 = [jnp.where(causal, _dot(q[h][:Hf] * jnp.exp2(bc[h][:Hf] - ref_t[h]),
                                  k[h][:Hf] * jnp.exp2(ref_t[h] - bc[h][:Hf]), nt=True), 0.0) for h in heads]
    s11 = [jnp.where(causal, _dot(q[h][Hf:] * jnp.exp2(bc[h][Hf:] - ref_b[h]),
                                  k[h][Hf:] * jnp.exp2(ref_b[h] - bc[h][Hf:]), nt=True), 0.0) for h in heads]
    s10 = [_dot(q[h][Hf:] * jnp.exp2(bc[h][Hf:] - ref_m[h]), k[h][:Hf] * jnp.exp2(ref_m[h] - bc[h][:Hf]), nt=True)
           for h in heads]
    st = [s_scr[h] for h in heads]
    o_int = [_dot(q[h] * jnp.exp2(bc[h]), st[h], nt=True) for h in heads]
    o_top = [_dot(s00[h], v[h][:Hf]) for h in heads]
    o_bot = [_dot(s10[h], v[h][:Hf]) + _dot(s11[h], v[h][Hf:]) for h in heads]
    upd = [_dot(v[h].T, k[h] * jnp.exp2(last[h] - bc[h])) for h in heads]
    for h in heads:
        s_scr[h] = st[h] * jnp.exp2(last[h]) + upd[h]
        o = jnp.concatenate([o_top[h], o_bot[h]], axis=0) + o_int[h]
        ms = jnp.mean(o * o, axis=-1, keepdims=True)
        on = o * lax.rsqrt(ms + NORM_EPS) * hg_ref[...]
        cols = slice(h * DV, (h + 1) * DV)
        o_ref[:, cols] = (on * _silu(g_ref[:, cols].astype(F32))).astype(o_ref.dtype)


def _gla_scan(proj, low, gate_w2, gate_b, head_g, batch, seq, *, out_dtype):
    t = proj.shape[0]
    R = GLA_BLOCK
    steps = seq // R
    kw, vw = GLA_KEY_WIDTH, GLA_VALUE_WIDTH

    def row(b, n):
        return b * steps + n

    return pl.pallas_call(
        _gla_kernel,
        grid=(batch, steps),
        in_specs=[
            pl.BlockSpec((R, kw), lambda b, n: (row(b, n), 0)),
            pl.BlockSpec((R, kw), lambda b, n: (row(b, n), 1)),
            pl.BlockSpec((R, vw), lambda b, n: (row(b, n), 2 * kw // vw)),
            pl.BlockSpec((R, vw), lambda b, n: (row(b, n), 2 * kw // vw + 1)),
            pl.BlockSpec((R, LANES), lambda b, n: (row(b, n), 0)),
            pl.BlockSpec((LANES, kw), lambda b, n: (0, 0)),
            pl.BlockSpec((1, kw), lambda b, n: (0, 0)),
            pl.BlockSpec((1, GLA_HEAD_V), lambda b, n: (0, 0)),
        ],
        out_specs=pl.BlockSpec((R, vw), lambda b, n: (row(b, n), 0)),
        out_shape=jax.ShapeDtypeStruct((t, vw), out_dtype),
        scratch_shapes=[pltpu.VMEM((GLA_HEADS, GLA_HEAD_V, GLA_HEAD_K), F32)],
        compiler_params=_cparams(("arbitrary", "arbitrary")),
        name="gla_scan",
    )(proj, proj, proj, proj, low, gate_w2, gate_b, head_g)


def _ssd_kernel(z_ref, xr_ref, br_ref, cr_ref, dt_ref, cw_ref, cbias_ref, dtb_ref, alog_ref, dsk_ref, ng_ref, o_ref,
                st_scr, raw_scr, xs_ref, bm_ref, cm_ref):
    C = SSD_CHUNK
    P = SSM_HEADDIM
    pairs_per_group = (SSM_HEADS // SSM_GROUPS) // 2
    gw = SSM_WIDTH // SSM_GROUPS

    @pl.when(pl.program_id(1) == 0)
    def _():
        st_scr[...] = jnp.zeros_like(st_scr)
        raw_scr[...] = jnp.zeros_like(raw_scr)

    raw_scr[0:SUBLANES, :] = raw_scr[C:C + SUBLANES, :]
    col = 0
    for src_ref, act_ref in ((xr_ref, xs_ref), (br_ref, bm_ref), (cr_ref, cm_ref)):
        width = src_ref.shape[1]
        cols = slice(col, col + width)
        raw_scr[SUBLANES:, cols] = src_ref[...].astype(F32)
        acc = cbias_ref[:, cols]
        for s in range(SSM_CONV):
            tap = raw_scr[SUBLANES - s:SUBLANES - s + C, cols]
            acc = acc + tap * cw_ref[SSM_CONV - 1 - s:SSM_CONV - s, cols]
        act_ref[...] = _silu(acc)
        col += width

    dt = _softplus(dt_ref[...] + dtb_ref[...])
    da = dt * (-LOG2E * jnp.exp(alog_ref[...]))
    ri = _iota2((C, C), 0)
    ci = _iota2((C, C), 1)
    causal = ci <= ri
    tri = jnp.where(causal, 1.0, 0.0).astype(BF16)
    acum = _dot_exact_lhs(tri, da)
    acum_t = acum.T
    alast = acum[C - 1:C, :]
    dsk = dsk_ref[...]

    lane = _iota2((1, LANES), 1)
    m0 = lane < P
    mb0 = m0.astype(BF16)
    mb1 = 1 - mb0

    def pair_cols(x, h0):
        return jnp.where(m0, x[:, h0:h0 + 1], x[:, h0 + 1:h0 + 2])

    for g in range(SSM_GROUPS):
        bm = bm_ref[:, g * SSM_STATE:(g + 1) * SSM_STATE].astype(F32)
        cm = cm_ref[:, g * SSM_STATE:(g + 1) * SSM_STATE].astype(F32)
        cb = _dot(cm, bm, nt=True)
        bm_t = bm.T
        ys = []
        for pp in range(pairs_per_group):
            pidx = g * pairs_per_group + pp
            h0 = 2 * pidx
            cols = slice(pidx * LANES, (pidx + 1) * LANES)
            x_p = xs_ref[:, cols].astype(F32)
            dt_p = pair_cols(dt, h0)
            ac_p = pair_cols(acum, h0)
            al_p = pair_cols(alast, h0)
            xc = x_p * dt_p
            dec0 = jnp.where(causal, jnp.exp2(acum[:, h0:h0 + 1] - acum_t[h0:h0 + 1, :]), 0.0)
            dec1 = jnp.where(causal, jnp.exp2(acum[:, h0 + 1:h0 + 2] - acum_t[h0 + 1:h0 + 2, :]), 0.0)
            lhs = jnp.concatenate([cb * dec0, cb * dec1], axis=1)
            xcb = xc.astype(BF16)
            rhs = jnp.concatenate([xcb * mb0, xcb * mb1], axis=0)
            prev = st_scr[pidx]
            y = _dot(lhs, rhs) + _dot(cm, prev) * jnp.exp2(ac_p) + pair_cols(dsk, h0) * x_p
            st_scr[pidx] = prev * jnp.exp2(al_p) + _dot(bm_t, xc * jnp.exp2(al_p - ac_p))
            ys.append(y * _silu(z_ref[:, cols].astype(F32)))
        yg = jnp.concatenate(ys, axis=1)
        ms = jnp.mean(yg * yg, axis=-1, keepdims=True)
        o_ref[:, g * gw:(g + 1) * gw] = (yg * lax.rsqrt(ms + SSM_NORM_EPS) * ng_ref[:, g * gw:(g + 1) * gw]).astype(o_ref.dtype)


def _ssd_scan(proj, dt, conv_w, conv_b, dt_bias, a_log, d_skip, norm_g, batch, seq, *, out_dtype):
    t = proj.shape[0]
    C = SSD_CHUNK
    steps = seq // C
    npairs = SSM_HEADS // 2

    def row(b, n):
        return b * steps + n

    bcol = 2 * SSM_WIDTH // SSM_BC_WIDTH
    return pl.pallas_call(
        _ssd_kernel,
        grid=(batch, steps),
        in_specs=[
            pl.BlockSpec((C, SSM_WIDTH), lambda b, n: (row(b, n), 0)),
            pl.BlockSpec((C, SSM_WIDTH), lambda b, n: (row(b, n), 1)),
            pl.BlockSpec((C, SSM_BC_WIDTH), lambda b, n: (row(b, n), bcol)),
            pl.BlockSpec((C, SSM_BC_WIDTH), lambda b, n: (row(b, n), bcol + 1)),
            pl.BlockSpec((C, LANES), lambda b, n: (row(b, n), 0)),
            pl.BlockSpec((SSM_CONV, SSM_CONV_WIDTH), lambda b, n: (0, 0)),
            pl.BlockSpec((1, SSM_CONV_WIDTH), lambda b, n: (0, 0)),
            pl.BlockSpec((1, LANES), lambda b, n: (0, 0)),
            pl.BlockSpec((1, LANES), lambda b, n: (0, 0)),
            pl.BlockSpec((1, LANES), lambda b, n: (0, 0)),
            pl.BlockSpec((1, SSM_WIDTH), lambda b, n: (0, 0)),
        ],
        out_specs=pl.BlockSpec((C, SSM_WIDTH), lambda b, n: (row(b, n), 0)),
        out_shape=jax.ShapeDtypeStruct((t, SSM_WIDTH), out_dtype),
        scratch_shapes=[pltpu.VMEM((npairs, SSM_STATE, LANES), F32), pltpu.VMEM((C + SUBLANES, SSM_CONV_WIDTH), F32),
                        pltpu.VMEM((C, SSM_WIDTH), F32), pltpu.VMEM((C, SSM_BC_WIDTH), F32),
                        pltpu.VMEM((C, SSM_BC_WIDTH), F32)],
        compiler_params=_cparams(("arbitrary", "arbitrary")),
        name="ssd_scan",
    )(proj, proj, proj, proj, dt, conv_w, conv_b, dt_bias, a_log, d_skip, norm_g)


def _pad_cols(w, n):
    return jnp.pad(w, ((0, 0), (0, n - w.shape[1])))


def _pad_rows(w, n):
    return jnp.pad(w, ((0, n - w.shape[0]), (0, 0)))


ACT_DTYPE = BF16


def kernel(x, c, ada_w, ada_b, norm_g, final_g, rwkv_mu, rwkv_w_in, rwkv_dec_w1, rwkv_dec_w2, rwkv_dec_w0, rwkv_iclr_w1, rwkv_iclr_w2, rwkv_iclr_w0, rwkv_k_k, rwkv_k_a, rwkv_r_k, rwkv_gn_w, rwkv_gn_b, rwkv_w_out, gla_w_in, gla_gate_w2, gla_gate_b, gla_head_g, gla_w_out, ssd_w_in, ssd_conv_w, ssd_conv_b, ssd_dt_bias, ssd_a_log, ssd_d, ssd_norm_g, ssd_w_out):
    batch, seq, d = x.shape
    t = batch * seq
    xf = x.reshape(t, d)

    c_pad = jnp.pad(c, ((0, SUBLANES - batch % SUBLANES if batch % SUBLANES else 0), (0, 0)))
    mod = _ada_mod(c_pad, ada_w, ada_b)[:, :batch].reshape(DEPTH, batch, 3, d)

    for i in range(DEPTH):
        kind, j = i % N_MIXERS, i // N_MIXERS
        g = norm_g[i].reshape(1, d)
        mod_l = mod[i]
        if kind == 0:
            w1 = jnp.concatenate([_pad_cols(rwkv_dec_w1[j], LANES), _pad_cols(rwkv_iclr_w1[j], LANES)], axis=1)
            xs, lora = _rwkv_prenorm(xf, g, mod_l, rwkv_mu[j], w1.astype(BF16), seq, tm=ROW_TILE)
            rkvg = _proj(xs, rwkv_w_in, j, 4 * d, seq, tm=PROJ_ROW_TILE, tn=1024, out_dtype=ACT_DTYPE)
            row = lambda v: v.reshape(1, -1)
            params = dict(dec_w2=_pad_rows(rwkv_dec_w2[j], LANES), iclr_w2=_pad_rows(rwkv_iclr_w2[j], LANES),
                          dec_w0=row(rwkv_dec_w0[j]), iclr_w0=row(rwkv_iclr_w0[j]), k_k=row(rwkv_k_k[j]),
                          k_a=row(rwkv_k_a[j]), r_k=row(rwkv_r_k[j]), gn_w=row(rwkv_gn_w[j]), gn_b=row(rwkv_gn_b[j]))
            z = _rwkv_scan(rkvg, lora, params, batch, seq, out_dtype=ACT_DTYPE)
            xf = _outproj(z, rwkv_w_out, j, xf, mod_l, seq, tm=PROJ_ROW_TILE, tn=1024)
        elif kind == 1:
            nmain = 2 * GLA_KEY_WIDTH + 2 * GLA_VALUE_WIDTH
            w_bf = gla_w_in[j:j + 1].astype(BF16)
            h, low = _prenorm_call(xf, g, mod_l, _pad_cols(w_bf[0, :, nmain:], LANES), seq, tm=ROW_TILE)
            proj = _proj(h[None], w_bf, 0, nmain, seq, tm=PROJ_ROW_TILE, tn=1024, out_dtype=ACT_DTYPE)
            z = _gla_scan(proj, low, _pad_rows(gla_gate_w2[j], LANES), gla_gate_b[j].reshape(1, -1),
                          gla_head_g[j].reshape(1, -1), batch, seq, out_dtype=ACT_DTYPE)
            xf = _outproj(z, gla_w_out, j, xf, mod_l, seq, tm=PROJ_ROW_TILE, tn=1024)
        else:
            nmain = SSM_WIDTH + SSM_CONV_WIDTH
            w_bf = ssd_w_in[j:j + 1].astype(BF16)
            h, dt = _prenorm_call(xf, g, mod_l, _pad_cols(w_bf[0, :, nmain:], LANES), seq, tm=ROW_TILE)
            proj = _proj(h[None], w_bf, 0, nmain, seq, tm=PROJ_ROW_TILE, tn=1024, out_dtype=ACT_DTYPE)
            padl = lambda v: _pad_cols(v.reshape(1, -1), LANES)
            z = _ssd_scan(proj, dt, ssd_conv_w[j], ssd_conv_b[j].reshape(1, -1),
                          padl(ssd_dt_bias[j]), padl(ssd_a_log[j]), padl(ssd_d[j]),
                          ssd_norm_g[j].reshape(1, -1), batch, seq, out_dtype=ACT_DTYPE)
            xf = _outproj(z, ssd_w_out, j, xf, mod_l, seq, tm=PROJ_ROW_TILE, tn=512)

    out = _final_norm(xf, final_g.reshape(1, d))
    return out.reshape(batch, seq, d)
```

```python
import functools
import math

import jax
import jax.numpy as jnp
from jax import lax
from jax.experimental import pallas as pl
from jax.experimental.pallas import tpu as pltpu

F32 = jnp.float32
BF16 = jnp.bfloat16

D_MODEL = 2048
DEPTH = 4
N_MIXERS = 3
NORM_EPS = 1e-6

RWKV_HEAD = 64
RWKV_LORA = 96
RWKV_GN_EPS = 64e-5

GLA_HEADS = 4
GLA_KEY_WIDTH = D_MODEL // 2
GLA_VALUE_WIDTH = D_MODEL
GLA_HEAD_K = GLA_KEY_WIDTH // GLA_HEADS
GLA_HEAD_V = GLA_VALUE_WIDTH // GLA_HEADS
GLA_GATE_RANK = 16
GLA_GATE_TAU = 16.0

SSM_WIDTH = 2 * D_MODEL
SSM_HEADDIM = 64
SSM_HEADS = SSM_WIDTH // SSM_HEADDIM
SSM_STATE = 128
SSM_GROUPS = 8
SSM_CONV = 4
SSM_NORM_EPS = 1e-5
SSM_BC_WIDTH = SSM_GROUPS * SSM_STATE
SSM_CONV_WIDTH = SSM_WIDTH + 2 * SSM_BC_WIDTH

LANES = 128
SUBLANES = 8
VMEM_LIMIT_BYTES = 52 * 1024 * 1024

RWKV_CHUNK = 64
RWKV_CHUNKS_PER_STEP = 2
RWKV_PAIRS_PER_STEP = 16
GLA_BLOCK = 128
SSD_CHUNK = 128

RWKV_PASSES = 1
GATE_PASSES = 3

LOG2E = math.log2(math.e)

ROW_TILE = 512
PROJ_ROW_TILE = 1024


def _cparams(sem):
    return pltpu.CompilerParams(dimension_semantics=sem, vmem_limit_bytes=VMEM_LIMIT_BYTES)


def _dot(a, b, *, nt=False, passes=1):
    dims = (((1,), (1,)), ((), ())) if nt else (((1,), (0,)), ((), ()))

    def d(x, y):
        return lax.dot_general(x, y, dims, preferred_element_type=F32)

    ah = a.astype(BF16)
    bh = b.astype(BF16)
    if passes == 1:
        return d(ah, bh)
    al = (a.astype(F32) - ah.astype(F32)).astype(BF16)
    bl = (b.astype(F32) - bh.astype(F32)).astype(BF16)
    return d(ah, bh) + (d(ah, bl) + d(al, bh))


def _dot_exact_lhs(m_bf16, x):
    x1 = x.astype(BF16)
    r1 = x - x1.astype(F32)
    x2 = r1.astype(BF16)
    x3 = (r1 - x2.astype(F32)).astype(BF16)

    def d(y):
        return jnp.dot(m_bf16, y, preferred_element_type=F32)

    return d(x1) + (d(x2) + d(x3))


def _silu(x):
    hx = 0.5 * x
    return hx + hx * jnp.tanh(hx)


def _softplus(x):
    return jnp.maximum(x, 0.0) + jnp.log(1.0 + jnp.exp(-jnp.abs(x)))


def _iota2(shape, dim):
    return lax.broadcasted_iota(jnp.int32, shape, dim)


def _ada_kernel(c_ref, w_ref, b_ref, o_ref):
    c = c_ref[...]
    o_ref[...] = _dot(_silu(c), w_ref[...]) + b_ref[...]


def _ada_mod(c_pad, ada_w, ada_b, tn=1024):
    depth, d, n = ada_w.shape
    rows = c_pad.shape[0]
    return pl.pallas_call(
        _ada_kernel,
        grid=(depth, n // tn),
        in_specs=[
            pl.BlockSpec((rows, d), lambda l, j: (0, 0)),
            pl.BlockSpec((None, d, tn), lambda l, j: (l, 0, j)),
            pl.BlockSpec((None, 1, tn), lambda l, j: (l, 0, j)),
        ],
        out_specs=pl.BlockSpec((None, rows, tn), lambda l, j: (l, 0, j)),
        out_shape=jax.ShapeDtypeStruct((depth, rows, n), F32),
        compiler_params=_cparams(("arbitrary", "arbitrary")),
        name="ada_mod",
    )(c_pad, ada_w, ada_b.reshape(depth, 1, n))


def _prenorm(x, g, mod):
    ms = jnp.mean(x * x, axis=-1, keepdims=True)
    return x * lax.rsqrt(ms + NORM_EPS) * g * (1.0 + mod[1:2, :]) + mod[0:1, :]


def _prenorm_kernel(x_ref, g_ref, mod_ref, ws_ref, h_ref, os_ref):
    h = _prenorm(x_ref[...], g_ref[...], mod_ref[...]).astype(BF16)
    h_ref[...] = h
    os_ref[...] = jnp.dot(h, ws_ref[...], preferred_element_type=F32)


def _prenorm_call(x, g, mod_l, w_side, seq, *, tm):
    t, d = x.shape
    ns = w_side.shape[1]
    tm = min(tm, seq)
    tiles_per_seq = seq // tm
    return pl.pallas_call(
        _prenorm_kernel,
        grid=(t // tm,),
        in_specs=[
            pl.BlockSpec((tm, d), lambda i: (i, 0)),
            pl.BlockSpec((1, d), lambda i: (0, 0)),
            pl.BlockSpec((None, 3, d), lambda i: (i // tiles_per_seq, 0, 0)),
            pl.BlockSpec((d, ns), lambda i: (0, 0)),
        ],
        out_specs=[pl.BlockSpec((tm, d), lambda i: (i, 0)), pl.BlockSpec((tm, ns), lambda i: (i, 0))],
        out_shape=[jax.ShapeDtypeStruct((t, d), BF16), jax.ShapeDtypeStruct((t, ns), F32)],
        compiler_params=_cparams(("arbitrary",)),
        name="prenorm",
    )(x, g, mod_l, w_side)


def _proj_kernel(a_ref, w_ref, o_ref, wb_scr):
    @pl.when(pl.program_id(1) == 0)
    def _():
        wb_scr[...] = w_ref[...].astype(BF16)

    o_ref[...] = jnp.dot(a_ref[...], wb_scr[...], preferred_element_type=F32).astype(o_ref.dtype)


def _proj(a, w, layer, n, seq, *, tm, tn, out_dtype):
    groups, t, kd = a.shape
    tm = min(tm, seq)
    tiles_per_group = (n // tn) // groups
    return pl.pallas_call(
        _proj_kernel,
        grid=(n // tn, t // tm),
        in_specs=[
            pl.BlockSpec((None, tm, kd), lambda j, i: (j // tiles_per_group, i, 0)),
            pl.BlockSpec((None, kd, tn), lambda j, i: (layer, 0, j)),
        ],
        out_specs=pl.BlockSpec((tm, tn), lambda j, i: (i, j)),
        out_shape=jax.ShapeDtypeStruct((t, n), out_dtype),
        scratch_shapes=[pltpu.VMEM((kd, tn), BF16)],
        compiler_params=_cparams(("arbitrary", "arbitrary")),
        name="proj",
    )(a, w)


def _outproj_kernel(z_ref, w_ref, x_ref, mod_ref, o_ref, wb_scr):
    @pl.when(pl.program_id(1) == 0)
    def _():
        wb_scr[...] = w_ref[...].astype(BF16)

    acc = jnp.dot(z_ref[...], wb_scr[...], preferred_element_type=F32)
    o_ref[...] = x_ref[...] + mod_ref[2:3, :] * acc


def _outproj(z, w, layer, x, mod_l, seq, *, tm, tn):
    t, kd = z.shape
    n = w.shape[2]
    tm = min(tm, seq)
    tiles_per_seq = seq // tm
    return pl.pallas_call(
        _outproj_kernel,
        grid=(n // tn, t // tm),
        in_specs=[
            pl.BlockSpec((tm, kd), lambda j, i: (i, 0)),
            pl.BlockSpec((None, kd, tn), lambda j, i: (layer, 0, j)),
            pl.BlockSpec((tm, tn), lambda j, i: (i, j)),
            pl.BlockSpec((None, 3, tn), lambda j, i: (i // tiles_per_seq, 0, j)),
        ],
        out_specs=pl.BlockSpec((tm, tn), lambda j, i: (i, j)),
        out_shape=jax.ShapeDtypeStruct((t, n), F32),
        scratch_shapes=[pltpu.VMEM((kd, tn), BF16)],
        compiler_params=_cparams(("arbitrary", "arbitrary")),
        name="outproj",
    )(z, w, x, mod_l)


def _final_norm_kernel(x_ref, g_ref, o_ref):
    x = x_ref[...]
    ms = jnp.mean(x * x, axis=-1, keepdims=True)
    o_ref[...] = x * lax.rsqrt(ms + NORM_EPS) * g_ref[...]


def _final_norm(x, g, *, tm=512):
    t, d = x.shape
    tm = min(tm, t)
    return pl.pallas_call(
        _final_norm_kernel,
        grid=(t // tm,),
        in_specs=[pl.BlockSpec((tm, d), lambda i: (i, 0)), pl.BlockSpec((1, d), lambda i: (0, 0))],
        out_specs=pl.BlockSpec((tm, d), lambda i: (i, 0)),
        out_shape=jax.ShapeDtypeStruct((t, d), F32),
        compiler_params=_cparams(("arbitrary",)),
        name="final_norm",
    )(x, g)


def _rwkv_prenorm_kernel(x_ref, xh_ref, g_ref, mod_ref, mu_ref, w1_ref, xs_ref, lora_ref, *, tiles_per_seq):
    i = pl.program_id(0)
    g = g_ref[...]
    mod = mod_ref[...]
    h = _prenorm(x_ref[...], g, mod)
    hp8 = _prenorm(xh_ref[...], g, mod)
    first = (i % tiles_per_seq) == 0
    prev_row = jnp.where(first, 0.0, hp8[SUBLANES - 1:SUBLANES, :])
    rolled = pltpu.roll(h, 1, 0)
    row0 = _iota2(h.shape, 0) == 0
    dh = jnp.where(row0, prev_row, rolled) - h
    mu = mu_ref[...]
    for c in range(4):
        xs_ref[c] = (h + dh * mu[c:c + 1, :]).astype(BF16)
    xw = (h + dh * mu[4:5, :]).astype(BF16)
    xa = (h + dh * mu[5:6, :]).astype(BF16)
    w1 = w1_ref[...]
    dec_h = jnp.tanh(jnp.dot(xw, w1[:, :LANES], preferred_element_type=F32))
    icl_h = jnp.dot(xa, w1[:, LANES:], preferred_element_type=F32)
    lora_ref[...] = jnp.concatenate([dec_h, icl_h], axis=1)


def _rwkv_prenorm(x, g, mod_l, mu, w1, seq, *, tm):
    t, d = x.shape
    tm = min(tm, seq)
    tiles_per_seq = seq // tm
    halo_blocks = tm // SUBLANES
    kern = functools.partial(_rwkv_prenorm_kernel, tiles_per_seq=tiles_per_seq)
    return pl.pallas_call(
        kern,
        grid=(t // tm,),
        in_specs=[
            pl.BlockSpec((tm, d), lambda i: (i, 0)),
            pl.BlockSpec((SUBLANES, d), lambda i: (jnp.maximum(i * halo_blocks - 1, 0), 0)),
            pl.BlockSpec((1, d), lambda i: (0, 0)),
            pl.BlockSpec((None, 3, d), lambda i: (i // tiles_per_seq, 0, 0)),
            pl.BlockSpec((6, d), lambda i: (0, 0)),
            pl.BlockSpec((d, 2 * LANES), lambda i: (0, 0)),
        ],
        out_specs=[
            pl.BlockSpec((4, tm, d), lambda i: (0, i, 0)),
            pl.BlockSpec((tm, 2 * LANES), lambda i: (i, 0)),
        ],
        out_shape=[
            jax.ShapeDtypeStruct((4, t, d), BF16),
            jax.ShapeDtypeStruct((t, 2 * LANES), F32),
        ],
        compiler_params=_cparams(("arbitrary",)),
        name="rwkv_prenorm",
    )(x, x, g, mod_l, mu, w1)


def _unit_lower_inverses(a_list, eye, blk8, offdiag, passes):
    n = eye.shape[0]
    eye_b = eye.astype(BF16)
    a8 = [a * blk8 for a in a_list]
    a2 = [_dot(x, x, passes=passes).astype(BF16) for x in a8]
    ia8 = [eye_b + x for x in a8]
    both = [_dot(jnp.concatenate([y, x], axis=0), y, passes=passes) for x, y in zip(ia8, a2)]
    inv = [_dot(x + b[n:], eye + b[:n], passes=passes) for x, b in zip(ia8, both)]
    for m in offdiag:
        t = [_dot(a * m, i, passes=passes) for a, i in zip(a_list, inv)]
        inv = [i + _dot(i, x, passes=passes) for i, x in zip(inv, t)]
    return inv


def _rwkv_scan_kernel(r_ref, k_ref, v_ref, g_ref, lora_ref, dw2_ref, iw2_ref, dw0_ref, iw0_ref,
                      kk_ref, ka_ref, rk_ref, gnw_ref, gnb_ref, z_ref, s_scr, cum_scr, y_scr,
                      *, chunks, pairs, passes):
    L = RWKV_CHUNK
    N = RWKV_HEAD
    R = L * chunks
    P = 2 * L

    @pl.when(pl.program_id(2) == 0)
    def _():
        s_scr[...] = jnp.zeros_like(s_scr)

    lane = _iota2((1, LANES), 1)
    m0 = lane < N
    mf0 = m0.astype(F32)
    mf1 = 1.0 - mf0

    def headsum(x):
        s0 = jnp.sum(jnp.where(m0, x, 0.0), axis=-1, keepdims=True)
        s1 = jnp.sum(jnp.where(m0, 0.0, x), axis=-1, keepdims=True)
        return jnp.where(m0, s0, s1)

    mb0 = mf0.astype(BF16)
    mb1 = mf1.astype(BF16)

    def stack(x):
        xb = x.astype(BF16)
        return jnp.concatenate([xb * mb0, xb * mb1], axis=0)

    lora = lora_ref[...]
    dec = dw0_ref[...] + _dot(lora[:, :LANES], dw2_ref[...], passes=passes)
    lw_all = -jnp.exp(-_softplus(-dec) - 0.5) * LOG2E
    a_all = jax.nn.sigmoid(iw0_ref[...] + _dot(lora[:, LANES:], iw2_ref[...], passes=passes))
    tri = jnp.where(_iota2((L, L), 1) <= _iota2((L, L), 0), 1.0, 0.0).astype(BF16)
    cum_all = jnp.concatenate([_dot_exact_lhs(tri, lw_all[c * L:(c + 1) * L]) for c in range(chunks)], axis=0)
    cum_scr[...] = cum_all

    pair_vals = []
    for p in range(pairs):
        cols = slice(p * LANES, (p + 1) * LANES)
        r = r_ref[:, cols].astype(F32)
        k = k_ref[:, cols].astype(F32)
        v = v_ref[:, cols].astype(F32)
        a = a_all[:, cols]
        kkr = k * kk_ref[:, cols]
        kk = kkr / jnp.maximum(jnp.sqrt(headsum(kkr * kkr)), 1e-12)
        k2 = k * (1.0 + (a - 1.0) * ka_ref[:, cols])
        cum = cum_all[:, cols]
        pair_vals.append(dict(r=r, k2=k2, v=v, av=-kk, bv=kk * a, cum=cum, cumex=cum - lw_all[:, cols]))

    rp = _iota2((P, P), 0)
    cp = _iota2((P, P), 1)
    strict = (rp % L) > (cp % L)
    incl = (rp % L) >= (cp % L)
    eye = jnp.where(rp == cp, 1.0, 0.0)
    blk8 = jnp.where((rp // 8) == (cp // 8), 1.0, 0.0).astype(BF16)
    offdiag = [jnp.where(((rp // (2 * b)) == (cp // (2 * b))) & ((rp // b) != (cp // b)), 1.0, 0.0).astype(BF16)
               for b in (8, 16, 32)]
    fac = {}

    def factor_pass(items):
        pre = []
        for p, c in items:
            pv = pair_vals[p]
            cols = slice(p * LANES, (p + 1) * LANES)
            sl = slice(c * L, (c + 1) * L)
            cref = cum_scr[pl.ds(c * L + L // 2 - 1, 1), cols]
            clast = cum_scr[pl.ds(c * L + L - 1, 1), cols]
            cum_c, cumex_c = pv["cum"][sl], pv["cumex"][sl]
            r_c, k_c, v_c, av_c, bv_c = pv["r"][sl], pv["k2"][sl], pv["v"][sl], pv["av"][sl], pv["bv"][sl]
            e_out = jnp.exp2(cref - cum_c)
            e_end = jnp.exp2(clast - cum_c)
            pre.append(dict(
                lhs1=jnp.concatenate([stack(av_c * jnp.exp2(cumex_c - cref)), stack(r_c * jnp.exp2(cum_c - cref))],
                                     axis=0),
                rhs1=jnp.concatenate([stack(bv_c * e_out), stack(k_c * e_out)], axis=0),
                v_st=stack(v_c), a0_st=stack(av_c * jnp.exp2(cumex_c)), r0_st=stack(r_c * jnp.exp2(cum_c)),
                bk=jnp.concatenate([stack(bv_c * e_end), stack(k_c * e_end)], axis=0), decay=jnp.exp2(clast)))
        x1 = [_dot(f["lhs1"], f["rhs1"], nt=True, passes=passes) for f in pre]
        a_ab = [jnp.where(strict, x[:P, :P], 0.0).astype(BF16) for x in x1]
        a_kr = [jnp.concatenate([jnp.where(strict, x[:P, P:], 0.0), jnp.where(incl, x[P:, P:], 0.0)], axis=0)
                for x in x1]
        a_rb = [jnp.where(incl, x[P:, :P], 0.0) for x in x1]
        m1 = [_dot(a, f["v_st"], passes=passes) for a, f in zip(a_kr, pre)]
        tinv = _unit_lower_inverses(a_ab, eye, blk8, offdiag, passes)
        m2 = [_dot(t, jnp.concatenate([f["a0_st"], m[:P].astype(BF16)], axis=1), passes=passes)
              for t, f, m in zip(tinv, pre, m1)]
        for it, f, m, mm, arb in zip(items, pre, m1, m2, a_rb):
            w_st, uv_st = mm[:, :LANES], mm[:, LANES:]
            fac[it] = dict(wr=jnp.concatenate([w_st.astype(BF16), f["r0_st"]], axis=0), uv_st=uv_st,
                           yv_st=m[P:], a_rb=arb, v_t=f["v_st"].astype(F32).T, decay=f["decay"], bk=f["bk"])

    state = [s_scr[p] for p in range(pairs)]

    def state_pass():
        for c in range(chunks):
            fs = [fac[(p, c)] for p in range(pairs)]
            m3 = [_dot(fs[p]["wr"], state[p], nt=True, passes=passes) for p in range(pairs)]
            u_st = [m3[p][:P] + fs[p]["uv_st"] for p in range(pairs)]
            for p in range(pairs):
                state[p] = state[p] * fs[p]["decay"] + _dot(
                    jnp.concatenate([u_st[p].T, fs[p]["v_t"]], axis=1), fs[p]["bk"], passes=passes)
            for p in range(pairs):
                y_st = m3[p][P:] + fs[p]["yv_st"] + _dot(fs[p]["a_rb"], u_st[p], passes=passes)
                y_scr[c * L:(c + 1) * L, p * LANES:(p + 1) * LANES] = y_st[:L] + y_st[L:]

    factor_pass([(p, c) for p in range(pairs) for c in range(chunks)])
    state_pass()
    for p in range(pairs):
        s_scr[p] = state[p]

    for p in range(pairs):
        cols = slice(p * LANES, (p + 1) * LANES)
        pv = pair_vals[p]
        y = y_scr[:, cols]
        mean = headsum(y) * (1.0 / N)
        yc = y - mean
        var = headsum(yc * yc) * (1.0 / N)
        yn = yc * lax.rsqrt(var + RWKV_GN_EPS) * gnw_ref[:, cols] + gnb_ref[:, cols]
        bonus = headsum(pv["r"] * pv["k2"] * rk_ref[:, cols]) * pv["v"]
        z_ref[:, cols] = ((yn + bonus) * _silu(g_ref[:, cols].astype(F32))).astype(z_ref.dtype)


def _rwkv_scan(rkvg, lora, p, batch, seq, *, out_dtype):
    t = rkvg.shape[0]
    w = D_MODEL
    pairs = RWKV_PAIRS_PER_STEP
    bw = pairs * LANES
    nblk = w // bw
    chunks = min(RWKV_CHUNKS_PER_STEP, seq // RWKV_CHUNK)
    rows = RWKV_CHUNK * chunks
    steps = seq // rows

    def act(col0):
        return pl.BlockSpec((rows, bw), lambda b, h, n: (b * steps + n, col0 + h))

    def vec():
        return pl.BlockSpec((1, bw), lambda b, h, n: (0, h))

    def w2():
        return pl.BlockSpec((LANES, bw), lambda b, h, n: (0, h))

    kern = functools.partial(_rwkv_scan_kernel, chunks=chunks, pairs=pairs, passes=RWKV_PASSES)
    return pl.pallas_call(
        kern,
        grid=(batch, nblk, steps),
        in_specs=[act(0), act(nblk), act(2 * nblk), act(3 * nblk),
                  pl.BlockSpec((rows, 2 * LANES), lambda b, h, n: (b * steps + n, 0)),
                  w2(), w2(), vec(), vec(), vec(), vec(), vec(), vec(), vec()],
        out_specs=pl.BlockSpec((rows, bw), lambda b, h, n: (b * steps + n, h)),
        out_shape=jax.ShapeDtypeStruct((t, w), out_dtype),
        scratch_shapes=[pltpu.VMEM((pairs, LANES, LANES), F32), pltpu.VMEM((rows, bw), F32),
                        pltpu.VMEM((rows, bw), F32)],
        compiler_params=_cparams(("arbitrary", "arbitrary", "arbitrary")),
        name="rwkv_scan",
    )(rkvg, rkvg, rkvg, rkvg, lora, p["dec_w2"], p["iclr_w2"], p["dec_w0"], p["iclr_w0"],
      p["k_k"], p["k_a"], p["r_k"], p["gn_w"], p["gn_b"])


def _gla_kernel(q_ref, k_ref, v_ref, g_ref, low_ref, w2_ref, b_ref, hg_ref, o_ref, s_scr):
    R = GLA_BLOCK
    Hf = R // 2

    @pl.when(pl.program_id(1) == 0)
    def _():
        s_scr[...] = jnp.zeros_like(s_scr)

    DK, DV = GLA_HEAD_K, GLA_HEAD_V
    heads = range(GLA_HEADS)
    ri = _iota2((R, R), 0)
    ci = _iota2((R, R), 1)
    tri = jnp.where(ci <= ri, 1.0, 0.0).astype(BF16)
    rh = _iota2((Hf, Hf), 0)
    ch = _iota2((Hf, Hf), 1)
    causal = ch <= rh

    la = -_softplus(-(_dot(low_ref[...], w2_ref[...], passes=GATE_PASSES) + b_ref[...])) * (LOG2E / GLA_GATE_TAU)
    bcum = _dot_exact_lhs(tri, la)
    q = [q_ref[:, h * DK:(h + 1) * DK].astype(F32) * (DK ** -0.5) for h in heads]
    k = [k_ref[:, h * DK:(h + 1) * DK].astype(F32) for h in heads]
    v = [v_ref[:, h * DV:(h + 1) * DV].astype(F32) for h in heads]
    bc = [bcum[:, h * DK:(h + 1) * DK] for h in heads]
    ref_t = [b[Hf // 2 - 1:Hf // 2] for b in bc]
    ref_m = [b[Hf - 1:Hf] for b in bc]
    ref_b = [b[Hf + Hf // 2 - 1:Hf + Hf // 2] for b in bc]
    last = [b[R - 1:R] for b in bc]
    s00 = [jnp.where(causal, _dot(q[h][:Hf] * jnp.exp2(bc[h][:Hf] - ref_t[h]),
                                  k[h][:Hf] * jnp.exp2(ref_t[h] - bc[h][:Hf]), nt=True), 0.0) for h in heads]
    s11 = [jnp.where(causal, _dot(q[h][Hf:] * jnp.exp2(bc[h][Hf:] - ref_b[h]),
                                  k[h][Hf:] * jnp.exp2(ref_b[h] - bc[h][Hf:]), nt=True), 0.0) for h in heads]
    s10 = [_dot(q[h][Hf:] * jnp.exp2(bc[h][Hf:] - ref_m[h]), k[h][:Hf] * jnp.exp2(ref_m[h] - bc[h][:Hf]), nt=True)
           for h in heads]
    st = [s_scr[h] for h in heads]
    o_int = [_dot(q[h] * jnp.exp2(bc[h]), st[h], nt=True) for h in heads]
    o_top = [_dot(s00[h], v[h][:Hf]) for h in heads]
    o_bot = [_dot(s10[h], v[h][:Hf]) + _dot(s11[h], v[h][Hf:]) for h in heads]
    upd = [_dot(v[h].T, k[h] * jnp.exp2(last[h] - bc[h])) for h in heads]
    for h in heads:
        s_scr[h] = st[h] * jnp.exp2(last[h]) + upd[h]
        o = jnp.concatenate([o_top[h], o_bot[h]], axis=0) + o_int[h]
        ms = jnp.mean(o * o, axis=-1, keepdims=True)
        on = o * lax.rsqrt(ms + NORM_EPS) * hg_ref[...]
        cols = slice(h * DV, (h + 1) * DV)
        o_ref[:, cols] = (on * _silu(g_ref[:, cols].astype(F32))).astype(o_ref.dtype)


def _gla_scan(proj, low, gate_w2, gate_b, head_g, batch, seq, *, out_dtype):
    t = proj.shape[0]
    R = GLA_BLOCK
    steps = seq // R
    kw, vw = GLA_KEY_WIDTH, GLA_VALUE_WIDTH

    def row(b, n):
        return b * steps + n

    return pl.pallas_call(
        _gla_kernel,
        grid=(batch, steps),
        in_specs=[
            pl.BlockSpec((R, kw), lambda b, n: (row(b, n), 0)),
            pl.BlockSpec((R, kw), lambda b, n: (row(b, n), 1)),
            pl.BlockSpec((R, vw), lambda b, n: (row(b, n), 2 * kw // vw)),
            pl.BlockSpec((R, vw), lambda b, n: (row(b, n), 2 * kw // vw + 1)),
            pl.BlockSpec((R, LANES), lambda b, n: (row(b, n), 0)),
            pl.BlockSpec((LANES, kw), lambda b, n: (0, 0)),
            pl.BlockSpec((1, kw), lambda b, n: (0, 0)),
            pl.BlockSpec((1, GLA_HEAD_V), lambda b, n: (0, 0)),
        ],
        out_specs=pl.BlockSpec((R, vw), lambda b, n: (row(b, n), 0)),
        out_shape=jax.ShapeDtypeStruct((t, vw), out_dtype),
        scratch_shapes=[pltpu.VMEM((GLA_HEADS, GLA_HEAD_V, GLA_HEAD_K), F32)],
        compiler_params=_cparams(("arbitrary", "arbitrary")),
        name="gla_scan",
    )(proj, proj, proj, proj, low, gate_w2, gate_b, head_g)


def _ssd_kernel(z_ref, xr_ref, br_ref, cr_ref, dt_ref, cw_ref, cbias_ref, dtb_ref, alog_ref, dsk_ref, ng_ref, o_ref,
                st_scr, raw_scr, xs_ref, bm_ref, cm_ref):
    C = SSD_CHUNK
    P = SSM_HEADDIM
    pairs_per_group = (SSM_HEADS // SSM_GROUPS) // 2
    gw = SSM_WIDTH // SSM_GROUPS

    @pl.when(pl.program_id(1) == 0)
    def _():
        st_scr[...] = jnp.zeros_like(st_scr)
        raw_scr[...] = jnp.zeros_like(raw_scr)

    raw_scr[0:SUBLANES, :] = raw_scr[C:C + SUBLANES, :]
    col = 0
    for src_ref, act_ref in ((xr_ref, xs_ref), (br_ref, bm_ref), (cr_ref, cm_ref)):
        width = src_ref.shape[1]
        cols = slice(col, col + width)
        raw_scr[SUBLANES:, cols] = src_ref[...].astype(F32)
        acc = cbias_ref[:, cols]
        for s in range(SSM_CONV):
            tap = raw_scr[SUBLANES - s:SUBLANES - s + C, cols]
            acc = acc + tap * cw_ref[SSM_CONV - 1 - s:SSM_CONV - s, cols]
        act_ref[...] = _silu(acc)
        col += width

    dt = _softplus(dt_ref[...] + dtb_ref[...])
    da = dt * (-LOG2E * jnp.exp(alog_ref[...]))
    ri = _iota2((C, C), 0)
    ci = _iota2((C, C), 1)
    causal = ci <= ri
    tri = jnp.where(causal, 1.0, 0.0).astype(BF16)
    acum = _dot_exact_lhs(tri, da)
    acum_t = acum.T
    alast = acum[C - 1:C, :]
    dsk = dsk_ref[...]

    lane = _iota2((1, LANES), 1)
    m0 = lane < P
    mb0 = m0.astype(BF16)
    mb1 = 1 - mb0

    def pair_cols(x, h0):
        return jnp.where(m0, x[:, h0:h0 + 1], x[:, h0 + 1:h0 + 2])

    for g in range(SSM_GROUPS):
        bm = bm_ref[:, g * SSM_STATE:(g + 1) * SSM_STATE].astype(F32)
        cm = cm_ref[:, g * SSM_STATE:(g + 1) * SSM_STATE].astype(F32)
        cb = _dot(cm, bm, nt=True)
        bm_t = bm.T
        ys = []
        for pp in range(pairs_per_group):
            pidx = g * pairs_per_group + pp
            h0 = 2 * pidx
            cols = slice(pidx * LANES, (pidx + 1) * LANES)
            x_p = xs_ref[:, cols].astype(F32)
            dt_p = pair_cols(dt, h0)
            ac_p = pair_cols(acum, h0)
            al_p = pair_cols(alast, h0)
            xc = x_p * dt_p
            dec0 = jnp.where(causal, jnp.exp2(acum[:, h0:h0 + 1] - acum_t[h0:h0 + 1, :]), 0.0)
            dec1 = jnp.where(causal, jnp.exp2(acum[:, h0 + 1:h0 + 2] - acum_t[h0 + 1:h0 + 2, :]), 0.0)
            lhs = jnp.concatenate([cb * dec0, cb * dec1], axis=1)
            xcb = xc.astype(BF16)
            rhs = jnp.concatenate([xcb * mb0, xcb * mb1], axis=0)
            prev = st_scr[pidx]
            y = _dot(lhs, rhs) + _dot(cm, prev) * jnp.exp2(ac_p) + pair_cols(dsk, h0) * x_p
            st_scr[pidx] = prev * jnp.exp2(al_p) + _dot(bm_t, xc * jnp.exp2(al_p - ac_p))
            ys.append(y * _silu(z_ref[:, cols].astype(F32)))
        yg = jnp.concatenate(ys, axis=1)
        ms = jnp.mean(yg * yg, axis=-1, keepdims=True)
        o_ref[:, g * gw:(g + 1) * gw] = (yg * lax.rsqrt(ms + SSM_NORM_EPS) * ng_ref[:, g * gw:(g + 1) * gw]).astype(o_ref.dtype)


def _ssd_scan(proj, dt, conv_w, conv_b, dt_bias, a_log, d_skip, norm_g, batch, seq, *, out_dtype):
    t = proj.shape[0]
    C = SSD_CHUNK
    steps = seq // C
    npairs = SSM_HEADS // 2

    def row(b, n):
        return b * steps + n

    bcol = 2 * SSM_WIDTH // SSM_BC_WIDTH
    return pl.pallas_call(
        _ssd_kernel,
        grid=(batch, steps),
        in_specs=[
            pl.BlockSpec((C, SSM_WIDTH), lambda b, n: (row(b, n), 0)),
            pl.BlockSpec((C, SSM_WIDTH), lambda b, n: (row(b, n), 1)),
            pl.BlockSpec((C, SSM_BC_WIDTH), lambda b, n: (row(b, n), bcol)),
            pl.BlockSpec((C, SSM_BC_WIDTH), lambda b, n: (row(b, n), bcol + 1)),
            pl.BlockSpec((C, LANES), lambda b, n: (row(b, n), 0)),
            pl.BlockSpec((SSM_CONV, SSM_CONV_WIDTH), lambda b, n: (0, 0)),
            pl.BlockSpec((1, SSM_CONV_WIDTH), lambda b, n: (0, 0)),
            pl.BlockSpec((1, LANES), lambda b, n: (0, 0)),
            pl.BlockSpec((1, LANES), lambda b, n: (0, 0)),
            pl.BlockSpec((1, LANES), lambda b, n: (0, 0)),
            pl.BlockSpec((1, SSM_WIDTH), lambda b, n: (0, 0)),
        ],
        out_specs=pl.BlockSpec((C, SSM_WIDTH), lambda b, n: (row(b, n), 0)),
        out_shape=jax.ShapeDtypeStruct((t, SSM_WIDTH), out_dtype),
        scratch_shapes=[pltpu.VMEM((npairs, SSM_STATE, LANES), F32), pltpu.VMEM((C + SUBLANES, SSM_CONV_WIDTH), F32),
                        pltpu.VMEM((C, SSM_WIDTH), F32), pltpu.VMEM((C, SSM_BC_WIDTH), F32),
                        pltpu.VMEM((C, SSM_BC_WIDTH), F32)],
        compiler_params=_cparams(("arbitrary", "arbitrary")),
        name="ssd_scan",
    )(proj, proj, proj, proj, dt, conv_w, conv_b, dt_bias, a_log, d_skip, norm_g)


def _pad_cols(w, n):
    return jnp.pad(w, ((0, 0), (0, n - w.shape[1])))


def _pad_rows(w, n):
    return jnp.pad(w, ((0, n - w.shape[0]), (0, 0)))


ACT_DTYPE = BF16


def kernel(x, c, ada_w, ada_b, norm_g, final_g, rwkv_mu, rwkv_w_in, rwkv_dec_w1, rwkv_dec_w2, rwkv_dec_w0, rwkv_iclr_w1, rwkv_iclr_w2, rwkv_iclr_w0, rwkv_k_k, rwkv_k_a, rwkv_r_k, rwkv_gn_w, rwkv_gn_b, rwkv_w_out, gla_w_in, gla_gate_w2, gla_gate_b, gla_head_g, gla_w_out, ssd_w_in, ssd_conv_w, ssd_conv_b, ssd_dt_bias, ssd_a_log, ssd_d, ssd_norm_g, ssd_w_out):
    batch, seq, d = x.shape
    t = batch * seq
    xf = x.reshape(t, d)

    c_pad = jnp.pad(c, ((0, SUBLANES - batch % SUBLANES if batch % SUBLANES else 0), (0, 0)))
    mod = _ada_mod(c_pad, ada_w, ada_b)[:, :batch].reshape(DEPTH, batch, 3, d)

    for i in range(DEPTH):
        kind, j = i % N_MIXERS, i // N_MIXERS
        g = norm_g[i].reshape(1, d)
        mod_l = mod[i]
        if kind == 0:
            w1 = jnp.concatenate([_pad_cols(rwkv_dec_w1[j], LANES), _pad_cols(rwkv_iclr_w1[j], LANES)], axis=1)
            xs, lora = _rwkv_prenorm(xf, g, mod_l, rwkv_mu[j], w1.astype(BF16), seq, tm=ROW_TILE)
            rkvg = _proj(xs, rwkv_w_in, j, 4 * d, seq, tm=PROJ_ROW_TILE, tn=1024, out_dtype=ACT_DTYPE)
            row = lambda v: v.reshape(1, -1)
            params = dict(dec_w2=_pad_rows(rwkv_dec_w2[j], LANES), iclr_w2=_pad_rows(rwkv_iclr_w2[j], LANES),
                          dec_w0=row(rwkv_dec_w0[j]), iclr_w0=row(rwkv_iclr_w0[j]), k_k=row(rwkv_k_k[j]),
                          k_a=row(rwkv_k_a[j]), r_k=row(rwkv_r_k[j]), gn_w=row(rwkv_gn_w[j]), gn_b=row(rwkv_gn_b[j]))
            z = _rwkv_scan(rkvg, lora, params, batch, seq, out_dtype=ACT_DTYPE)
            xf = _outproj(z, rwkv_w_out, j, xf, mod_l, seq, tm=PROJ_ROW_TILE, tn=1024)
        elif kind == 1:
            nmain = 2 * GLA_KEY_WIDTH + 2 * GLA_VALUE_WIDTH
            w_bf = gla_w_in[j:j + 1].astype(BF16)
            h, low = _prenorm_call(xf, g, mod_l, _pad_cols(w_bf[0, :, nmain:], LANES), seq, tm=ROW_TILE)
            proj = _proj(h[None], w_bf, 0, nmain, seq, tm=PROJ_ROW_TILE, tn=1024, out_dtype=ACT_DTYPE)
            z = _gla_scan(proj, low, _pad_rows(gla_gate_w2[j], LANES), gla_gate_b[j].reshape(1, -1),
                          gla_head_g[j].reshape(1, -1), batch, seq, out_dtype=ACT_DTYPE)
            xf = _outproj(z, gla_w_out, j, xf, mod_l, seq, tm=PROJ_ROW_TILE, tn=1024)
        else:
            nmain = SSM_WIDTH + SSM_CONV_WIDTH
            w_bf = ssd_w_in[j:j + 1].astype(BF16)
            h, dt = _prenorm_call(xf, g, mod_l, _pad_cols(w_bf[0, :, nmain:], LANES), seq, tm=ROW_TILE)
            proj = _proj(h[None], w_bf, 0, nmain, seq, tm=PROJ_ROW_TILE, tn=1024, out_dtype=ACT_DTYPE)
            padl = lambda v: _pad_cols(v.reshape(1, -1), LANES)
            z = _ssd_scan(proj, dt, ssd_conv_w[j], ssd_conv_b[j].reshape(1, -1),
                          padl(ssd_dt_bias[j]), padl(ssd_a_log[j]), padl(ssd_d[j]),
                          ssd_norm_g[j].reshape(1, -1), batch, seq, out_dtype=ACT_DTYPE)
            xf = _outproj(z, ssd_w_out, j, xf, mod_l, seq, tm=PROJ_ROW_TILE, tn=512)

    out = _final_norm(xf, final_g.reshape(1, d))
    return out.reshape(batch, seq, d)
```

```python
import functools
import math

import jax
import jax.numpy as jnp
from jax import lax
from jax.experimental import pallas as pl
from jax.experimental.pallas import tpu as pltpu

F32 = jnp.float32
BF16 = jnp.bfloat16

D_MODEL = 2048
DEPTH = 4
N_MIXERS = 3
NORM_EPS = 1e-6

RWKV_HEAD = 64
RWKV_LORA = 96
RWKV_GN_EPS = 64e-5

GLA_HEADS = 4
GLA_KEY_WIDTH = D_MODEL // 2
GLA_VALUE_WIDTH = D_MODEL
GLA_HEAD_K = GLA_KEY_WIDTH // GLA_HEADS
GLA_HEAD_V = GLA_VALUE_WIDTH // GLA_HEADS
GLA_GATE_RANK = 16
GLA_GATE_TAU = 16.0

SSM_WIDTH = 2 * D_MODEL
SSM_HEADDIM = 64
SSM_HEADS = SSM_WIDTH // SSM_HEADDIM
SSM_STATE = 128
SSM_GROUPS = 8
SSM_CONV = 4
SSM_NORM_EPS = 1e-5
SSM_BC_WIDTH = SSM_GROUPS * SSM_STATE
SSM_CONV_WIDTH = SSM_WIDTH + 2 * SSM_BC_WIDTH

LANES = 128
SUBLANES = 8
VMEM_LIMIT_BYTES = 52 * 1024 * 1024

RWKV_CHUNK = 64
RWKV_CHUNKS_PER_STEP = 2
RWKV_PAIRS_PER_STEP = 16
GLA_BLOCK = 128
GLA_BLOCKS_PER_STEP = 2
SSD_CHUNK = 128

RWKV_PASSES = 1
GATE_PASSES = 3

LOG2E = math.log2(math.e)

ROW_TILE = 512
PROJ_ROW_TILE = 1024


def _cparams(sem):
    return pltpu.CompilerParams(dimension_semantics=sem, vmem_limit_bytes=VMEM_LIMIT_BYTES)


def _dot(a, b, *, nt=False, passes=1):
    dims = (((1,), (1,)), ((), ())) if nt else (((1,), (0,)), ((), ()))

    def d(x, y):
        return lax.dot_general(x, y, dims, preferred_element_type=F32)

    ah = a.astype(BF16)
    bh = b.astype(BF16)
    if passes == 1:
        return d(ah, bh)
    al = (a.astype(F32) - ah.astype(F32)).astype(BF16)
    bl = (b.astype(F32) - bh.astype(F32)).astype(BF16)
    return d(ah, bh) + (d(ah, bl) + d(al, bh))


def _dot_exact_lhs(m_bf16, x):
    x1 = x.astype(BF16)
    r1 = x - x1.astype(F32)
    x2 = r1.astype(BF16)
    x3 = (r1 - x2.astype(F32)).astype(BF16)

    def d(y):
        return jnp.dot(m_bf16, y, preferred_element_type=F32)

    return d(x1) + (d(x2) + d(x3))


def _silu(x):
    hx = 0.5 * x
    return hx + hx * jnp.tanh(hx)


def _softplus(x):
    return jnp.maximum(x, 0.0) + jnp.log(1.0 + jnp.exp(-jnp.abs(x)))


def _iota2(shape, dim):
    return lax.broadcasted_iota(jnp.int32, shape, dim)


def _ada_kernel(c_ref, w_ref, b_ref, o_ref):
    c = c_ref[...]
    o_ref[...] = _dot(_silu(c), w_ref[...]) + b_ref[...]


def _ada_mod(c_pad, ada_w, ada_b, tn=1024):
    depth, d, n = ada_w.shape
    rows = c_pad.shape[0]
    return pl.pallas_call(
        _ada_kernel,
        grid=(depth, n // tn),
        in_specs=[
            pl.BlockSpec((rows, d), lambda l, j: (0, 0)),
            pl.BlockSpec((None, d, tn), lambda l, j: (l, 0, j)),
            pl.BlockSpec((None, 1, tn), lambda l, j: (l, 0, j)),
        ],
        out_specs=pl.BlockSpec((None, rows, tn), lambda l, j: (l, 0, j)),
        out_shape=jax.ShapeDtypeStruct((depth, rows, n), F32),
        compiler_params=_cparams(("arbitrary", "arbitrary")),
        name="ada_mod",
    )(c_pad, ada_w, ada_b.reshape(depth, 1, n))


def _prenorm(x, g, mod):
    ms = jnp.mean(x * x, axis=-1, keepdims=True)
    return x * lax.rsqrt(ms + NORM_EPS) * g * (1.0 + mod[1:2, :]) + mod[0:1, :]


def _prenorm_kernel(x_ref, g_ref, mod_ref, ws_ref, h_ref, os_ref):
    h = _prenorm(x_ref[...], g_ref[...], mod_ref[...]).astype(BF16)
    h_ref[...] = h
    os_ref[...] = jnp.dot(h, ws_ref[...], preferred_element_type=F32)


def _prenorm_call(x, g, mod_l, w_side, seq, *, tm):
    t, d = x.shape
    ns = w_side.shape[1]
    tm = min(tm, seq)
    tiles_per_seq = seq // tm
    return pl.pallas_call(
        _prenorm_kernel,
        grid=(t // tm,),
        in_specs=[
            pl.BlockSpec((tm, d), lambda i: (i, 0)),
            pl.BlockSpec((1, d), lambda i: (0, 0)),
            pl.BlockSpec((None, 3, d), lambda i: (i // tiles_per_seq, 0, 0)),
            pl.BlockSpec((d, ns), lambda i: (0, 0)),
        ],
        out_specs=[pl.BlockSpec((tm, d), lambda i: (i, 0)), pl.BlockSpec((tm, ns), lambda i: (i, 0))],
        out_shape=[jax.ShapeDtypeStruct((t, d), BF16), jax.ShapeDtypeStruct((t, ns), F32)],
        compiler_params=_cparams(("arbitrary",)),
        name="prenorm",
    )(x, g, mod_l, w_side)


def _proj_kernel(a_ref, w_ref, o_ref, wb_scr):
    @pl.when(pl.program_id(1) == 0)
    def _():
        wb_scr[...] = w_ref[...].astype(BF16)

    o_ref[...] = jnp.dot(a_ref[...], wb_scr[...], preferred_element_type=F32).astype(o_ref.dtype)


def _proj(a, w, layer, n, seq, *, tm, tn, out_dtype):
    groups, t, kd = a.shape
    tm = min(tm, seq)
    tiles_per_group = (n // tn) // groups
    return pl.pallas_call(
        _proj_kernel,
        grid=(n // tn, t // tm),
        in_specs=[
            pl.BlockSpec((None, tm, kd), lambda j, i: (j // tiles_per_group, i, 0)),
            pl.BlockSpec((None, kd, tn), lambda j, i: (layer, 0, j)),
        ],
        out_specs=pl.BlockSpec((tm, tn), lambda j, i: (i, j)),
        out_shape=jax.ShapeDtypeStruct((t, n), out_dtype),
        scratch_shapes=[pltpu.VMEM((kd, tn), BF16)],
        compiler_params=_cparams(("arbitrary", "arbitrary")),
        name="proj",
    )(a, w)


def _outproj_kernel(z_ref, w_ref, x_ref, mod_ref, o_ref, wb_scr):
    @pl.when(pl.program_id(1) == 0)
    def _():
        wb_scr[...] = w_ref[...].astype(BF16)

    acc = jnp.dot(z_ref[...], wb_scr[...], preferred_element_type=F32)
    o_ref[...] = x_ref[...] + mod_ref[2:3, :] * acc


def _outproj(z, w, layer, x, mod_l, seq, *, tm, tn):
    t, kd = z.shape
    n = w.shape[2]
    tm = min(tm, seq)
    tiles_per_seq = seq // tm
    return pl.pallas_call(
        _outproj_kernel,
        grid=(n // tn, t // tm),
        in_specs=[
            pl.BlockSpec((tm, kd), lambda j, i: (i, 0)),
            pl.BlockSpec((None, kd, tn), lambda j, i: (layer, 0, j)),
            pl.BlockSpec((tm, tn), lambda j, i: (i, j)),
            pl.BlockSpec((None, 3, tn), lambda j, i: (i // tiles_per_seq, 0, j)),
        ],
        out_specs=pl.BlockSpec((tm, tn), lambda j, i: (i, j)),
        out_shape=jax.ShapeDtypeStruct((t, n), F32),
        scratch_shapes=[pltpu.VMEM((kd, tn), BF16)],
        compiler_params=_cparams(("arbitrary", "arbitrary")),
        name="outproj",
    )(z, w, x, mod_l)


def _final_norm_kernel(x_ref, g_ref, o_ref):
    x = x_ref[...]
    ms = jnp.mean(x * x, axis=-1, keepdims=True)
    o_ref[...] = x * lax.rsqrt(ms + NORM_EPS) * g_ref[...]


def _final_norm(x, g, *, tm=512):
    t, d = x.shape
    tm = min(tm, t)
    return pl.pallas_call(
        _final_norm_kernel,
        grid=(t // tm,),
        in_specs=[pl.BlockSpec((tm, d), lambda i: (i, 0)), pl.BlockSpec((1, d), lambda i: (0, 0))],
        out_specs=pl.BlockSpec((tm, d), lambda i: (i, 0)),
        out_shape=jax.ShapeDtypeStruct((t, d), F32),
        compiler_params=_cparams(("arbitrary",)),
        name="final_norm",
    )(x, g)


def _rwkv_prenorm_kernel(x_ref, xh_ref, g_ref, mod_ref, mu_ref, w1_ref, xs_ref, lora_ref, *, tiles_per_seq):
    i = pl.program_id(0)
    g = g_ref[...]
    mod = mod_ref[...]
    h = _prenorm(x_ref[...], g, mod)
    hp8 = _prenorm(xh_ref[...], g, mod)
    first = (i % tiles_per_seq) == 0
    prev_row = jnp.where(first, 0.0, hp8[SUBLANES - 1:SUBLANES, :])
    rolled = pltpu.roll(h, 1, 0)
    row0 = _iota2(h.shape, 0) == 0
    dh = jnp.where(row0, prev_row, rolled) - h
    mu = mu_ref[...]
    for c in range(4):
        xs_ref[c] = (h + dh * mu[c:c + 1, :]).astype(BF16)
    xw = (h + dh * mu[4:5, :]).astype(BF16)
    xa = (h + dh * mu[5:6, :]).astype(BF16)
    w1 = w1_ref[...]
    dec_h = jnp.tanh(jnp.dot(xw, w1[:, :LANES], preferred_element_type=F32))
    icl_h = jnp.dot(xa, w1[:, LANES:], preferred_element_type=F32)
    lora_ref[...] = jnp.concatenate([dec_h, icl_h], axis=1)


def _rwkv_prenorm(x, g, mod_l, mu, w1, seq, *, tm):
    t, d = x.shape
    tm = min(tm, seq)
    tiles_per_seq = seq // tm
    halo_blocks = tm // SUBLANES
    kern = functools.partial(_rwkv_prenorm_kernel, tiles_per_seq=tiles_per_seq)
    return pl.pallas_call(
        kern,
        grid=(t // tm,),
        in_specs=[
            pl.BlockSpec((tm, d), lambda i: (i, 0)),
            pl.BlockSpec((SUBLANES, d), lambda i: (jnp.maximum(i * halo_blocks - 1, 0), 0)),
            pl.BlockSpec((1, d), lambda i: (0, 0)),
            pl.BlockSpec((None, 3, d), lambda i: (i // tiles_per_seq, 0, 0)),
            pl.BlockSpec((6, d), lambda i: (0, 0)),
            pl.BlockSpec((d, 2 * LANES), lambda i: (0, 0)),
        ],
        out_specs=[
            pl.BlockSpec((4, tm, d), lambda i: (0, i, 0)),
            pl.BlockSpec((tm, 2 * LANES), lambda i: (i, 0)),
        ],
        out_shape=[
            jax.ShapeDtypeStruct((4, t, d), BF16),
            jax.ShapeDtypeStruct((t, 2 * LANES), F32),
        ],
        compiler_params=_cparams(("arbitrary",)),
        name="rwkv_prenorm",
    )(x, x, g, mod_l, mu, w1)


def _unit_lower_inverses(a_list, eye, blk8, offdiag, passes):
    n = eye.shape[0]
    eye_b = eye.astype(BF16)
    a8 = [a * blk8 for a in a_list]
    a2 = [_dot(x, x, passes=passes).astype(BF16) for x in a8]
    ia8 = [eye_b + x for x in a8]
    both = [_dot(jnp.concatenate([y, x], axis=0), y, passes=passes) for x, y in zip(ia8, a2)]
    inv = [_dot(x + b[n:], eye + b[:n], passes=passes) for x, b in zip(ia8, both)]
    for m in offdiag:
        t = [_dot(a * m, i, passes=passes) for a, i in zip(a_list, inv)]
        inv = [i + _dot(i, x, passes=passes) for i, x in zip(inv, t)]
    return inv


def _rwkv_scan_kernel(r_ref, k_ref, v_ref, g_ref, lora_ref, dw2_ref, iw2_ref, dw0_ref, iw0_ref,
                      kk_ref, ka_ref, rk_ref, gnw_ref, gnb_ref, z_ref, s_scr, cum_scr, y_scr,
                      *, chunks, pairs, passes):
    L = RWKV_CHUNK
    N = RWKV_HEAD
    R = L * chunks
    P = 2 * L

    @pl.when(pl.program_id(2) == 0)
    def _():
        s_scr[...] = jnp.zeros_like(s_scr)

    lane = _iota2((1, LANES), 1)
    m0 = lane < N
    mf0 = m0.astype(F32)
    mf1 = 1.0 - mf0

    def headsum(x):
        s0 = jnp.sum(jnp.where(m0, x, 0.0), axis=-1, keepdims=True)
        s1 = jnp.sum(jnp.where(m0, 0.0, x), axis=-1, keepdims=True)
        return jnp.where(m0, s0, s1)

    mb0 = mf0.astype(BF16)
    mb1 = mf1.astype(BF16)

    def stack(x):
        xb = x.astype(BF16)
        return jnp.concatenate([xb * mb0, xb * mb1], axis=0)

    lora = lora_ref[...]
    dec = dw0_ref[...] + _dot(lora[:, :LANES], dw2_ref[...], passes=passes)
    lw_all = -jnp.exp(-_softplus(-dec) - 0.5) * LOG2E
    a_all = jax.nn.sigmoid(iw0_ref[...] + _dot(lora[:, LANES:], iw2_ref[...], passes=passes))
    tri = jnp.where(_iota2((L, L), 1) <= _iota2((L, L), 0), 1.0, 0.0).astype(BF16)
    cum_all = jnp.concatenate([_dot_exact_lhs(tri, lw_all[c * L:(c + 1) * L]) for c in range(chunks)], axis=0)
    cum_scr[...] = cum_all

    pair_vals = []
    for p in range(pairs):
        cols = slice(p * LANES, (p + 1) * LANES)
        r = r_ref[:, cols].astype(F32)
        k = k_ref[:, cols].astype(F32)
        v = v_ref[:, cols].astype(F32)
        a = a_all[:, cols]
        kkr = k * kk_ref[:, cols]
        kk = kkr / jnp.maximum(jnp.sqrt(headsum(kkr * kkr)), 1e-12)
        k2 = k * (1.0 + (a - 1.0) * ka_ref[:, cols])
        cum = cum_all[:, cols]
        pair_vals.append(dict(r=r, k2=k2, v=v, av=-kk, bv=kk * a, cum=cum, cumex=cum - lw_all[:, cols]))

    rp = _iota2((P, P), 0)
    cp = _iota2((P, P), 1)
    strict = (rp % L) > (cp % L)
    incl = (rp % L) >= (cp % L)
    eye = jnp.where(rp == cp, 1.0, 0.0)
    blk8 = jnp.where((rp // 8) == (cp // 8), 1.0, 0.0).astype(BF16)
    offdiag = [jnp.where(((rp // (2 * b)) == (cp // (2 * b))) & ((rp // b) != (cp // b)), 1.0, 0.0).astype(BF16)
               for b in (8, 16, 32)]
    fac = {}

    def factor_pass(items):
        pre = []
        for p, c in items:
            pv = pair_vals[p]
            cols = slice(p * LANES, (p + 1) * LANES)
            sl = slice(c * L, (c + 1) * L)
            cref = cum_scr[pl.ds(c * L + L // 2 - 1, 1), cols]
            clast = cum_scr[pl.ds(c * L + L - 1, 1), cols]
            cum_c, cumex_c = pv["cum"][sl], pv["cumex"][sl]
            r_c, k_c, v_c, av_c, bv_c = pv["r"][sl], pv["k2"][sl], pv["v"][sl], pv["av"][sl], pv["bv"][sl]
            e_out = jnp.exp2(cref - cum_c)
            e_end = jnp.exp2(clast - cum_c)
            pre.append(dict(
                lhs1=jnp.concatenate([stack(av_c * jnp.exp2(cumex_c - cref)), stack(r_c * jnp.exp2(cum_c - cref))],
                                     axis=0),
                rhs1=jnp.concatenate([stack(bv_c * e_out), stack(k_c * e_out)], axis=0),
                v_st=stack(v_c), a0_st=stack(av_c * jnp.exp2(cumex_c)), r0_st=stack(r_c * jnp.exp2(cum_c)),
                bk=jnp.concatenate([stack(bv_c * e_end), stack(k_c * e_end)], axis=0), decay=jnp.exp2(clast)))
        x1 = [_dot(f["lhs1"], f["rhs1"], nt=True, passes=passes) for f in pre]
        a_ab = [jnp.where(strict, x[:P, :P], 0.0).astype(BF16) for x in x1]
        a_kr = [jnp.concatenate([jnp.where(strict, x[:P, P:], 0.0), jnp.where(incl, x[P:, P:], 0.0)], axis=0)
                for x in x1]
        a_rb = [jnp.where(incl, x[P:, :P], 0.0) for x in x1]
        m1 = [_dot(a, f["v_st"], passes=passes) for a, f in zip(a_kr, pre)]
        tinv = _unit_lower_inverses(a_ab, eye, blk8, offdiag, passes)
        m2 = [_dot(t, jnp.concatenate([f["a0_st"], m[:P].astype(BF16)], axis=1), passes=passes)
              for t, f, m in zip(tinv, pre, m1)]
        for it, f, m, mm, arb in zip(items, pre, m1, m2, a_rb):
            w_st, uv_st = mm[:, :LANES], mm[:, LANES:]
            fac[it] = dict(wr=jnp.concatenate([w_st.astype(BF16), f["r0_st"]], axis=0), uv_st=uv_st,
                           yv_st=m[P:], a_rb=arb, v_t=f["v_st"].astype(F32).T, decay=f["decay"], bk=f["bk"])

    state = [s_scr[p] for p in range(pairs)]

    def state_pass():
        for c in range(chunks):
            fs = [fac[(p, c)] for p in range(pairs)]
            m3 = [_dot(fs[p]["wr"], state[p], nt=True, passes=passes) for p in range(pairs)]
            u_st = [m3[p][:P] + fs[p]["uv_st"] for p in range(pairs)]
            for p in range(pairs):
                state[p] = state[p] * fs[p]["decay"] + _dot(
                    jnp.concatenate([u_st[p].T, fs[p]["v_t"]], axis=1), fs[p]["bk"], passes=passes)
            for p in range(pairs):
                y_st = m3[p][P:] + fs[p]["yv_st"] + _dot(fs[p]["a_rb"], u_st[p], passes=passes)
                y_scr[c * L:(c + 1) * L, p * LANES:(p + 1) * LANES] = y_st[:L] + y_st[L:]

    factor_pass([(p, c) for p in range(pairs) for c in range(chunks)])
    state_pass()
    for p in range(pairs):
        s_scr[p] = state[p]

    for p in range(pairs):
        cols = slice(p * LANES, (p + 1) * LANES)
        pv = pair_vals[p]
        y = y_scr[:, cols]
        mean = headsum(y) * (1.0 / N)
        yc = y - mean
        var = headsum(yc * yc) * (1.0 / N)
        yn = yc * lax.rsqrt(var + RWKV_GN_EPS) * gnw_ref[:, cols] + gnb_ref[:, cols]
        bonus = headsum(pv["r"] * pv["k2"] * rk_ref[:, cols]) * pv["v"]
        z_ref[:, cols] = ((yn + bonus) * _silu(g_ref[:, cols].astype(F32))).astype(z_ref.dtype)


def _rwkv_scan(rkvg, lora, p, batch, seq, *, out_dtype):
    t = rkvg.shape[0]
    w = D_MODEL
    pairs = RWKV_PAIRS_PER_STEP
    bw = pairs * LANES
    nblk = w // bw
    chunks = min(RWKV_CHUNKS_PER_STEP, seq // RWKV_CHUNK)
    rows = RWKV_CHUNK * chunks
    steps = seq // rows

    def act(col0):
        return pl.BlockSpec((rows, bw), lambda b, h, n: (b * steps + n, col0 + h))

    def vec():
        return pl.BlockSpec((1, bw), lambda b, h, n: (0, h))

    def w2():
        return pl.BlockSpec((LANES, bw), lambda b, h, n: (0, h))

    kern = functools.partial(_rwkv_scan_kernel, chunks=chunks, pairs=pairs, passes=RWKV_PASSES)
    return pl.pallas_call(
        kern,
        grid=(batch, nblk, steps),
        in_specs=[act(0), act(nblk), act(2 * nblk), act(3 * nblk),
                  pl.BlockSpec((rows, 2 * LANES), lambda b, h, n: (b * steps + n, 0)),
                  w2(), w2(), vec(), vec(), vec(), vec(), vec(), vec(), vec()],
        out_specs=pl.BlockSpec((rows, bw), lambda b, h, n: (b * steps + n, h)),
        out_shape=jax.ShapeDtypeStruct((t, w), out_dtype),
        scratch_shapes=[pltpu.VMEM((pairs, LANES, LANES), F32), pltpu.VMEM((rows, bw), F32),
                        pltpu.VMEM((rows, bw), F32)],
        compiler_params=_cparams(("arbitrary", "arbitrary", "arbitrary")),
        name="rwkv_scan",
    )(rkvg, rkvg, rkvg, rkvg, lora, p["dec_w2"], p["iclr_w2"], p["dec_w0"], p["iclr_w0"],
      p["k_k"], p["k_a"], p["r_k"], p["gn_w"], p["gn_b"])


def _gla_kernel(q_ref, k_ref, v_ref, g_ref, low_ref, w2_ref, b_ref, hg_ref, o_ref, s_scr, *, blocks):
    R = GLA_BLOCK
    Hf = R // 2

    @pl.when(pl.program_id(1) == 0)
    def _():
        s_scr[...] = jnp.zeros_like(s_scr)

    DK, DV = GLA_HEAD_K, GLA_HEAD_V
    heads = range(GLA_HEADS)
    items = [(h, b) for b in range(blocks) for h in heads]
    ri = _iota2((R, R), 0)
    ci = _iota2((R, R), 1)
    tri = jnp.where(ci <= ri, 1.0, 0.0).astype(BF16)
    rh = _iota2((Hf, Hf), 0)
    ch = _iota2((Hf, Hf), 1)
    causal = ch <= rh

    def rows(b):
        return slice(b * R, (b + 1) * R)

    la = -_softplus(-(_dot(low_ref[...], w2_ref[...], passes=GATE_PASSES) + b_ref[...])) * (LOG2E / GLA_GATE_TAU)
    bcum = [_dot_exact_lhs(tri, la[rows(b)]) for b in range(blocks)]
    q = {(h, b): q_ref[rows(b), h * DK:(h + 1) * DK].astype(F32) * (DK ** -0.5) for h, b in items}
    k = {(h, b): k_ref[rows(b), h * DK:(h + 1) * DK].astype(F32) for h, b in items}
    v = {(h, b): v_ref[rows(b), h * DV:(h + 1) * DV].astype(F32) for h, b in items}
    bc = {(h, b): bcum[b][:, h * DK:(h + 1) * DK] for h, b in items}
    ref_t = {it: bc[it][Hf // 2 - 1:Hf // 2] for it in items}
    ref_m = {it: bc[it][Hf - 1:Hf] for it in items}
    ref_b = {it: bc[it][Hf + Hf // 2 - 1:Hf + Hf // 2] for it in items}
    last = {it: bc[it][R - 1:R] for it in items}
    s00 = {it: jnp.where(causal, _dot(q[it][:Hf] * jnp.exp2(bc[it][:Hf] - ref_t[it]),
                                      k[it][:Hf] * jnp.exp2(ref_t[it] - bc[it][:Hf]), nt=True), 0.0) for it in items}
    s11 = {it: jnp.where(causal, _dot(q[it][Hf:] * jnp.exp2(bc[it][Hf:] - ref_b[it]),
                                      k[it][Hf:] * jnp.exp2(ref_b[it] - bc[it][Hf:]), nt=True), 0.0) for it in items}
    s10 = {it: _dot(q[it][Hf:] * jnp.exp2(bc[it][Hf:] - ref_m[it]), k[it][:Hf] * jnp.exp2(ref_m[it] - bc[it][:Hf]),
                    nt=True) for it in items}
    o_top = {it: _dot(s00[it], v[it][:Hf]) for it in items}
    o_bot = {it: _dot(s10[it], v[it][:Hf]) + _dot(s11[it], v[it][Hf:]) for it in items}
    upd = {it: _dot(v[it].T, k[it] * jnp.exp2(last[it] - bc[it])) for it in items}
    st = [s_scr[h] for h in heads]
    for b in range(blocks):
        o_int = [_dot(q[(h, b)] * jnp.exp2(bc[(h, b)]), st[h], nt=True) for h in heads]
        st = [st[h] * jnp.exp2(last[(h, b)]) + upd[(h, b)] for h in heads]
        for h in heads:
            o = jnp.concatenate([o_top[(h, b)], o_bot[(h, b)]], axis=0) + o_int[h]
            ms = jnp.mean(o * o, axis=-1, keepdims=True)
            on = o * lax.rsqrt(ms + NORM_EPS) * hg_ref[...]
            cols = slice(h * DV, (h + 1) * DV)
            o_ref[rows(b), cols] = (on * _silu(g_ref[rows(b), cols].astype(F32))).astype(o_ref.dtype)
    for h in heads:
        s_scr[h] = st[h]


def _gla_scan(proj, low, gate_w2, gate_b, head_g, batch, seq, *, out_dtype):
    t = proj.shape[0]
    blocks = min(GLA_BLOCKS_PER_STEP, seq // GLA_BLOCK)
    R = GLA_BLOCK * blocks
    steps = seq // R
    kw, vw = GLA_KEY_WIDTH, GLA_VALUE_WIDTH

    def row(b, n):
        return b * steps + n

    return pl.pallas_call(
        functools.partial(_gla_kernel, blocks=blocks),
        grid=(batch, steps),
        in_specs=[
            pl.BlockSpec((R, kw), lambda b, n: (row(b, n), 0)),
            pl.BlockSpec((R, kw), lambda b, n: (row(b, n), 1)),
            pl.BlockSpec((R, vw), lambda b, n: (row(b, n), 2 * kw // vw)),
            pl.BlockSpec((R, vw), lambda b, n: (row(b, n), 2 * kw // vw + 1)),
            pl.BlockSpec((R, LANES), lambda b, n: (row(b, n), 0)),
            pl.BlockSpec((LANES, kw), lambda b, n: (0, 0)),
            pl.BlockSpec((1, kw), lambda b, n: (0, 0)),
            pl.BlockSpec((1, GLA_HEAD_V), lambda b, n: (0, 0)),
        ],
        out_specs=pl.BlockSpec((R, vw), lambda b, n: (row(b, n), 0)),
        out_shape=jax.ShapeDtypeStruct((t, vw), out_dtype),
        scratch_shapes=[pltpu.VMEM((GLA_HEADS, GLA_HEAD_V, GLA_HEAD_K), F32)],
        compiler_params=_cparams(("arbitrary", "arbitrary")),
        name="gla_scan",
    )(proj, proj, proj, proj, low, gate_w2, gate_b, head_g)


def _ssd_kernel(z_ref, xr_ref, br_ref, cr_ref, dt_ref, cw_ref, cbias_ref, dtb_ref, alog_ref, dsk_ref, ng_ref, o_ref,
                st_scr, raw_scr, xs_ref, bm_ref, cm_ref):
    C = SSD_CHUNK
    P = SSM_HEADDIM
    pairs_per_group = (SSM_HEADS // SSM_GROUPS) // 2
    gw = SSM_WIDTH // SSM_GROUPS

    @pl.when(pl.program_id(1) == 0)
    def _():
        st_scr[...] = jnp.zeros_like(st_scr)
        raw_scr[...] = jnp.zeros_like(raw_scr)

    raw_scr[0:SUBLANES, :] = raw_scr[C:C + SUBLANES, :]
    col = 0
    for src_ref, act_ref in ((xr_ref, xs_ref), (br_ref, bm_ref), (cr_ref, cm_ref)):
        width = src_ref.shape[1]
        cols = slice(col, col + width)
        raw_scr[SUBLANES:, cols] = src_ref[...].astype(F32)
        acc = cbias_ref[:, cols]
        for s in range(SSM_CONV):
            tap = raw_scr[SUBLANES - s:SUBLANES - s + C, cols]
            acc = acc + tap * cw_ref[SSM_CONV - 1 - s:SSM_CONV - s, cols]
        act_ref[...] = _silu(acc)
        col += width

    dt = _softplus(dt_ref[...] + dtb_ref[...])
    da = dt * (-LOG2E * jnp.exp(alog_ref[...]))
    ri = _iota2((C, C), 0)
    ci = _iota2((C, C), 1)
    causal = ci <= ri
    tri = jnp.where(causal, 1.0, 0.0).astype(BF16)
    acum = _dot_exact_lhs(tri, da)
    acum_t = acum.T
    alast = acum[C - 1:C, :]
    dsk = dsk_ref[...]

    lane = _iota2((1, LANES), 1)
    m0 = lane < P
    mb0 = m0.astype(BF16)
    mb1 = 1 - mb0

    def pair_cols(x, h0):
        return jnp.where(m0, x[:, h0:h0 + 1], x[:, h0 + 1:h0 + 2])

    for g in range(SSM_GROUPS):
        bm = bm_ref[:, g * SSM_STATE:(g + 1) * SSM_STATE].astype(F32)
        cm = cm_ref[:, g * SSM_STATE:(g + 1) * SSM_STATE].astype(F32)
        cb = _dot(cm, bm, nt=True)
        bm_t = bm.T
        ys = []
        for pp in range(pairs_per_group):
            pidx = g * pairs_per_group + pp
            h0 = 2 * pidx
            cols = slice(pidx * LANES, (pidx + 1) * LANES)
            x_p = xs_ref[:, cols].astype(F32)
            dt_p = pair_cols(dt, h0)
            ac_p = pair_cols(acum, h0)
            al_p = pair_cols(alast, h0)
            xc = x_p * dt_p
            dec0 = jnp.where(causal, jnp.exp2(acum[:, h0:h0 + 1] - acum_t[h0:h0 + 1, :]), 0.0)
            dec1 = jnp.where(causal, jnp.exp2(acum[:, h0 + 1:h0 + 2] - acum_t[h0 + 1:h0 + 2, :]), 0.0)
            lhs = jnp.concatenate([cb * dec0, cb * dec1], axis=1)
            xcb = xc.astype(BF16)
            rhs = jnp.concatenate([xcb * mb0, xcb * mb1], axis=0)
            prev = st_scr[pidx]
            y = _dot(lhs, rhs) + _dot(cm, prev) * jnp.exp2(ac_p) + pair_cols(dsk, h0) * x_p
            st_scr[pidx] = prev * jnp.exp2(al_p) + _dot(bm_t, xc * jnp.exp2(al_p - ac_p))
            ys.append(y * _silu(z_ref[:, cols].astype(F32)))
        yg = jnp.concatenate(ys, axis=1)
        ms = jnp.mean(yg * yg, axis=-1, keepdims=True)
        o_ref[:, g * gw:(g + 1) * gw] = (yg * lax.rsqrt(ms + SSM_NORM_EPS) * ng_ref[:, g * gw:(g + 1) * gw]).astype(o_ref.dtype)


def _ssd_scan(proj, dt, conv_w, conv_b, dt_bias, a_log, d_skip, norm_g, batch, seq, *, out_dtype):
    t = proj.shape[0]
    C = SSD_CHUNK
    steps = seq // C
    npairs = SSM_HEADS // 2

    def row(b, n):
        return b * steps + n

    bcol = 2 * SSM_WIDTH // SSM_BC_WIDTH
    return pl.pallas_call(
        _ssd_kernel,
        grid=(batch, steps),
        in_specs=[
            pl.BlockSpec((C, SSM_WIDTH), lambda b, n: (row(b, n), 0)),
            pl.BlockSpec((C, SSM_WIDTH), lambda b, n: (row(b, n), 1)),
            pl.BlockSpec((C, SSM_BC_WIDTH), lambda b, n: (row(b, n), bcol)),
            pl.BlockSpec((C, SSM_BC_WIDTH), lambda b, n: (row(b, n), bcol + 1)),
            pl.BlockSpec((C, LANES), lambda b, n: (row(b, n), 0)),
            pl.BlockSpec((SSM_CONV, SSM_CONV_WIDTH), lambda b, n: (0, 0)),
            pl.BlockSpec((1, SSM_CONV_WIDTH), lambda b, n: (0, 0)),
            pl.BlockSpec((1, LANES), lambda b, n: (0, 0)),
            pl.BlockSpec((1, LANES), lambda b, n: (0, 0)),
            pl.BlockSpec((1, LANES), lambda b, n: (0, 0)),
            pl.BlockSpec((1, SSM_WIDTH), lambda b, n: (0, 0)),
        ],
        out_specs=pl.BlockSpec((C, SSM_WIDTH), lambda b, n: (row(b, n), 0)),
        out_shape=jax.ShapeDtypeStruct((t, SSM_WIDTH), out_dtype),
        scratch_shapes=[pltpu.VMEM((npairs, SSM_STATE, LANES), F32), pltpu.VMEM((C + SUBLANES, SSM_CONV_WIDTH), F32),
                        pltpu.VMEM((C, SSM_WIDTH), F32), pltpu.VMEM((C, SSM_BC_WIDTH), F32),
                        pltpu.VMEM((C, SSM_BC_WIDTH), F32)],
        compiler_params=_cparams(("arbitrary", "arbitrary")),
        name="ssd_scan",
    )(proj, proj, proj, proj, dt, conv_w, conv_b, dt_bias, a_log, d_skip, norm_g)


def _pad_cols(w, n):
    return jnp.pad(w, ((0, 0), (0, n - w.shape[1])))


def _pad_rows(w, n):
    return jnp.pad(w, ((0, n - w.shape[0]), (0, 0)))


ACT_DTYPE = BF16


def kernel(x, c, ada_w, ada_b, norm_g, final_g, rwkv_mu, rwkv_w_in, rwkv_dec_w1, rwkv_dec_w2, rwkv_dec_w0, rwkv_iclr_w1, rwkv_iclr_w2, rwkv_iclr_w0, rwkv_k_k, rwkv_k_a, rwkv_r_k, rwkv_gn_w, rwkv_gn_b, rwkv_w_out, gla_w_in, gla_gate_w2, gla_gate_b, gla_head_g, gla_w_out, ssd_w_in, ssd_conv_w, ssd_conv_b, ssd_dt_bias, ssd_a_log, ssd_d, ssd_norm_g, ssd_w_out):
    batch, seq, d = x.shape
    t = batch * seq
    xf = x.reshape(t, d)

    c_pad = jnp.pad(c, ((0, SUBLANES - batch % SUBLANES if batch % SUBLANES else 0), (0, 0)))
    mod = _ada_mod(c_pad, ada_w, ada_b)[:, :batch].reshape(DEPTH, batch, 3, d)

    for i in range(DEPTH):
        kind, j = i % N_MIXERS, i // N_MIXERS
        g = norm_g[i].reshape(1, d)
        mod_l = mod[i]
        if kind == 0:
            w1 = jnp.concatenate([_pad_cols(rwkv_dec_w1[j], LANES), _pad_cols(rwkv_iclr_w1[j], LANES)], axis=1)
            xs, lora = _rwkv_prenorm(xf, g, mod_l, rwkv_mu[j], w1.astype(BF16), seq, tm=ROW_TILE)
            rkvg = _proj(xs, rwkv_w_in, j, 4 * d, seq, tm=PROJ_ROW_TILE, tn=1024, out_dtype=ACT_DTYPE)
            row = lambda v: v.reshape(1, -1)
            params = dict(dec_w2=_pad_rows(rwkv_dec_w2[j], LANES), iclr_w2=_pad_rows(rwkv_iclr_w2[j], LANES),
                          dec_w0=row(rwkv_dec_w0[j]), iclr_w0=row(rwkv_iclr_w0[j]), k_k=row(rwkv_k_k[j]),
                          k_a=row(rwkv_k_a[j]), r_k=row(rwkv_r_k[j]), gn_w=row(rwkv_gn_w[j]), gn_b=row(rwkv_gn_b[j]))
            z = _rwkv_scan(rkvg, lora, params, batch, seq, out_dtype=ACT_DTYPE)
            xf = _outproj(z, rwkv_w_out, j, xf, mod_l, seq, tm=PROJ_ROW_TILE, tn=1024)
        elif kind == 1:
            nmain = 2 * GLA_KEY_WIDTH + 2 * GLA_VALUE_WIDTH
            w_bf = gla_w_in[j:j + 1].astype(BF16)
            h, low = _prenorm_call(xf, g, mod_l, _pad_cols(w_bf[0, :, nmain:], LANES), seq, tm=ROW_TILE)
            proj = _proj(h[None], w_bf, 0, nmain, seq, tm=PROJ_ROW_TILE, tn=1024, out_dtype=ACT_DTYPE)
            z = _gla_scan(proj, low, _pad_rows(gla_gate_w2[j], LANES), gla_gate_b[j].reshape(1, -1),
                          gla_head_g[j].reshape(1, -1), batch, seq, out_dtype=ACT_DTYPE)
            xf = _outproj(z, gla_w_out, j, xf, mod_l, seq, tm=PROJ_ROW_TILE, tn=1024)
        else:
            nmain = SSM_WIDTH + SSM_CONV_WIDTH
            w_bf = ssd_w_in[j:j + 1].astype(BF16)
            h, dt = _prenorm_call(xf, g, mod_l, _pad_cols(w_bf[0, :, nmain:], LANES), seq, tm=ROW_TILE)
            proj = _proj(h[None], w_bf, 0, nmain, seq, tm=PROJ_ROW_TILE, tn=1024, out_dtype=ACT_DTYPE)
            padl = lambda v: _pad_cols(v.reshape(1, -1), LANES)
            z = _ssd_scan(proj, dt, ssd_conv_w[j], ssd_conv_b[j].reshape(1, -1),
                          padl(ssd_dt_bias[j]), padl(ssd_a_log[j]), padl(ssd_d[j]),
                          ssd_norm_g[j].reshape(1, -1), batch, seq, out_dtype=ACT_DTYPE)
            xf = _outproj(z, ssd_w_out, j, xf, mod_l, seq, tm=PROJ_ROW_TILE, tn=512)

    out = _final_norm(xf, final_g.reshape(1, d))
    return out.reshape(batch, seq, d)
```
